```python
import math
import jax, jax.numpy as jnp
from jax import lax
import numpy as np


D_MODEL = 1024
BATCH = 8
SEQ = 4096
DEPTH = 4

WIDTH_A = D_MODEL // 2
HEAD_DIM_A = 64
N_HEADS_A = WIDTH_A // HEAD_DIM_A
DILATED_BRANCHES = ((128, 1), (512, 4), (2048, 16))
ROPE_THETA = 10000.0

WIDTH_R = D_MODEL // 4
N_HEADS_R = 4
V_DIM_R = WIDTH_R // N_HEADS_R
QK_DIM_R = V_DIM_R // 2
RET_CHUNK = 128

WIDTH_G = D_MODEL // 4
N_HEADS_G = 4
V_DIM_G = WIDTH_G // N_HEADS_G
QK_DIM_G = V_DIM_G // 2
GLA_LOW_RANK = 16
GLA_TAU = 16.0
GLA_CHUNK = 64

MIX_WIDTH = WIDTH_A + WIDTH_R + WIDTH_G
PROJ_SIZES = (WIDTH_A, WIDTH_A, WIDTH_A,
              N_HEADS_R * QK_DIM_R, N_HEADS_R * QK_DIM_R, WIDTH_R, WIDTH_R,
              N_HEADS_G * QK_DIM_G, N_HEADS_G * QK_DIM_G, WIDTH_G, WIDTH_G, GLA_LOW_RANK)
PROJ_WIDTH = sum(PROJ_SIZES)

D_FF = 128 * ((8 * D_MODEL // 3 + 127) // 128)
CONV_WIDTH = 3

DEEPNORM_ALPHA = (2 * DEPTH) ** 0.25
DEEPNORM_BETA = (8 * DEPTH) ** -0.25
LN_EPS = 1e-5
HEAD_NORM_EPS = 1e-6

kernel_name = "hymba_style_dilated_retnet_gla_convglu"


def layer_norm(x, g, b):
    xf = x.astype(jnp.float32)
    mu = xf.mean(-1, keepdims=True)
    var = jnp.square(xf - mu).mean(-1, keepdims=True)
    y = (xf - mu) * lax.rsqrt(var + LN_EPS)
    return (y * g.astype(jnp.float32) + b.astype(jnp.float32)).astype(x.dtype)


def head_norm(t):
    t = t.astype(jnp.float32)
    mu = t.mean(-1, keepdims=True)
    var = jnp.square(t - mu).mean(-1, keepdims=True)
    return (t - mu) * lax.rsqrt(var + HEAD_NORM_EPS)


def split_heads(t, n_heads):
    b, s, _ = t.shape
    return t.reshape(b, s, n_heads, -1).transpose(0, 2, 1, 3)


def merge_heads(t):
    b, h, s, d = t.shape
    return t.transpose(0, 2, 1, 3).reshape(b, s, h * d)


def rope_tables(seq, dim):
    inv = 1.0 / (ROPE_THETA ** (jnp.arange(0, dim, 2, dtype=jnp.float32) / dim))
    ang = jnp.arange(seq, dtype=jnp.float32)[:, None] * inv[None, :]
    return jnp.cos(ang), jnp.sin(ang)


def apply_rope(t, cos, sin):
    c = cos.astype(t.dtype)
    s = sin.astype(t.dtype)
    t1, t2 = jnp.split(t, 2, axis=-1)
    return jnp.concatenate([t1 * c - t2 * s, t1 * s + t2 * c], axis=-1)


def dilated_branch(q, k, v, window, dilation):
    b, h, s, d = q.shape
    blk = window // dilation
    unit = blk * dilation
    s_pad = -(-s // unit) * unit
    nb = s_pad // unit

    def to_blocks(t):
        t = jnp.pad(t, ((0, 0), (0, 0), (0, s_pad - s), (0, 0)))
        t = t.reshape(b, h, nb, blk, dilation, d)
        return t.transpose(0, 1, 4, 2, 3, 5)

    def with_prev(t):
        prev = jnp.pad(t, ((0, 0), (0, 0), (0, 0), (1, 0), (0, 0), (0, 0)))[:, :, :, :-1]
        return jnp.concatenate([prev, t], axis=4)

    def from_blocks(t):
        t = jnp.moveaxis(t, 2, 4)
        return t.reshape(b, h, s_pad, *t.shape[5:])[:, :, :s]

    qb = to_blocks(q)
    kk = with_prev(to_blocks(k))
    vv = with_prev(to_blocks(v)).astype(jnp.float32)
    scores = jnp.einsum('bhrnqd,bhrnkd->bhrnqk', qb, kk).astype(jnp.float32) * (d ** -0.5)
    qi = jnp.arange(blk)[:, None]
    kj = jnp.arange(2 * blk)[None, :]
    band = (kj >= qi) & (kj <= qi + blk)
    has_prev = (jnp.arange(nb) > 0)[:, None, None] | (kj >= blk)[None]
    mask = band[None] & has_prev
    scores = jnp.where(mask, scores, -jnp.inf)
    m = scores.max(-1)
    p = jnp.exp(scores - m[..., None])
    l = p.sum(-1)
    o = jnp.einsum('bhrnqk,bhrnkd->bhrnqd', p, vv) / l[..., None]
    return from_blocks(o), from_blocks(m), from_blocks(l)


def dilated_attention(q, k, v):
    outs, maxes, denoms = zip(*[dilated_branch(q, k, v, w, r) for (w, r) in DILATED_BRANCHES])
    m_all = jnp.stack(maxes)
    l_all = jnp.stack(denoms)
    o_all = jnp.stack(outs)
    wts = l_all * jnp.exp(m_all - m_all.max(0))
    wts = wts / wts.sum(0)
    return jnp.einsum('gbhs,gbhsd->bhsd', wts, o_all)


def retention(q, k, v):
    b, h, s, dk = q.shape
    dv = v.shape[-1]
    c = RET_CHUNK
    n = s // c
    lg = jnp.log(1.0 - jnp.power(2.0, -5.0 - jnp.arange(h, dtype=jnp.float32)))
    idx = jnp.arange(c, dtype=jnp.float32)
    dist = idx[:, None] - idx[None, :]
    decay_intra = jnp.where(dist >= 0, jnp.exp(lg[:, None, None] * jnp.maximum(dist, 0.0)), 0.0)
    q_dec = jnp.exp(lg[:, None] * (idx + 1.0))
    k_dec = jnp.exp(lg[:, None] * (c - 1.0 - idx))
    chunk_dec = jnp.exp(lg * c)
    qc = q.reshape(b, h, n, c, dk)
    kc = k.reshape(b, h, n, c, dk)
    vc = v.reshape(b, h, n, c, dv)
    scores = jnp.einsum('bhncd,bhnsd->bhncs', qc, kc) * decay_intra[:, None]
    intra = jnp.einsum('bhncs,bhnse->bhnce', scores, vc)
    kv = jnp.einsum('bhnsd,bhnse->nbhde', kc * k_dec[:, None, :, None], vc)

    def step(state, kv_n):
        return chunk_dec[:, None, None] * state + kv_n, state

    _, prev = lax.scan(step, jnp.zeros((b, h, dk, dv), jnp.float32), kv)
    inter = jnp.einsum('bhncd,nbhde->bhnce', qc * q_dec[:, None, :, None], prev)
    return (intra + inter).reshape(b, h, s, dv)


def gated_linear_attention(q, k, v, log_alpha):
    b, h, s, dk = q.shape
    dv = v.shape[-1]
    c = GLA_CHUNK
    n = s // c
    qc = q.reshape(b, h, n, c, dk)
    kc = k.reshape(b, h, n, c, dk)
    vc = v.reshape(b, h, n, c, dv)
    cum = jnp.cumsum(log_alpha.reshape(b, h, n, c, dk), axis=3)
    q_t = qc * jnp.exp(cum)
    k_t = kc * jnp.exp(-cum)
    causal = jnp.tril(jnp.ones((c, c), dtype=bool))
    att = jnp.where(causal, jnp.einsum('bhncd,bhnsd->bhncs', q_t, k_t), 0.0)
    intra = jnp.einsum('bhncs,bhnse->bhnce', att, vc)
    last = cum[:, :, :, -1:]
    kv = jnp.einsum('bhnsd,bhnse->nbhde', kc * jnp.exp(last - cum), vc)
    chunk_dec = jnp.moveaxis(jnp.exp(last[:, :, :, 0]), 2, 0)

    def step(state, xs):
        dec, kv_n = xs
        return dec[..., None] * state + kv_n, state

    _, prev = lax.scan(step, jnp.zeros((b, h, dk, dv), jnp.float32), (chunk_dec, kv))
    inter = jnp.einsum('bhncd,nbhde->bhnce', q_t, prev)
    return (intra + inter).reshape(b, h, s, dv)


def hybrid_mixer(x, w_in, w_alpha, b_alpha, mix_scale, w_out, cos_a, sin_a, cos_r, sin_r):
    f32 = jnp.float32
    proj = x @ w_in
    split_points = np.cumsum(PROJ_SIZES)[:-1].tolist()
    qa, ka, va, qr, kr, vr, gr, qg, kg, vg, rg, ag = jnp.split(proj, split_points, axis=-1)

    qa = apply_rope(split_heads(qa, N_HEADS_A), cos_a, sin_a)
    ka = apply_rope(split_heads(ka, N_HEADS_A), cos_a, sin_a)
    va = split_heads(va, N_HEADS_A)
    ya = merge_heads(head_norm(dilated_attention(qa, ka, va)))

    qr = apply_rope(split_heads(qr, N_HEADS_R), cos_r, sin_r).astype(f32)
    kr = apply_rope(split_heads(kr, N_HEADS_R), cos_r, sin_r).astype(f32) * (QK_DIM_R ** -0.5)
    vr = split_heads(vr, N_HEADS_R).astype(f32)
    yr = merge_heads(head_norm(retention(qr, kr, vr))) * jax.nn.silu(gr.astype(f32))

    log_alpha = jax.nn.log_sigmoid((ag @ w_alpha + b_alpha).astype(f32)) / GLA_TAU
    qg = split_heads(qg, N_HEADS_G).astype(f32) * (QK_DIM_G ** -0.5)
    kg = split_heads(kg, N_HEADS_G).astype(f32)
    vg = split_heads(vg, N_HEADS_G).astype(f32)
    og = gated_linear_attention(qg, kg, vg, split_heads(log_alpha, N_HEADS_G))
    yg = merge_heads(head_norm(og)) * jax.nn.silu(rg.astype(f32))

    y = jnp.concatenate([ya, yr, yg], axis=-1) * mix_scale.astype(f32)
    return y.astype(x.dtype) @ w_out


def causal_dwconv(u, w, bias):
    ch = u.shape[-1]
    y = lax.conv_general_dilated(u, w[:, None, :].astype(u.dtype), window_strides=(1,),
                                 padding=((CONV_WIDTH - 1, 0),),
                                 dimension_numbers=('NWC', 'WIO', 'NWC'),
                                 feature_group_count=ch)
    return y + bias.astype(u.dtype)


def conv_glu_ffn(x, w_up, conv_w, conv_b, w_down):
    u = causal_dwconv(x @ w_up, conv_w, conv_b)
    gate, val = jnp.split(u, 2, axis=-1)
    return (jax.nn.silu(gate) * val) @ w_down


def setup_inputs(seed: int = 0) -> dict:
    key = jax.random.key(seed)
    ks = jax.random.split(key, 15)
    f32 = jnp.float32

    def nrm(k, shape, scale):
        return jax.random.normal(k, shape, f32) * scale

    hg = N_HEADS_G * QK_DIM_G
    return {
        "x": nrm(ks[0], (BATCH, SEQ, D_MODEL), 1.0),
        "w_in": nrm(ks[1], (DEPTH, D_MODEL, PROJ_WIDTH), D_MODEL ** -0.5),
        "w_alpha": nrm(ks[2], (DEPTH, GLA_LOW_RANK, hg), GLA_LOW_RANK ** -0.5),
        "b_alpha": nrm(ks[3], (DEPTH, hg), 0.1),
        "mix_scale": 1.0 + nrm(ks[4], (DEPTH, MIX_WIDTH), 0.02),
        "w_out": nrm(ks[5], (DEPTH, MIX_WIDTH, D_MODEL), MIX_WIDTH ** -0.5 * DEEPNORM_BETA),
        "ln1_g": 1.0 + nrm(ks[6], (DEPTH, D_MODEL), 0.02),
        "ln1_b": nrm(ks[7], (DEPTH, D_MODEL), 0.02),
        "w_up": nrm(ks[8], (DEPTH, D_MODEL, 2 * D_FF), D_MODEL ** -0.5),
        "conv_w": nrm(ks[9], (DEPTH, CONV_WIDTH, 2 * D_FF), CONV_WIDTH ** -0.5),
        "conv_b": nrm(ks[10], (DEPTH, 2 * D_FF), 0.02),
        "w_down": nrm(ks[11], (DEPTH, D_FF, D_MODEL), D_FF ** -0.5 * DEEPNORM_BETA),
        "ln2_g": 1.0 + nrm(ks[12], (DEPTH, D_MODEL), 0.02),
        "ln2_b": nrm(ks[13], (DEPTH, D_MODEL), 0.02),
    }


def reference(x, w_in, w_alpha, b_alpha, mix_scale, w_out, ln1_g, ln1_b,
              w_up, conv_w, conv_b, w_down, ln2_g, ln2_b):
    seq = x.shape[1]
    cos_a, sin_a = rope_tables(seq, HEAD_DIM_A)
    cos_r, sin_r = rope_tables(seq, QK_DIM_R)
    for l in range(DEPTH):
        mix = hybrid_mixer(x, w_in[l], w_alpha[l], b_alpha[l], mix_scale[l], w_out[l],
                           cos_a, sin_a, cos_r, sin_r)
        x = layer_norm(DEEPNORM_ALPHA * x + mix, ln1_g[l], ln1_b[l])
        ffn = conv_glu_ffn(x, w_up[l], conv_w[l], conv_b[l], w_down[l])
        x = layer_norm(DEEPNORM_ALPHA * x + ffn, ln2_g[l], ln2_b[l])
    return x
```

```python
import functools

import jax
import jax.numpy as jnp
import numpy as np
from jax import lax
from jax.experimental import pallas as pl
from jax.experimental.pallas import tpu as pltpu

F32 = jnp.float32
BF16 = jnp.bfloat16

D_MODEL = 1024
DEPTH = 4
HEAD_DIM_A = 64
N_HEADS_A = 8
WIDTH_A = 512
DILATIONS = (1, 4, 16)
ATT_BLK = 128
ROPE_THETA = 10000.0
N_HEADS_R = 4
QK_DIM_R = 32
V_DIM_R = 64
RET_CHUNK = 128
N_HEADS_G = 4
QK_DIM_G = 32
V_DIM_G = 64
GLA_LOW_RANK = 16
GLA_TAU = 16.0
GLA_CHUNK = 64
GLA_BLOCK = 256
D_FF = 2816
FF_CHUNK = 256
N_FF_CHUNKS = D_FF // FF_CHUNK
DEEPNORM_ALPHA = (2 * DEPTH) ** 0.25
LN_EPS = 1e-5
HEAD_NORM_EPS = 1e-6
LANES = 128
ROW_TILE = 512
VMEM_LIMIT = 56 * 1024 * 1024


def _iota(shape, dim):
    return lax.broadcasted_iota(jnp.int32, shape, dim)


def _dot(a, b):
    return jnp.dot(a, b, preferred_element_type=F32)


def _dot_nt(a, b):
    return lax.dot_general(a, b, (((1,), (1,)), ((), ())), preferred_element_type=F32)


def _dot_tn(a, b):
    return lax.dot_general(a, b, (((0,), (0,)), ((), ())), preferred_element_type=F32)


def _split_hi_lo(x):
    hi = x.astype(BF16)
    lo = (x - hi.astype(F32)).astype(BF16)
    return hi, lo


def _group_mean(x, ones_bd, group):
    hi, lo = _split_hi_lo(x)
    return (_dot(hi, ones_bd) + _dot(lo, ones_bd)) * (1.0 / group)


def _head_norm(x, ones_bd, group):
    mu = _group_mean(x, ones_bd, group)
    d = x - mu
    var = _group_mean(d * d, ones_bd, group)
    return d * lax.rsqrt(var + HEAD_NORM_EPS)


def _silu(g):
    return g * (1.0 / (1.0 + jnp.exp(-g)))


def _layer_norm(z, g, b):
    mu = jnp.mean(z, axis=-1, keepdims=True)
    d = z - mu
    var = jnp.mean(d * d, axis=-1, keepdims=True)
    return d * lax.rsqrt(var + LN_EPS) * g + b


def _rope(t, cos, sin_signed, half):
    lane = _iota(t.shape, 1)
    first = (lane % (2 * half)) < half
    partner = jnp.where(first, pltpu.roll(t, LANES - half, 1), pltpu.roll(t, half, 1))
    return t * cos + partner * sin_signed


def _inproj_kernel(x_ref, w_ref, wag_ref, walpha_ref, balpha_ref, ca_ref, sa_ref, cr_ref, sr_ref,
                   qa_ref, ka_ref, va_ref, r_ref, g_ref, la_ref):
    xb = x_ref[...].astype(BF16)

    def mm(c0, n):
        return _dot(xb, w_ref[:, c0:c0 + n])

    ca = ca_ref[...]
    sa = sa_ref[...]
    for s in range(WIDTH_A // LANES):
        sl = slice(s * LANES, (s + 1) * LANES)
        qa_ref[:, sl] = _rope(mm(s * LANES, LANES), ca, sa, HEAD_DIM_A // 2).astype(BF16)
        ka_ref[:, sl] = _rope(mm(WIDTH_A + s * LANES, LANES), ca, sa, HEAD_DIM_A // 2).astype(BF16)
    va_ref[...] = mm(2 * WIDTH_A, WIDTH_A).astype(BF16)

    base = 3 * WIDTH_A
    cr = cr_ref[...]
    sr = sr_ref[...]
    r_ref[:, 0:128] = _rope(mm(base, 128), cr, sr, QK_DIM_R // 2).astype(BF16)
    r_ref[:, 128:256] = (_rope(mm(base + 128, 128), cr, sr, QK_DIM_R // 2) * (QK_DIM_R ** -0.5)).astype(BF16)
    r_ref[:, 256:768] = mm(base + 256, 512).astype(BF16)

    base += 768
    g_ref[:, 0:128] = (mm(base, 128) * (QK_DIM_G ** -0.5)).astype(BF16)
    g_ref[:, 128:768] = mm(base + 128, 640).astype(BF16)

    ag = _dot(xb, wag_ref[...]).astype(BF16)
    z = _dot(ag, walpha_ref[...]) + balpha_ref[...]
    log_sig = jnp.minimum(z, 0.0) - jnp.log(1.0 + jnp.exp(-jnp.abs(z)))
    la_ref[...] = log_sig * (1.0 / GLA_TAU)


def _inproj(x2d, w_main, w_ag, w_alpha, b_alpha, ca, sa, cr, sr, seq):
    t = x2d.shape[0]
    tm = ROW_TILE
    n_seq_tiles = seq // tm
    row = lambda i: (i, 0)
    fixed = lambda i: (0, 0)
    tab = lambda i: (i % n_seq_tiles, 0)
    return pl.pallas_call(
        _inproj_kernel,
        grid=(t // tm,),
        in_specs=[
            pl.BlockSpec((tm, D_MODEL), row),
            pl.BlockSpec(w_main.shape, fixed),
            pl.BlockSpec(w_ag.shape, fixed),
            pl.BlockSpec(w_alpha.shape, fixed),
            pl.BlockSpec(b_alpha.shape, fixed),
            pl.BlockSpec((tm, LANES), tab),
            pl.BlockSpec((tm, LANES), tab),
            pl.BlockSpec((tm, LANES), tab),
            pl.BlockSpec((tm, LANES), tab),
        ],
        out_specs=[
            pl.BlockSpec((tm, WIDTH_A), row),
            pl.BlockSpec((tm, WIDTH_A), row),
            pl.BlockSpec((tm, WIDTH_A), row),
            pl.BlockSpec((tm, 768), row),
            pl.BlockSpec((tm, 768), row),
            pl.BlockSpec((tm, LANES), row),
        ],
        out_shape=[
            jax.ShapeDtypeStruct((t, WIDTH_A), BF16),
            jax.ShapeDtypeStruct((t, WIDTH_A), BF16),
            jax.ShapeDtypeStruct((t, WIDTH_A), BF16),
            jax.ShapeDtypeStruct((t, 768), BF16),
            jax.ShapeDtypeStruct((t, 768), BF16),
            jax.ShapeDtypeStruct((t, LANES), F32),
        ],
        compiler_params=pltpu.CompilerParams(
            dimension_semantics=("arbitrary",), vmem_limit_bytes=VMEM_LIMIT),
        name="inproj",
    )(x2d, w_main, w_ag, w_alpha, b_alpha, ca, sa, cr, sr)


def _attn_kernel(q_ref, k_ref, v_ref, o_ref, lse_ref):
    blk = ATT_BLK
    i = pl.program_id(1)
    prev = jnp.maximum(i - 1, 0)
    p0 = pl.multiple_of(prev * blk, blk)
    c0 = pl.multiple_of(i * blk, blk)
    k2 = jnp.concatenate([k_ref[pl.ds(p0, blk), :], k_ref[pl.ds(c0, blk), :]], axis=0)
    v2 = jnp.concatenate([v_ref[pl.ds(p0, blk), :], v_ref[pl.ds(c0, blk), :]], axis=0)
    q = q_ref[...]

    qi = _iota((blk, 2 * blk), 0)
    kj = _iota((blk, 2 * blk), 1)
    mask = (kj >= qi) & (kj <= qi + blk) & ((kj >= blk) | (i > 0))
    low_q = _iota((blk, LANES), 1) < HEAD_DIM_A
    low_v = _iota((2 * blk, LANES), 1) < HEAD_DIM_A
    scale = HEAD_DIM_A ** -0.5

    for s in range(WIDTH_A // LANES):
        sl = slice(s * LANES, (s + 1) * LANES)
        qs = q[:, sl] * jnp.asarray(scale, BF16)
        ks = k2[:, sl]
        vs = v2[:, sl]
        res = []
        for hh in range(2):
            sel_q = low_q if hh == 0 else jnp.logical_not(low_q)
            sel_v = low_v if hh == 0 else jnp.logical_not(low_v)
            qm = jnp.where(sel_q, qs, jnp.zeros_like(qs))
            sc = _dot_nt(qm, ks)
            sc = jnp.where(mask, sc, -jnp.inf)
            m = jnp.max(sc, axis=-1, keepdims=True)
            p = jnp.exp(sc - m).astype(BF16)
            vm = jnp.where(sel_v, vs, jnp.ones_like(vs))
            r = _dot(p, vm)
            den_col = r[:, HEAD_DIM_A:HEAD_DIM_A + 1] if hh == 0 else r[:, 0:1]
            lse_ref[:, 2 * s + hh:2 * s + hh + 1] = m + jnp.log(den_col)
            res.append(r)
        num = jnp.where(low_q, res[0], res[1])
        den = pltpu.roll(jnp.where(low_q, res[1], res[0]), HEAD_DIM_A, 1)
        o_ref[:, sl] = (num / den).astype(BF16)


def _attention(q, k, v):
    nb, length, _ = q.shape
    blk = ATT_BLK
    return pl.pallas_call(
        _attn_kernel,
        grid=(nb, length // blk),
        in_specs=[
            pl.BlockSpec((None, blk, WIDTH_A), lambda n, i: (n, i, 0)),
            pl.BlockSpec((None, length, WIDTH_A), lambda n, i: (n, 0, 0)),
            pl.BlockSpec((None, length, WIDTH_A), lambda n, i: (n, 0, 0)),
        ],
        out_specs=[
            pl.BlockSpec((None, blk, WIDTH_A), lambda n, i: (n, i, 0)),
            pl.BlockSpec((None, blk, N_HEADS_A), lambda n, i: (n, i, 0)),
        ],
        out_shape=[
            jax.ShapeDtypeStruct((nb, length, WIDTH_A), BF16),
            jax.ShapeDtypeStruct((nb, length, N_HEADS_A), F32),
        ],
        compiler_params=pltpu.CompilerParams(
            dimension_semantics=("arbitrary", "arbitrary"), vmem_limit_bytes=VMEM_LIMIT),
        name="dilated_attn",
    )(q, k, v)


def _retention_kernel(r_ref, qdec_ref, kdec_ref, dmask_ref, cdec_ref, bd_ref, ones_ref, y_ref, state_ref):
    c = RET_CHUNK

    @pl.when(pl.program_id(1) == 0)
    def _():
        state_ref[...] = jnp.zeros_like(state_ref)

    q = r_ref[:, 0:128]
    k = r_ref[:, 128:256]
    v = r_ref[:, 256:512]
    gate = r_ref[:, 512:768].astype(F32)
    lane_q = _iota((c, 128), 1) // QK_DIM_R
    lane_v = _iota((c, 256), 1) // V_DIM_R

    intra = jnp.zeros((c, 256), F32)
    for h in range(N_HEADS_R):
        qm = jnp.where(lane_q == h, q, jnp.zeros_like(q))
        sc = _dot_nt(qm, k) * dmask_ref[h]
        vm = jnp.where(lane_v == h, v, jnp.zeros_like(v))
        intra = intra + _dot(sc.astype(BF16), vm)

    state = state_ref[...]
    qd = (q.astype(F32) * qdec_ref[...]).astype(BF16)
    inter = _dot(qd, state.astype(BF16))
    kd = (k.astype(F32) * kdec_ref[...]).astype(BF16)
    kv = _dot_tn(kd, v)
    state_ref[...] = cdec_ref[...] * state + jnp.where(bd_ref[...] > 0, kv, 0.0)

    y = _head_norm(intra + inter, ones_ref[...], V_DIM_R) * _silu(gate)
    y_ref[...] = y.astype(BF16)


def _retention(r3, tabs):
    b, seq, _ = r3.shape
    c = RET_CHUNK
    qdec, kdec, dmask, cdec, bd, ones = tabs
    fixed2 = lambda n, i: (0, 0)
    return pl.pallas_call(
        _retention_kernel,
        grid=(b, seq // c),
        in_specs=[
            pl.BlockSpec((None, c, 768), lambda n, i: (n, i, 0)),
            pl.BlockSpec(qdec.shape, fixed2),
            pl.BlockSpec(kdec.shape, fixed2),
            pl.BlockSpec(dmask.shape, lambda n, i: (0, 0, 0)),
            pl.BlockSpec(cdec.shape, fixed2),
            pl.BlockSpec(bd.shape, fixed2),
            pl.BlockSpec(ones.shape, fixed2),
        ],
        out_specs=pl.BlockSpec((None, c, 256), lambda n, i: (n, i, 0)),
        out_shape=jax.ShapeDtypeStruct((b, seq, 256), BF16),
        scratch_shapes=[pltpu.VMEM((128, 256), F32)],
        compiler_params=pltpu.CompilerParams(
            dimension_semantics=("arbitrary", "arbitrary"), vmem_limit_bytes=VMEM_LIMIT),
        name="retention",
    )(r3, qdec, kdec, dmask, cdec, bd, ones)


def _gla_kernel(g_ref, la_ref, tri_ref, bdt_ref, ones_ref, y_ref, state_ref):
    c = GLA_CHUNK

    @pl.when(pl.program_id(1) == 0)
    def _():
        state_ref[...] = jnp.zeros_like(state_ref)

    la_hi, la_lo = _split_hi_lo(la_ref[...])
    cum_all = _dot(tri_ref[...], la_hi) + _dot(tri_ref[...], la_lo)

    lane_q = _iota((c, 128), 1) // QK_DIM_G
    lane_v = _iota((c, 256), 1) // V_DIM_G
    causal = _iota((c, c), 0) >= _iota((c, c), 1)
    bdt = bdt_ref[...] > 0

    for ci in range(GLA_BLOCK // c):
        rows = slice(ci * c, (ci + 1) * c)
        q = g_ref[rows, 0:128].astype(F32)
        k = g_ref[rows, 128:256].astype(F32)
        v = g_ref[rows, 256:512]
        gate = g_ref[rows, 512:768].astype(F32)
        cum = cum_all[rows, :]
        last = cum[c - 1:c, :]
        q_t = (q * jnp.exp(cum)).astype(BF16)
        k_t = (k * jnp.exp(-cum)).astype(BF16)
        k_l = (k * jnp.exp(last - cum)).astype(BF16)

        intra = jnp.zeros((c, 256), F32)
        for h in range(N_HEADS_G):
            qm = jnp.where(lane_q == h, q_t, jnp.zeros_like(q_t))
            att = jnp.where(causal, _dot_nt(qm, k_t), 0.0)
            vm = jnp.where(lane_v == h, v, jnp.zeros_like(v))
            intra = intra + _dot(att.astype(BF16), vm)

        state_t = state_ref[...]
        inter = _dot_nt(q_t, state_t.astype(BF16))
        kv_t = _dot_tn(v, k_l)
        state_ref[...] = jnp.exp(last) * state_t + jnp.where(bdt, kv_t, 0.0)

        y = _head_norm(intra + inter, ones_ref[...], V_DIM_G) * _silu(gate)
        y_ref[rows, :] = y.astype(BF16)


def _gla(g3, la3, tabs):
    b, seq, _ = g3.shape
    blk = GLA_BLOCK
    tri, bdt, ones = tabs
    fixed2 = lambda n, i: (0, 0)
    return pl.pallas_call(
        _gla_kernel,
        grid=(b, seq // blk),
        in_specs=[
            pl.BlockSpec((None, blk, 768), lambda n, i: (n, i, 0)),
            pl.BlockSpec((None, blk, LANES), lambda n, i: (n, i, 0)),
            pl.BlockSpec(tri.shape, fixed2),
            pl.BlockSpec(bdt.shape, fixed2),
            pl.BlockSpec(ones.shape, fixed2),
        ],
        out_specs=pl.BlockSpec((None, blk, 256), lambda n, i: (n, i, 0)),
        out_shape=jax.ShapeDtypeStruct((b, seq, 256), BF16),
        scratch_shapes=[pltpu.VMEM((256, 128), F32)],
        compiler_params=pltpu.CompilerParams(
            dimension_semantics=("arbitrary", "arbitrary"), vmem_limit_bytes=VMEM_LIMIT),
        name="gla",
    )(g3, la3, tri, bdt, ones)


def _mix_out_kernel(x_ref, o1_ref, o4_ref, o16_ref, l1_ref, l4_ref, l16_ref, yr_ref, yg_ref,
                    ms_ref, wout_ref, lng_ref, lnb_ref, ones_ref, out_ref):
    l1 = l1_ref[...]
    l4 = l4_ref[...]
    l16 = l16_ref[...]
    top = jnp.maximum(jnp.maximum(l1, l4), l16)
    e1 = jnp.exp(l1 - top)
    e4 = jnp.exp(l4 - top)
    e16 = jnp.exp(l16 - top)
    den = e1 + e4 + e16
    w1 = e1 / den
    w4 = e4 / den
    w16 = e16 / den

    tm = x_ref.shape[0]
    low = _iota((tm, LANES), 1) < HEAD_DIM_A
    ms = ms_ref[...]
    parts = []
    for s in range(WIDTH_A // LANES):
        sl = slice(s * LANES, (s + 1) * LANES)

        def expand(w):
            return jnp.where(low, w[:, 2 * s:2 * s + 1], w[:, 2 * s + 1:2 * s + 2])

        merged = (expand(w1) * o1_ref[:, sl].astype(F32) + expand(w4) * o4_ref[:, sl].astype(F32)
                  + expand(w16) * o16_ref[:, sl].astype(F32))
        ya = _head_norm(merged, ones_ref[...], HEAD_DIM_A)
        parts.append((ya * ms[:, sl]).astype(BF16))
    parts.append((yr_ref[...].astype(F32) * ms[:, 512:768]).astype(BF16))
    parts.append((yg_ref[...].astype(F32) * ms[:, 768:1024]).astype(BF16))
    y = jnp.concatenate(parts, axis=1)
    z = DEEPNORM_ALPHA * x_ref[...] + _dot(y, wout_ref[...])
    out_ref[...] = _layer_norm(z, lng_ref[...], lnb_ref[...])


def _mix_out(x2d, o1, o4, o16, l1, l4, l16, yr, yg, ms, w_out, ln_g, ln_b, ones):
    t = x2d.shape[0]
    tm = ROW_TILE
    row = lambda i: (i, 0)
    fixed = lambda i: (0, 0)
    return pl.pallas_call(
        _mix_out_kernel,
        grid=(t // tm,),
        in_specs=[
            pl.BlockSpec((tm, D_MODEL), row),
            pl.BlockSpec((tm, WIDTH_A), row),
            pl.BlockSpec((tm, WIDTH_A), row),
            pl.BlockSpec((tm, WIDTH_A), row),
            pl.BlockSpec((tm, N_HEADS_A), row),
            pl.BlockSpec((tm, N_HEADS_A), row),
            pl.BlockSpec((tm, N_HEADS_A), row),
            pl.BlockSpec((tm, 256), row),
            pl.BlockSpec((tm, 256), row),
            pl.BlockSpec(ms.shape, fixed),
            pl.BlockSpec(w_out.shape, fixed),
            pl.BlockSpec(ln_g.shape, fixed),
            pl.BlockSpec(ln_b.shape, fixed),
            pl.BlockSpec(ones.shape, fixed),
        ],
        out_specs=pl.BlockSpec((tm, D_MODEL), row),
        out_shape=jax.ShapeDtypeStruct((t, D_MODEL), F32),
        compiler_params=pltpu.CompilerParams(
            dimension_semantics=("arbitrary",), vmem_limit_bytes=VMEM_LIMIT),
        name="mix_out",
    )(x2d, o1, o4, o16, l1, l4, l16, yr, yg, ms, w_out, ln_g, ln_b, ones)


def _ffn_kernel(x_ref, wup_ref, cw_ref, wdn_ref, lng_ref, lnb_ref, out_ref, u_ref, carry_ref, acc_ref):
    tm = x_ref.shape[0]
    halo = 8

    @pl.when(pl.program_id(1) == 0)
    def _():
        carry_ref[...] = jnp.zeros_like(carry_ref)

    x = x_ref[...]
    xb = x.astype(BF16)
    acc_ref[...] = jnp.zeros_like(acc_ref)

    def body(j, _):
        u = _dot(xb, wup_ref[j])
        u_ref[0:halo, :] = carry_ref[j]
        u_ref[halo:halo + tm, :] = u
        carry_ref[j] = u[tm - halo:tm, :]
        cw = cw_ref[j]
        y = (u * cw[2:3, :] + u_ref[halo - 1:halo - 1 + tm, :] * cw[1:2, :]
             + u_ref[halo - 2:halo - 2 + tm, :] * cw[0:1, :] + cw[3:4, :])
        h = _silu(y[:, :FF_CHUNK]) * y[:, FF_CHUNK:]
        acc_ref[...] += _dot(h.astype(BF16), wdn_ref[j])
        return 0

    lax.fori_loop(0, N_FF_CHUNKS, body, 0)
    z = DEEPNORM_ALPHA * x + acc_ref[...]
    out_ref[...] = _layer_norm(z, lng_ref[...], lnb_ref[...])


def _ffn(x3, wup_c, cw_c, wdn_c, ln_g, ln_b):
    b, seq, _ = x3.shape
    tm = ROW_TILE
    fixed3 = lambda n, i: (0, 0, 0)
    fixed2 = lambda n, i: (0, 0)
    return pl.pallas_call(
        _ffn_kernel,
        grid=(b, seq // tm),
        in_specs=[
            pl.BlockSpec((None, tm, D_MODEL), lambda n, i: (n, i, 0)),
            pl.BlockSpec(wup_c.shape, fixed3),
            pl.BlockSpec(cw_c.shape, fixed3),
            pl.BlockSpec(wdn_c.shape, fixed3),
            pl.BlockSpec(ln_g.shape, fixed2),
            pl.BlockSpec(ln_b.shape, fixed2),
        ],
        out_specs=pl.BlockSpec((None, tm, D_MODEL), lambda n, i: (n, i, 0)),
        out_shape=jax.ShapeDtypeStruct((b, seq, D_MODEL), F32),
        scratch_shapes=[
            pltpu.VMEM((tm + 8, 2 * FF_CHUNK), F32),
            pltpu.VMEM((N_FF_CHUNKS, 8, 2 * FF_CHUNK), F32),
            pltpu.VMEM((tm, D_MODEL), F32),
        ],
        compiler_params=pltpu.CompilerParams(
            dimension_semantics=("arbitrary", "arbitrary"), vmem_limit_bytes=VMEM_LIMIT),
        name="conv_glu_ffn",
    )(x3, wup_c, cw_c, wdn_c, ln_g, ln_b)


def _rope_tables(seq, dim):
    inv = 1.0 / (ROPE_THETA ** (jnp.arange(0, dim, 2, dtype=F32) / dim))
    ang = jnp.arange(seq, dtype=F32)[:, None] * inv[None, :]
    cos, sin = jnp.cos(ang), jnp.sin(ang)
    reps = LANES // dim
    cos_l = jnp.tile(jnp.concatenate([cos, cos], axis=1), (1, reps))
    sin_l = jnp.tile(jnp.concatenate([-sin, sin], axis=1), (1, reps))
    return cos_l, sin_l


def _block_diag_ones(n, group):
    idx = np.arange(n) // group
    return jnp.asarray(idx[:, None] == idx[None, :], BF16)


def _retention_tables():
    c = RET_CHUNK
    h = N_HEADS_R
    lg = jnp.log(1.0 - jnp.power(2.0, -5.0 - jnp.arange(h, dtype=F32)))
    idx = jnp.arange(c, dtype=F32)
    dist = idx[:, None] - idx[None, :]
    dmask = jnp.where(dist >= 0, jnp.exp(lg[:, None, None] * jnp.maximum(dist, 0.0)), 0.0)
    q_dec = jnp.exp(lg[:, None] * (idx + 1.0))
    k_dec = jnp.exp(lg[:, None] * (c - 1.0 - idx))
    chunk_dec = jnp.exp(lg * c)
    qdec_l = jnp.repeat(q_dec.T, QK_DIM_R, axis=1)
    kdec_l = jnp.repeat(k_dec.T, QK_DIM_R, axis=1)
    row_h = np.arange(128) // QK_DIM_R
    col_h = np.arange(256) // V_DIM_R
    bd = jnp.asarray(row_h[:, None] == col_h[None, :], F32)
    cdec = bd * chunk_dec[row_h][:, None]
    return qdec_l, kdec_l, dmask, cdec, bd, _block_diag_ones(256, V_DIM_R)


def _gla_tables():
    r = np.arange(GLA_BLOCK)
    tri = jnp.asarray((r[:, None] >= r[None, :]) & (r[:, None] // GLA_CHUNK == r[None, :] // GLA_CHUNK), BF16)
    row_h = np.arange(256) // V_DIM_G
    col_h = np.arange(128) // QK_DIM_G
    bdt = jnp.asarray(row_h[:, None] == col_h[None, :], F32)
    return tri, bdt, _block_diag_ones(256, V_DIM_G)


def _to_classes(t2d, b, seq, d):
    w = t2d.shape[-1]
    if d == 1:
        return t2d.reshape(b, seq, w)
    return t2d.reshape(b, seq // d, d, w).transpose(0, 2, 1, 3).reshape(b * d, seq // d, w)


def _from_classes(t3d, b, seq, d):
    w = t3d.shape[-1]
    if d == 1:
        return t3d.reshape(b * seq, w)
    return t3d.reshape(b, d, seq // d, w).transpose(0, 2, 1, 3).reshape(b * seq, w)


def kernel(x, w_in, w_alpha, b_alpha, mix_scale, w_out, ln1_g, ln1_b, w_up, conv_w, conv_b, w_down, ln2_g, ln2_b):
    b, seq, d_model = x.shape
    assert (d_model, w_in.shape[0]) == (D_MODEL, DEPTH) and seq % ROW_TILE == 0
    t = b * seq
    ca, sa = _rope_tables(seq, HEAD_DIM_A)
    cr, sr = _rope_tables(seq, QK_DIM_R)
    ret_tabs = _retention_tables()
    gla_tabs = _gla_tables()
    ones_a = _block_diag_ones(LANES, HEAD_DIM_A)
    n_main = 3 * WIDTH_A + 768 + 768

    x2d = x.reshape(t, d_model)
    for l in range(DEPTH):
        w_main = w_in[l, :, :n_main].astype(BF16)
        w_ag = jnp.pad(w_in[l, :, n_main:], ((0, 0), (0, LANES - GLA_LOW_RANK))).astype(BF16)
        w_al = jnp.pad(w_alpha[l], ((0, LANES - GLA_LOW_RANK), (0, 0))).astype(BF16)
        qa, ka, va, r2d, g2d, la = _inproj(x2d, w_main, w_ag, w_al, b_alpha[l][None, :], ca, sa, cr, sr, seq)

        outs, lses = [], []
        for d in DILATIONS:
            o, lse = _attention(_to_classes(qa, b, seq, d), _to_classes(ka, b, seq, d), _to_classes(va, b, seq, d))
            outs.append(_from_classes(o, b, seq, d))
            lses.append(_from_classes(lse, b, seq, d))

        yr = _retention(r2d.reshape(b, seq, 768), ret_tabs).reshape(t, 256)
        yg = _gla(g2d.reshape(b, seq, 768), la.reshape(b, seq, LANES), gla_tabs).reshape(t, 256)

        x2d = _mix_out(x2d, outs[0], outs[1], outs[2], lses[0], lses[1], lses[2], yr, yg,
                       mix_scale[l][None, :], w_out[l].astype(BF16), ln1_g[l][None, :], ln1_b[l][None, :], ones_a)

        wu = w_up[l].astype(BF16)
        wup_c = jnp.concatenate(
            [wu[:, :D_FF].reshape(D_MODEL, N_FF_CHUNKS, FF_CHUNK), wu[:, D_FF:].reshape(D_MODEL, N_FF_CHUNKS, FF_CHUNK)],
            axis=2).transpose(1, 0, 2)
        taps = jnp.concatenate([conv_w[l], conv_b[l][None, :], jnp.zeros((4, 2 * D_FF), F32)], axis=0)
        cw_c = jnp.concatenate(
            [taps[:, :D_FF].reshape(8, N_FF_CHUNKS, FF_CHUNK), taps[:, D_FF:].reshape(8, N_FF_CHUNKS, FF_CHUNK)],
            axis=2).transpose(1, 0, 2)
        wdn_c = w_down[l].astype(BF16).reshape(N_FF_CHUNKS, FF_CHUNK, D_MODEL)
        x2d = _ffn(x2d.reshape(b, seq, d_model), wup_c, cw_c, wdn_c, ln2_g[l][None, :], ln2_b[l][None, :]).reshape(t, d_model)
    return x2d.reshape(b, seq, d_model)
```

```python
import functools

import jax
import jax.numpy as jnp
import numpy as np
from jax import lax
from jax.experimental import pallas as pl
from jax.experimental.pallas import tpu as pltpu

F32 = jnp.float32
BF16 = jnp.bfloat16

D_MODEL = 1024
DEPTH = 4
HEAD_DIM_A = 64
N_HEADS_A = 8
WIDTH_A = 512
DILATIONS = (1, 4, 16)
ATT_BLK = 128
ROPE_THETA = 10000.0
N_HEADS_R = 4
QK_DIM_R = 32
V_DIM_R = 64
RET_CHUNK = 128
N_HEADS_G = 4
QK_DIM_G = 32
V_DIM_G = 64
GLA_LOW_RANK = 16
GLA_TAU = 16.0
GLA_CHUNK = 64
GLA_BLOCK = 256
D_FF = 2816
FF_CHUNK = 256
N_FF_CHUNKS = D_FF // FF_CHUNK
DEEPNORM_ALPHA = (2 * DEPTH) ** 0.25
LN_EPS = 1e-5
HEAD_NORM_EPS = 1e-6
LANES = 128
ROW_TILE = 512
VMEM_LIMIT = 56 * 1024 * 1024


def _iota(shape, dim):
    return lax.broadcasted_iota(jnp.int32, shape, dim)


def _dot(a, b):
    return jnp.dot(a, b, preferred_element_type=F32)


def _dot_nt(a, b):
    return lax.dot_general(a, b, (((1,), (1,)), ((), ())), preferred_element_type=F32)


def _dot_tn(a, b):
    return lax.dot_general(a, b, (((0,), (0,)), ((), ())), preferred_element_type=F32)


def _split_hi_lo(x):
    hi = x.astype(BF16)
    lo = (x - hi.astype(F32)).astype(BF16)
    return hi, lo


def _group_mean(x, ones_bd, group):
    hi, lo = _split_hi_lo(x)
    return (_dot(hi, ones_bd) + _dot(lo, ones_bd)) * (1.0 / group)


def _head_norm(x, ones_bd, group):
    mu = _group_mean(x, ones_bd, group)
    d = x - mu
    var = _group_mean(d * d, ones_bd, group)
    return d * lax.rsqrt(var + HEAD_NORM_EPS)


def _silu(g):
    return g * (1.0 / (1.0 + jnp.exp(-g)))


def _layer_norm(z, g, b):
    mu = jnp.mean(z, axis=-1, keepdims=True)
    d = z - mu
    var = jnp.mean(d * d, axis=-1, keepdims=True)
    return d * lax.rsqrt(var + LN_EPS) * g + b


def _rope(t, cos, sin_signed, half):
    lane = _iota(t.shape, 1)
    first = (lane % (2 * half)) < half
    partner = jnp.where(first, pltpu.roll(t, LANES - half, 1), pltpu.roll(t, half, 1))
    return t * cos + partner * sin_signed


def _inproj_kernel(x_ref, w_ref, wag_ref, walpha_ref, balpha_ref, ca_ref, sa_ref, cr_ref, sr_ref,
                   qa_ref, ka_ref, va_ref, r_ref, g_ref, la_ref):
    xb = x_ref[...].astype(BF16)

    def mm(c0, n):
        return _dot(xb, w_ref[:, c0:c0 + n])

    ca = ca_ref[...]
    sa = sa_ref[...]
    for s in range(WIDTH_A // LANES):
        sl = slice(s * LANES, (s + 1) * LANES)
        qa_ref[:, sl] = _rope(mm(s * LANES, LANES), ca, sa, HEAD_DIM_A // 2).astype(BF16)
        ka_ref[:, sl] = _rope(mm(WIDTH_A + s * LANES, LANES), ca, sa, HEAD_DIM_A // 2).astype(BF16)
    va_ref[...] = mm(2 * WIDTH_A, WIDTH_A).astype(BF16)

    base = 3 * WIDTH_A
    cr = cr_ref[...]
    sr = sr_ref[...]
    r_ref[:, 0:128] = _rope(mm(base, 128), cr, sr, QK_DIM_R // 2).astype(BF16)
    r_ref[:, 128:256] = (_rope(mm(base + 128, 128), cr, sr, QK_DIM_R // 2) * (QK_DIM_R ** -0.5)).astype(BF16)
    r_ref[:, 256:768] = mm(base + 256, 512).astype(BF16)

    base += 768
    g_ref[:, 0:128] = (mm(base, 128) * (QK_DIM_G ** -0.5)).astype(BF16)
    g_ref[:, 128:768] = mm(base + 128, 640).astype(BF16)

    ag = _dot(xb, wag_ref[...]).astype(BF16)
    z = _dot(ag, walpha_ref[...]) + balpha_ref[...]
    log_sig = jnp.minimum(z, 0.0) - jnp.log(1.0 + jnp.exp(-jnp.abs(z)))
    la_ref[...] = log_sig * (1.0 / GLA_TAU)


def _inproj(x2d, w_main, w_ag, w_alpha, b_alpha, ca, sa, cr, sr, seq):
    t = x2d.shape[0]
    tm = ROW_TILE
    n_seq_tiles = seq // tm
    row = lambda i: (i, 0)
    fixed = lambda i: (0, 0)
    tab = lambda i: (i % n_seq_tiles, 0)
    return pl.pallas_call(
        _inproj_kernel,
        grid=(t // tm,),
        in_specs=[
            pl.BlockSpec((tm, D_MODEL), row),
            pl.BlockSpec(w_main.shape, fixed),
            pl.BlockSpec(w_ag.shape, fixed),
            pl.BlockSpec(w_alpha.shape, fixed),
            pl.BlockSpec(b_alpha.shape, fixed),
            pl.BlockSpec((tm, LANES), tab),
            pl.BlockSpec((tm, LANES), tab),
            pl.BlockSpec((tm, LANES), tab),
            pl.BlockSpec((tm, LANES), tab),
        ],
        out_specs=[
            pl.BlockSpec((tm, WIDTH_A), row),
            pl.BlockSpec((tm, WIDTH_A), row),
            pl.BlockSpec((tm, WIDTH_A), row),
            pl.BlockSpec((tm, 768), row),
            pl.BlockSpec((tm, 768), row),
            pl.BlockSpec((tm, LANES), row),
        ],
        out_shape=[
            jax.ShapeDtypeStruct((t, WIDTH_A), BF16),
            jax.ShapeDtypeStruct((t, WIDTH_A), BF16),
            jax.ShapeDtypeStruct((t, WIDTH_A), BF16),
            jax.ShapeDtypeStruct((t, 768), BF16),
            jax.ShapeDtypeStruct((t, 768), BF16),
            jax.ShapeDtypeStruct((t, LANES), F32),
        ],
        compiler_params=pltpu.CompilerParams(
            dimension_semantics=("arbitrary",), vmem_limit_bytes=VMEM_LIMIT),
        name="inproj",
    )(x2d, w_main, w_ag, w_alpha, b_alpha, ca, sa, cr, sr)


def _attn_kernel(q_ref, k_ref, v_ref, o_ref, lse_ref):
    blk = ATT_BLK
    tq = q_ref.shape[0]
    nsub = tq // blk
    n_slabs = WIDTH_A // LANES
    i = pl.program_id(1)

    qi = _iota((blk, 2 * blk), 0)
    kj = _iota((blk, 2 * blk), 1)
    band = (kj >= qi) & (kj <= qi + blk)
    kj0 = kj + jnp.where(i == 0, blk, 0)
    first = (kj0 >= qi) & (kj0 <= qi + blk)
    lane = _iota((blk, LANES), 1)
    low_q = lane < HEAD_DIM_A
    low_v = _iota((2 * blk, LANES), 1) < HEAD_DIM_A
    scale = jnp.asarray(HEAD_DIM_A ** -0.5, BF16)

    units = [(jb, s, hh) for jb in range(nsub) for s in range(n_slabs) for hh in range(2)]

    def key_rows(jb):
        if jb == 0:
            start = jnp.maximum(i * tq - blk, 0)
        else:
            start = i * tq + (jb - 1) * blk
        return pl.ds(pl.multiple_of(start, blk), 2 * blk)

    def scores(jb, s, hh):
        sl = slice(s * LANES, (s + 1) * LANES)
        qs = q_ref[jb * blk:(jb + 1) * blk, sl] * scale
        sel_q = low_q if hh == 0 else jnp.logical_not(low_q)
        qm = jnp.where(sel_q, qs, jnp.zeros_like(qs))
        return _dot_nt(qm, k_ref[key_rows(jb), sl])

    def weighted(jb, s, hh, sc):
        sl = slice(s * LANES, (s + 1) * LANES)
        sc = jnp.where(first if jb == 0 else band, sc, -jnp.inf)
        m = jnp.max(sc, axis=-1, keepdims=True)
        p = jnp.exp(sc - m).astype(BF16)
        vs = v_ref[key_rows(jb), sl]
        sel_v = low_v if hh == 0 else jnp.logical_not(low_v)
        vm = jnp.where(sel_v, vs, jnp.ones_like(vs))
        r = _dot(p, vm)
        return r, m + jnp.log(r)

    skew = 4
    pending = {}
    done = {}
    lse_parts = {}
    grp = (lane % HEAD_DIM_A) // (HEAD_DIM_A // n_slabs)
    for t in range(len(units) + skew):
        if t < len(units):
            pending[t] = scores(*units[t])
        if t >= skew:
            jb, s, hh = units[t - skew]
            done[hh] = weighted(jb, s, hh, pending.pop(t - skew))
            if hh == 1:
                (r0, lse0), (r1, lse1) = done[0], done[1]
                num = jnp.where(low_q, r0, r1)
                den = pltpu.roll(jnp.where(low_q, r1, r0), HEAD_DIM_A, 1)
                o_ref[jb * blk:(jb + 1) * blk, s * LANES:(s + 1) * LANES] = (num / den).astype(BF16)
                lse_parts[s] = jnp.where(low_q, lse1, lse0)
                if s == n_slabs - 1:
                    tile = lse_parts[n_slabs - 1]
                    for s2 in range(n_slabs - 1):
                        tile = jnp.where(grp == s2, lse_parts[s2], tile)
                    lse_ref[jb * blk:(jb + 1) * blk, :] = tile


def _attention(q, k, v):
    nb, length, _ = q.shape
    blk = min(4 * ATT_BLK, length)
    return pl.pallas_call(
        _attn_kernel,
        grid=(nb, length // blk),
        in_specs=[
            pl.BlockSpec((None, blk, WIDTH_A), lambda n, i: (n, i, 0)),
            pl.BlockSpec((None, length, WIDTH_A), lambda n, i: (n, 0, 0)),
            pl.BlockSpec((None, length, WIDTH_A), lambda n, i: (n, 0, 0)),
        ],
        out_specs=[
            pl.BlockSpec((None, blk, WIDTH_A), lambda n, i: (n, i, 0)),
            pl.BlockSpec((None, blk, LANES), lambda n, i: (n, i, 0)),
        ],
        out_shape=[
            jax.ShapeDtypeStruct((nb, length, WIDTH_A), BF16),
            jax.ShapeDtypeStruct((nb, length, LANES), F32),
        ],
        compiler_params=pltpu.CompilerParams(
            dimension_semantics=("arbitrary", "arbitrary"), vmem_limit_bytes=VMEM_LIMIT),
        name="dilated_attn",
    )(q, k, v)


def _retention_kernel(r_ref, qdec_ref, kdec_ref, dmask_ref, cdec_ref, bd_ref, ones_ref, y_ref, state_ref):
    c = RET_CHUNK

    @pl.when(pl.program_id(1) == 0)
    def _():
        state_ref[...] = jnp.zeros_like(state_ref)

    q = r_ref[:, 0:128]
    k = r_ref[:, 128:256]
    v = r_ref[:, 256:512]
    gate = r_ref[:, 512:768].astype(F32)
    lane_q = _iota((c, 128), 1) // QK_DIM_R
    lane_v = _iota((c, 256), 1) // V_DIM_R

    intra = jnp.zeros((c, 256), F32)
    for h in range(N_HEADS_R):
        qm = jnp.where(lane_q == h, q, jnp.zeros_like(q))
        sc = _dot_nt(qm, k) * dmask_ref[h]
        vm = jnp.where(lane_v == h, v, jnp.zeros_like(v))
        intra = intra + _dot(sc.astype(BF16), vm)

    state = state_ref[...]
    qd = (q.astype(F32) * qdec_ref[...]).astype(BF16)
    inter = _dot(qd, state.astype(BF16))
    kd = (k.astype(F32) * kdec_ref[...]).astype(BF16)
    kv = _dot_tn(kd, v)
    state_ref[...] = cdec_ref[...] * state + jnp.where(bd_ref[...] > 0, kv, 0.0)

    y = _head_norm(intra + inter, ones_ref[...], V_DIM_R) * _silu(gate)
    y_ref[...] = y.astype(BF16)


def _retention(r3, tabs):
    b, seq, _ = r3.shape
    c = RET_CHUNK
    qdec, kdec, dmask, cdec, bd, ones = tabs
    fixed2 = lambda n, i: (0, 0)
    return pl.pallas_call(
        _retention_kernel,
        grid=(b, seq // c),
        in_specs=[
            pl.BlockSpec((None, c, 768), lambda n, i: (n, i, 0)),
            pl.BlockSpec(qdec.shape, fixed2),
            pl.BlockSpec(kdec.shape, fixed2),
            pl.BlockSpec(dmask.shape, lambda n, i: (0, 0, 0)),
            pl.BlockSpec(cdec.shape, fixed2),
            pl.BlockSpec(bd.shape, fixed2),
            pl.BlockSpec(ones.shape, fixed2),
        ],
        out_specs=pl.BlockSpec((None, c, 256), lambda n, i: (n, i, 0)),
        out_shape=jax.ShapeDtypeStruct((b, seq, 256), BF16),
        scratch_shapes=[pltpu.VMEM((128, 256), F32)],
        compiler_params=pltpu.CompilerParams(
            dimension_semantics=("arbitrary", "arbitrary"), vmem_limit_bytes=VMEM_LIMIT),
        name="retention",
    )(r3, qdec, kdec, dmask, cdec, bd, ones)


def _gla_kernel(g_ref, la_ref, tri_ref, bdt_ref, ones_ref, y_ref, state_ref):
    c = GLA_CHUNK

    @pl.when(pl.program_id(1) == 0)
    def _():
        state_ref[...] = jnp.zeros_like(state_ref)

    la_hi, la_lo = _split_hi_lo(la_ref[...])
    cum_all = _dot(tri_ref[...], la_hi) + _dot(tri_ref[...], la_lo)

    lane_q = _iota((c, 128), 1) // QK_DIM_G
    lane_v = _iota((c, 256), 1) // V_DIM_G
    causal = _iota((c, c), 0) >= _iota((c, c), 1)
    bdt = bdt_ref[...] > 0

    for ci in range(GLA_BLOCK // c):
        rows = slice(ci * c, (ci + 1) * c)
        q = g_ref[rows, 0:128].astype(F32)
        k = g_ref[rows, 128:256].astype(F32)
        v = g_ref[rows, 256:512]
        gate = g_ref[rows, 512:768].astype(F32)
        cum = cum_all[rows, :]
        last = cum[c - 1:c, :]
        q_t = (q * jnp.exp(cum)).astype(BF16)
        k_t = (k * jnp.exp(-cum)).astype(BF16)
        k_l = (k * jnp.exp(last - cum)).astype(BF16)

        intra = jnp.zeros((c, 256), F32)
        for h in range(N_HEADS_G):
            qm = jnp.where(lane_q == h, q_t, jnp.zeros_like(q_t))
            att = jnp.where(causal, _dot_nt(qm, k_t), 0.0)
            vm = jnp.where(lane_v == h, v, jnp.zeros_like(v))
            intra = intra + _dot(att.astype(BF16), vm)

        state_t = state_ref[...]
        inter = _dot_nt(q_t, state_t.astype(BF16))
        kv_t = _dot_tn(v, k_l)
        state_ref[...] = jnp.exp(last) * state_t + jnp.where(bdt, kv_t, 0.0)

        y = _head_norm(intra + inter, ones_ref[...], V_DIM_G) * _silu(gate)
        y_ref[rows, :] = y.astype(BF16)


def _gla(g3, la3, tabs):
    b, seq, _ = g3.shape
    blk = GLA_BLOCK
    tri, bdt, ones = tabs
    fixed2 = lambda n, i: (0, 0)
    return pl.pallas_call(
        _gla_kernel,
        grid=(b, seq // blk),
        in_specs=[
            pl.BlockSpec((None, blk, 768), lambda n, i: (n, i, 0)),
            pl.BlockSpec((None, blk, LANES), lambda n, i: (n, i, 0)),
            pl.BlockSpec(tri.shape, fixed2),
            pl.BlockSpec(bdt.shape, fixed2),
            pl.BlockSpec(ones.shape, fixed2),
        ],
        out_specs=pl.BlockSpec((None, blk, 256), lambda n, i: (n, i, 0)),
        out_shape=jax.ShapeDtypeStruct((b, seq, 256), BF16),
        scratch_shapes=[pltpu.VMEM((256, 128), F32)],
        compiler_params=pltpu.CompilerParams(
            dimension_semantics=("arbitrary", "arbitrary"), vmem_limit_bytes=VMEM_LIMIT),
        name="gla",
    )(g3, la3, tri, bdt, ones)


def _mix_out_kernel(x_ref, o1_ref, o4_ref, o16_ref, l1_ref, l4_ref, l16_ref, yr_ref, yg_ref,
                    ms_ref, wout_ref, lng_ref, lnb_ref, ones_ref, out_ref):
    l1 = l1_ref[...]
    l4 = l4_ref[...]
    l16 = l16_ref[...]
    top = jnp.maximum(jnp.maximum(l1, l4), l16)
    e1 = jnp.exp(l1 - top)
    e4 = jnp.exp(l4 - top)
    e16 = jnp.exp(l16 - top)
    den = e1 + e4 + e16
    w1 = e1 / den
    w4 = e4 / den
    w16 = e16 / den

    tm = x_ref.shape[0]
    low = _iota((tm, LANES), 1) < HEAD_DIM_A
    ms = ms_ref[...]
    parts = []
    for s in range(WIDTH_A // LANES):
        sl = slice(s * LANES, (s + 1) * LANES)

        def expand(w):
            c0 = HEAD_DIM_A + 16 * s
            c1 = 16 * s
            return jnp.where(low, w[:, c0:c0 + 1], w[:, c1:c1 + 1])

        merged = (expand(w1) * o1_ref[:, sl].astype(F32) + expand(w4) * o4_ref[:, sl].astype(F32)
                  + expand(w16) * o16_ref[:, sl].astype(F32))
        ya = _head_norm(merged, ones_ref[...], HEAD_DIM_A)
        parts.append((ya * ms[:, sl]).astype(BF16))
    parts.append((yr_ref[...].astype(F32) * ms[:, 512:768]).astype(BF16))
    parts.append((yg_ref[...].astype(F32) * ms[:, 768:1024]).astype(BF16))
    y = jnp.concatenate(parts, axis=1)
    z = DEEPNORM_ALPHA * x_ref[...] + _dot(y, wout_ref[...])
    out_ref[...] = _layer_norm(z, lng_ref[...], lnb_ref[...])


def _mix_out(x2d, o1, o4, o16, l1, l4, l16, yr, yg, ms, w_out, ln_g, ln_b, ones):
    t = x2d.shape[0]
    tm = ROW_TILE
    row = lambda i: (i, 0)
    fixed = lambda i: (0, 0)
    return pl.pallas_call(
        _mix_out_kernel,
        grid=(t // tm,),
        in_specs=[
            pl.BlockSpec((tm, D_MODEL), row),
            pl.BlockSpec((tm, WIDTH_A), row),
            pl.BlockSpec((tm, WIDTH_A), row),
            pl.BlockSpec((tm, WIDTH_A), row),
            pl.BlockSpec((tm, LANES), row),
            pl.BlockSpec((tm, LANES), row),
            pl.BlockSpec((tm, LANES), row),
            pl.BlockSpec((tm, 256), row),
            pl.BlockSpec((tm, 256), row),
            pl.BlockSpec(ms.shape, fixed),
            pl.BlockSpec(w_out.shape, fixed),
            pl.BlockSpec(ln_g.shape, fixed),
            pl.BlockSpec(ln_b.shape, fixed),
            pl.BlockSpec(ones.shape, fixed),
        ],
        out_specs=pl.BlockSpec((tm, D_MODEL), row),
        out_shape=jax.ShapeDtypeStruct((t, D_MODEL), F32),
        compiler_params=pltpu.CompilerParams(
            dimension_semantics=("arbitrary",), vmem_limit_bytes=VMEM_LIMIT),
        name="mix_out",
    )(x2d, o1, o4, o16, l1, l4, l16, yr, yg, ms, w_out, ln_g, ln_b, ones)


def _ffn_kernel(x_ref, wup_ref, cw_ref, wdn_ref, lng_ref, lnb_ref, out_ref, u_ref, carry_ref, acc_ref, xb_ref):
    assert N_FF_CHUNKS % 2 == 1 and N_FF_CHUNKS >= 3
    tm = x_ref.shape[0]
    halo = 8

    @pl.when(pl.program_id(1) == 0)
    def _():
        carry_ref[...] = jnp.zeros_like(carry_ref)

    xb_ref[...] = x_ref[...].astype(BF16)

    def produce(j, buf):
        u_ref[buf, 0:halo, :] = carry_ref[j]
        u_ref[buf, halo:halo + tm, :] = _dot(xb_ref[...], wup_ref[j])
        carry_ref[j] = u_ref[buf, tm:tm + halo, :]

    def consume(j, buf, first=False):
        cw = cw_ref[j]
        y = (u_ref[buf, halo:halo + tm, :] * cw[2:3, :] + u_ref[buf, halo - 1:halo - 1 + tm, :] * cw[1:2, :]
             + u_ref[buf, halo - 2:halo - 2 + tm, :] * cw[0:1, :] + cw[3:4, :])
        h = _silu(y[:, :FF_CHUNK]) * y[:, FF_CHUNK:]
        d = _dot(h.astype(BF16), wdn_ref[j])
        if first:
            acc_ref[...] = d
        else:
            acc_ref[...] += d

    produce(0, 0)
    produce(1, 1)
    consume(0, 0, first=True)

    def pair(p, _):
        j = 2 * p + 1
        produce(j + 1, 0)
        consume(j, 1)
        produce(j + 2, 1)
        consume(j + 1, 0)
        return 0

    lax.fori_loop(0, (N_FF_CHUNKS - 3) // 2, pair, 0)
    produce(N_FF_CHUNKS - 1, 0)
    consume(N_FF_CHUNKS - 2, 1)
    consume(N_FF_CHUNKS - 1, 0)
    z = DEEPNORM_ALPHA * x_ref[...] + acc_ref[...]
    out_ref[...] = _layer_norm(z, lng_ref[...], lnb_ref[...])


def _ffn(x3, wup_c, cw_c, wdn_c, ln_g, ln_b):
    b, seq, _ = x3.shape
    tm = ROW_TILE
    fixed3 = lambda n, i: (0, 0, 0)
    fixed2 = lambda n, i: (0, 0)
    return pl.pallas_call(
        _ffn_kernel,
        grid=(b, seq // tm),
        in_specs=[
            pl.BlockSpec((None, tm, D_MODEL), lambda n, i: (n, i, 0)),
            pl.BlockSpec(wup_c.shape, fixed3),
            pl.BlockSpec(cw_c.shape, fixed3),
            pl.BlockSpec(wdn_c.shape, fixed3),
            pl.BlockSpec(ln_g.shape, fixed2),
            pl.BlockSpec(ln_b.shape, fixed2),
        ],
        out_specs=pl.BlockSpec((None, tm, D_MODEL), lambda n, i: (n, i, 0)),
        out_shape=jax.ShapeDtypeStruct((b, seq, D_MODEL), F32),
        scratch_shapes=[
            pltpu.VMEM((2, tm + 8, 2 * FF_CHUNK), F32),
            pltpu.VMEM((N_FF_CHUNKS, 8, 2 * FF_CHUNK), F32),
            pltpu.VMEM((tm, D_MODEL), F32),
            pltpu.VMEM((tm, D_MODEL), BF16),
        ],
        compiler_params=pltpu.CompilerParams(
            dimension_semantics=("arbitrary", "arbitrary"), vmem_limit_bytes=VMEM_LIMIT),
        name="conv_glu_ffn",
    )(x3, wup_c, cw_c, wdn_c, ln_g, ln_b)


def _rope_tables(seq, dim):
    inv = 1.0 / (ROPE_THETA ** (jnp.arange(0, dim, 2, dtype=F32) / dim))
    ang = jnp.arange(seq, dtype=F32)[:, None] * inv[None, :]
    cos, sin = jnp.cos(ang), jnp.sin(ang)
    reps = LANES // dim
    cos_l = jnp.tile(jnp.concatenate([cos, cos], axis=1), (1, reps))
    sin_l = jnp.tile(jnp.concatenate([-sin, sin], axis=1), (1, reps))
    return cos_l, sin_l


def _block_diag_ones(n, group):
    idx = np.arange(n) // group
    return jnp.asarray(idx[:, None] == idx[None, :], BF16)


def _retention_tables():
    c = RET_CHUNK
    h = N_HEADS_R
    lg = jnp.log(1.0 - jnp.power(2.0, -5.0 - jnp.arange(h, dtype=F32)))
    idx = jnp.arange(c, dtype=F32)
    dist = idx[:, None] - idx[None, :]
    dmask = jnp.where(dist >= 0, jnp.exp(lg[:, None, None] * jnp.maximum(dist, 0.0)), 0.0)
    q_dec = jnp.exp(lg[:, None] * (idx + 1.0))
    k_dec = jnp.exp(lg[:, None] * (c - 1.0 - idx))
    chunk_dec = jnp.exp(lg * c)
    qdec_l = jnp.repeat(q_dec.T, QK_DIM_R, axis=1)
    kdec_l = jnp.repeat(k_dec.T, QK_DIM_R, axis=1)
    row_h = np.arange(128) // QK_DIM_R
    col_h = np.arange(256) // V_DIM_R
    bd = jnp.asarray(row_h[:, None] == col_h[None, :], F32)
    cdec = bd * chunk_dec[row_h][:, None]
    return qdec_l, kdec_l, dmask, cdec, bd, _block_diag_ones(256, V_DIM_R)


def _gla_tables():
    r = np.arange(GLA_BLOCK)
    tri = jnp.asarray((r[:, None] >= r[None, :]) & (r[:, None] // GLA_CHUNK == r[None, :] // GLA_CHUNK), BF16)
    row_h = np.arange(256) // V_DIM_G
    col_h = np.arange(128) // QK_DIM_G
    bdt = jnp.asarray(row_h[:, None] == col_h[None, :], F32)
    return tri, bdt, _block_diag_ones(256, V_DIM_G)


def _to_classes(t2d, b, seq, d):
    w = t2d.shape[-1]
    if d == 1:
        return t2d.reshape(b, seq, w)
    return t2d.reshape(b, seq // d, d, w).transpose(0, 2, 1, 3).reshape(b * d, seq // d, w)


def _from_classes(t3d, b, seq, d):
    w = t3d.shape[-1]
    if d == 1:
        return t3d.reshape(b * seq, w)
    return t3d.reshape(b, d, seq // d, w).transpose(0, 2, 1, 3).reshape(b * seq, w)


def kernel(x, w_in, w_alpha, b_alpha, mix_scale, w_out, ln1_g, ln1_b, w_up, conv_w, conv_b, w_down, ln2_g, ln2_b):
    b, seq, d_model = x.shape
    assert (d_model, w_in.shape[0]) == (D_MODEL, DEPTH) and seq % ROW_TILE == 0
    t = b * seq
    ca, sa = _rope_tables(seq, HEAD_DIM_A)
    cr, sr = _rope_tables(seq, QK_DIM_R)
    ret_tabs = _retention_tables()
    gla_tabs = _gla_tables()
    ones_a = _block_diag_ones(LANES, HEAD_DIM_A)
    n_main = 3 * WIDTH_A + 768 + 768

    x2d = x.reshape(t, d_model)
    for l in range(DEPTH):
        w_main = w_in[l, :, :n_main].astype(BF16)
        w_ag = jnp.pad(w_in[l, :, n_main:], ((0, 0), (0, LANES - GLA_LOW_RANK))).astype(BF16)
        w_al = jnp.pad(w_alpha[l], ((0, LANES - GLA_LOW_RANK), (0, 0))).astype(BF16)
        qa, ka, va, r2d, g2d, la = _inproj(x2d, w_main, w_ag, w_al, b_alpha[l][None, :], ca, sa, cr, sr, seq)

        outs, lses = [], []
        for d in DILATIONS:
            o, lse = _attention(_to_classes(qa, b, seq, d), _to_classes(ka, b, seq, d), _to_classes(va, b, seq, d))
            outs.append(_from_classes(o, b, seq, d))
            lses.append(_from_classes(lse, b, seq, d))

        yr = _retention(r2d.reshape(b, seq, 768), ret_tabs).reshape(t, 256)
        yg = _gla(g2d.reshape(b, seq, 768), la.reshape(b, seq, LANES), gla_tabs).reshape(t, 256)

        x2d = _mix_out(x2d, outs[0], outs[1], outs[2], lses[0], lses[1], lses[2], yr, yg,
                       mix_scale[l][None, :], w_out[l].astype(BF16), ln1_g[l][None, :], ln1_b[l][None, :], ones_a)

        wu = w_up[l].astype(BF16)
        wup_c = jnp.concatenate(
            [wu[:, :D_FF].reshape(D_MODEL, N_FF_CHUNKS, FF_CHUNK), wu[:, D_FF:].reshape(D_MODEL, N_FF_CHUNKS, FF_CHUNK)],
            axis=2).transpose(1, 0, 2)
        taps = jnp.concatenate([conv_w[l], conv_b[l][None, :], jnp.zeros((4, 2 * D_FF), F32)], axis=0)
        cw_c = jnp.concatenate(
            [taps[:, :D_FF].reshape(8, N_FF_CHUNKS, FF_CHUNK), taps[:, D_FF:].reshape(8, N_FF_CHUNKS, FF_CHUNK)],
            axis=2).transpose(1, 0, 2)
        wdn_c = w_down[l].astype(BF16).reshape(N_FF_CHUNKS, FF_CHUNK, D_MODEL)
        x2d = _ffn(x2d.reshape(b, seq, d_model), wup_c, cw_c, wdn_c, ln2_g[l][None, :], ln2_b[l][None, :]).reshape(t, d_model)
    return x2d.reshape(b, seq, d_model)
```

```python
import jax
import jax.numpy as jnp
import numpy as np
from jax import lax
from jax.experimental import pallas as pl
from jax.experimental.pallas import tpu as pltpu

F32 = jnp.float32
BF16 = jnp.bfloat16

D_MODEL = 1024
DEPTH = 4
HEAD_DIM_A = 64
N_HEADS_A = 8
WIDTH_A = 512
DILATIONS = (1, 4, 16)
ATT_BLK = 128
ROPE_THETA = 10000.0
N_HEADS_R = 4
QK_DIM_R = 32
V_DIM_R = 64
RET_CHUNK = 128
N_HEADS_G = 4
QK_DIM_G = 32
V_DIM_G = 64
GLA_LOW_RANK = 16
GLA_TAU = 16.0
GLA_CHUNK = 64
CUM_BLOCK = 256
D_FF = 2816
FF_CHUNK = 256
N_FF_CHUNKS = D_FF // FF_CHUNK
DEEPNORM_ALPHA = (2 * DEPTH) ** 0.25
LN_EPS = 1e-5
HEAD_NORM_EPS = 1e-6
LANES = 128
ROW_TILE = 512
SUB_ROWS = 128
VMEM_LIMIT = 56 * 1024 * 1024


def _iota(shape, dim):
    return lax.broadcasted_iota(jnp.int32, shape, dim)


def _dot(a, b):
    return jnp.dot(a, b, preferred_element_type=F32)


def _dot_nt(a, b):
    return lax.dot_general(a, b, (((1,), (1,)), ((), ())), preferred_element_type=F32)


def _dot_tn(a, b):
    return lax.dot_general(a, b, (((0,), (0,)), ((), ())), preferred_element_type=F32)


def _split_hi_lo(x):
    hi = x.astype(BF16)
    lo = (x - hi.astype(F32)).astype(BF16)
    return hi, lo


def _group_mean(x, ones_bd, group):
    hi, lo = _split_hi_lo(x)
    return (_dot(hi, ones_bd) + _dot(lo, ones_bd)) * (1.0 / group)


def _head_norm(x, ones_bd, group):
    mu = _group_mean(x, ones_bd, group)
    d = x - mu
    var = _group_mean(d * d, ones_bd, group)
    return d * lax.rsqrt(var + HEAD_NORM_EPS)


def _silu(g):
    return g * (1.0 / (1.0 + jnp.exp(-g)))


def _layer_norm(z, g, b):
    mu = jnp.mean(z, axis=-1, keepdims=True)
    d = z - mu
    var = jnp.mean(d * d, axis=-1, keepdims=True)
    return d * lax.rsqrt(var + LN_EPS) * g + b


def _head_stack(t, lane_head, n_heads):
    return jnp.concatenate([jnp.where(lane_head == h, t, jnp.zeros_like(t)) for h in range(n_heads)], axis=0)


def _rope(t, cos, sin_signed, half):
    lane = _iota(t.shape, 1)
    first = (lane % (2 * half)) < half
    partner = jnp.where(first, pltpu.roll(t, LANES - half, 1), pltpu.roll(t, half, 1))
    return t * cos + partner * sin_signed


def _inproj_kernel(x_ref, w_ref, walpha_ref, balpha_ref, ca_ref, sa_ref, cr_ref, sr_ref,
                   q1_ref, k1_ref, v1_ref, q4_ref, k4_ref, v4_ref, q16_ref, k16_ref, v16_ref,
                   r_ref, g_ref, la_ref, slab_ref):
    tm = x_ref.shape[0]
    n_slabs = WIDTH_A // LANES
    xb = x_ref[...].astype(BF16)

    def mm(c0, n):
        return _dot(xb, w_ref[:, c0:c0 + n])

    ca = ca_ref[...]
    sa = sa_ref[...]
    outs = ((q1_ref, q4_ref, q16_ref), (k1_ref, k4_ref, k16_ref), (v1_ref, v4_ref, v16_ref))
    for ti in range(3):
        t = mm(ti * WIDTH_A, WIDTH_A)
        for s in range(n_slabs):
            sl = slice(s * LANES, (s + 1) * LANES)
            slab = t[:, sl]
            if ti < 2:
                slab = _rope(slab, ca, sa, HEAD_DIM_A // 2)
            slab_ref[ti * n_slabs + s] = slab
            outs[ti][0][:, sl] = slab.astype(BF16)
    for ti in range(3):
        for s in range(n_slabs):
            sl = slice(s * LANES, (s + 1) * LANES)
            for o_ref, d in ((outs[ti][1], DILATIONS[1]), (outs[ti][2], DILATIONS[2])):
                for r in range(d):
                    o_ref[r, :, sl] = slab_ref[ti * n_slabs + s, pl.ds(r, tm // d, stride=d), :].astype(BF16)

    base = 3 * WIDTH_A
    cr = cr_ref[...]
    sr = sr_ref[...]
    rr = mm(base, 768)
    r_ref[:, 0:128] = _rope(rr[:, 0:128], cr, sr, QK_DIM_R // 2).astype(BF16)
    r_ref[:, 128:256] = (_rope(rr[:, 128:256], cr, sr, QK_DIM_R // 2) * (QK_DIM_R ** -0.5)).astype(BF16)
    r_ref[:, 256:768] = rr[:, 256:768].astype(BF16)

    gg = mm(base + 768, 896)
    g_ref[:, 0:128] = (gg[:, 0:128] * (QK_DIM_G ** -0.5)).astype(BF16)
    g_ref[:, 128:768] = gg[:, 128:768].astype(BF16)
    ag = gg[:, 768:896].astype(BF16)
    z = _dot(ag, walpha_ref[...]) + balpha_ref[...]
    log_sig = jnp.minimum(z, 0.0) - jnp.log(1.0 + jnp.exp(-jnp.abs(z)))
    la_ref[...] = log_sig * (1.0 / GLA_TAU)


def _inproj(x3, w_all, w_alpha, b_alpha, ca, sa, cr, sr):
    b, seq, _ = x3.shape
    tm = ROW_TILE
    row = lambda n, i: (n, i, 0)
    cls = lambda n, i: (n, 0, i, 0)
    fixed = lambda n, i: (0, 0)
    tab = lambda n, i: (i, 0)
    d4, d16 = DILATIONS[1], DILATIONS[2]
    nat = jax.ShapeDtypeStruct((b, seq, WIDTH_A), BF16)
    c4 = jax.ShapeDtypeStruct((b, d4, seq // d4, WIDTH_A), BF16)
    c16 = jax.ShapeDtypeStruct((b, d16, seq // d16, WIDTH_A), BF16)
    nat_spec = pl.BlockSpec((None, tm, WIDTH_A), row)
    c4_spec = pl.BlockSpec((None, d4, tm // d4, WIDTH_A), cls)
    c16_spec = pl.BlockSpec((None, d16, tm // d16, WIDTH_A), cls)
    return pl.pallas_call(
        _inproj_kernel,
        grid=(b, seq // tm),
        in_specs=[
            pl.BlockSpec((None, tm, D_MODEL), row),
            pl.BlockSpec(w_all.shape, fixed),
            pl.BlockSpec(w_alpha.shape, fixed),
            pl.BlockSpec(b_alpha.shape, fixed),
            pl.BlockSpec((tm, LANES), tab),
            pl.BlockSpec((tm, LANES), tab),
            pl.BlockSpec((tm, LANES), tab),
            pl.BlockSpec((tm, LANES), tab),
        ],
        out_specs=[nat_spec] * 3 + [c4_spec] * 3 + [c16_spec] * 3 + [
            pl.BlockSpec((None, tm, 768), row),
            pl.BlockSpec((None, tm, 768), row),
            pl.BlockSpec((None, tm, LANES), row),
        ],
        out_shape=[nat] * 3 + [c4] * 3 + [c16] * 3 + [
            jax.ShapeDtypeStruct((b, seq, 768), BF16),
            jax.ShapeDtypeStruct((b, seq, 768), BF16),
            jax.ShapeDtypeStruct((b, seq, LANES), F32),
        ],
        scratch_shapes=[pltpu.VMEM((3 * (WIDTH_A // LANES), tm, LANES), F32)],
        compiler_params=pltpu.CompilerParams(
            dimension_semantics=("arbitrary", "arbitrary"), vmem_limit_bytes=VMEM_LIMIT),
        name="inproj",
    )(x3, w_all, w_alpha, b_alpha, ca, sa, cr, sr)


def _attn_kernel(q_ref, k_ref, v_ref, o_ref, lse_ref):
    blk = ATT_BLK
    tq = q_ref.shape[0]
    nsub = tq // blk
    n_slabs = WIDTH_A // LANES
    i = pl.program_id(1)

    qi = _iota((blk, 2 * blk), 0)
    kj = _iota((blk, 2 * blk), 1)
    band = (kj >= qi) & (kj <= qi + blk)
    kj0 = kj + jnp.where(i == 0, blk, 0)
    first = (kj0 >= qi) & (kj0 <= qi + blk)
    lane = _iota((blk, LANES), 1)
    low_q = lane < HEAD_DIM_A
    low_v = _iota((2 * blk, LANES), 1) < HEAD_DIM_A
    scale = jnp.asarray(HEAD_DIM_A ** -0.5, BF16)

    units = [(jb, s, hh) for jb in range(nsub) for s in range(n_slabs) for hh in range(2)]

    def key_rows(jb):
        if jb == 0:
            start = jnp.maximum(i * tq - blk, 0)
        else:
            start = i * tq + (jb - 1) * blk
        return pl.ds(pl.multiple_of(start, blk), 2 * blk)

    def scores(jb, s, hh):
        sl = slice(s * LANES, (s + 1) * LANES)
        qs = q_ref[jb * blk:(jb + 1) * blk, sl] * scale
        sel_q = low_q if hh == 0 else jnp.logical_not(low_q)
        qm = jnp.where(sel_q, qs, jnp.zeros_like(qs))
        return _dot_nt(qm, k_ref[key_rows(jb), sl])

    def weighted(jb, s, hh, sc):
        sl = slice(s * LANES, (s + 1) * LANES)
        sc = jnp.where(first if jb == 0 else band, sc, -jnp.inf)
        m = jnp.max(sc, axis=-1, keepdims=True)
        p = jnp.exp(sc - m).astype(BF16)
        vs = v_ref[key_rows(jb), sl]
        sel_v = low_v if hh == 0 else jnp.logical_not(low_v)
        vm = jnp.where(sel_v, vs, jnp.ones_like(vs))
        r = _dot(p, vm)
        return r, m + jnp.log(r)

    skew = 4
    pending = {}
    done = {}
    lse_parts = {}
    grp = (lane % HEAD_DIM_A) // (HEAD_DIM_A // n_slabs)
    for t in range(len(units) + skew):
        if t < len(units):
            pending[t] = scores(*units[t])
        if t >= skew:
            jb, s, hh = units[t - skew]
            done[hh] = weighted(jb, s, hh, pending.pop(t - skew))
            if hh == 1:
                (r0, lse0), (r1, lse1) = done[0], done[1]
                num = jnp.where(low_q, r0, r1)
                den = pltpu.roll(jnp.where(low_q, r1, r0), HEAD_DIM_A, 1)
                o_ref[jb * blk:(jb + 1) * blk, s * LANES:(s + 1) * LANES] = (num / den).astype(BF16)
                lse_parts[s] = jnp.where(low_q, lse1, lse0)
                if s == n_slabs - 1:
                    tile = lse_parts[n_slabs - 1]
                    for s2 in range(n_slabs - 1):
                        tile = jnp.where(grp == s2, lse_parts[s2], tile)
                    lse_ref[jb * blk:(jb + 1) * blk, :] = tile


def _attention(q, k, v):
    nb, length, _ = q.shape
    blk = min(4 * ATT_BLK, length)
    return pl.pallas_call(
        _attn_kernel,
        grid=(nb, length // blk),
        in_specs=[
            pl.BlockSpec((None, blk, WIDTH_A), lambda n, i: (n, i, 0)),
            pl.BlockSpec((None, length, WIDTH_A), lambda n, i: (n, 0, 0)),
            pl.BlockSpec((None, length, WIDTH_A), lambda n, i: (n, 0, 0)),
        ],
        out_specs=[
            pl.BlockSpec((None, blk, WIDTH_A), lambda n, i: (n, i, 0)),
            pl.BlockSpec((None, blk, LANES), lambda n, i: (n, i, 0)),
        ],
        out_shape=[
            jax.ShapeDtypeStruct((nb, length, WIDTH_A), BF16),
            jax.ShapeDtypeStruct((nb, length, LANES), F32),
        ],
        compiler_params=pltpu.CompilerParams(
            dimension_semantics=("arbitrary", "arbitrary"), vmem_limit_bytes=VMEM_LIMIT),
        name="dilated_attn",
    )(q, k, v)


def _retention_kernel(r_ref, qdec_ref, kdec_ref, dmask_ref, cdec_ref, bd_ref, ones_ref, y_ref,
                      state_ref, qd_ref, o_ref, kv_ref, stb_ref):
    c = RET_CHUNK
    n_chunks = r_ref.shape[0] // c

    @pl.when(pl.program_id(1) == 0)
    def _():
        state_ref[...] = jnp.zeros_like(state_ref)

    lane_q = _iota((c, 128), 1) // QK_DIM_R
    lane_v = _iota((c, 256), 1) // V_DIM_R
    bd = bd_ref[...] > 0

    for ci in range(n_chunks):
        rows = slice(ci * c, (ci + 1) * c)
        q = r_ref[rows, 0:128]
        k = r_ref[rows, 128:256]
        v = r_ref[rows, 256:512]
        sc = _dot_nt(q, _head_stack(k, lane_q, N_HEADS_R)) * dmask_ref[...]
        o_ref[rows, :] = _dot(sc.astype(BF16), _head_stack(v, lane_v, N_HEADS_R))
        qd_ref[rows, :] = (q.astype(F32) * qdec_ref[...]).astype(BF16)
        kd = (k.astype(F32) * kdec_ref[...]).astype(BF16)
        kv_ref[ci] = jnp.where(bd, _dot_tn(kd, v), 0.0)

    st = state_ref[...]
    for ci in range(n_chunks):
        stb_ref[ci] = st.astype(BF16)
        st = cdec_ref[...] * st + kv_ref[ci]
    state_ref[...] = st

    for ci in range(n_chunks):
        rows = slice(ci * c, (ci + 1) * c)
        o_ref[rows, :] += _dot(qd_ref[rows, :], stb_ref[ci])

    y = _head_norm(o_ref[...], ones_ref[...], V_DIM_R) * _silu(r_ref[:, 512:768].astype(F32))
    y_ref[...] = y.astype(BF16)


def _retention(r3, tabs):
    b, seq, _ = r3.shape
    tm = ROW_TILE
    qdec, kdec, dmask, cdec, bd, ones = tabs
    fixed2 = lambda n, i: (0, 0)
    return pl.pallas_call(
        _retention_kernel,
        grid=(b, seq // tm),
        in_specs=[
            pl.BlockSpec((None, tm, 768), lambda n, i: (n, i, 0)),
            pl.BlockSpec(qdec.shape, fixed2),
            pl.BlockSpec(kdec.shape, fixed2),
            pl.BlockSpec(dmask.shape, fixed2),
            pl.BlockSpec(cdec.shape, fixed2),
            pl.BlockSpec(bd.shape, fixed2),
            pl.BlockSpec(ones.shape, fixed2),
        ],
        out_specs=pl.BlockSpec((None, tm, 256), lambda n, i: (n, i, 0)),
        out_shape=jax.ShapeDtypeStruct((b, seq, 256), BF16),
        scratch_shapes=[
            pltpu.VMEM((128, 256), F32),
            pltpu.VMEM((tm, 128), BF16),
            pltpu.VMEM((tm, 256), F32),
            pltpu.VMEM((tm // RET_CHUNK, 128, 256), F32),
            pltpu.VMEM((tm // RET_CHUNK, 128, 256), BF16),
        ],
        compiler_params=pltpu.CompilerParams(
            dimension_semantics=("arbitrary", "arbitrary"), vmem_limit_bytes=VMEM_LIMIT),
        name="retention",
    )(r3, qdec, kdec, dmask, cdec, bd, ones)


def _gla_kernel(g_ref, la_ref, tri_ref, bdt_ref, ones_ref, y_ref,
                state_ref, cum_ref, qt_ref, o_ref, kv_ref, stb_ref):
    c = GLA_CHUNK
    tm = g_ref.shape[0]
    n_chunks = tm // c

    @pl.when(pl.program_id(1) == 0)
    def _():
        state_ref[...] = jnp.zeros_like(state_ref)

    for part in range(tm // CUM_BLOCK):
        rows = slice(part * CUM_BLOCK, (part + 1) * CUM_BLOCK)
        la_hi, la_lo = _split_hi_lo(la_ref[rows, :])
        cum_ref[rows, :] = _dot(tri_ref[...], la_hi) + _dot(tri_ref[...], la_lo)

    lane_q = _iota((c, 128), 1) // QK_DIM_G
    lane_v = _iota((c, 256), 1) // V_DIM_G
    causal = _iota((c, N_HEADS_G * c), 0) >= (_iota((c, N_HEADS_G * c), 1) % c)
    bdt = bdt_ref[...] > 0

    decays = []
    for ci in range(n_chunks):
        rows = slice(ci * c, (ci + 1) * c)
        q = g_ref[rows, 0:128].astype(F32)
        k = g_ref[rows, 128:256].astype(F32)
        v = g_ref[rows, 256:512]
        cum = cum_ref[rows, :]
        last = cum[c - 1:c, :]
        q_t = (q * jnp.exp(cum)).astype(BF16)
        k_t = (k * jnp.exp(-cum)).astype(BF16)
        k_l = (k * jnp.exp(last - cum)).astype(BF16)
        qt_ref[rows, :] = q_t
        att = jnp.where(causal, _dot_nt(q_t, _head_stack(k_t, lane_q, N_HEADS_G)), 0.0)
        o_ref[rows, :] = _dot(att.astype(BF16), _head_stack(v, lane_v, N_HEADS_G))
        kv_ref[ci] = jnp.where(bdt, _dot_tn(v, k_l), 0.0)
        decays.append(jnp.exp(last))

    st = state_ref[...]
    for ci in range(n_chunks):
        stb_ref[ci] = st.astype(BF16)
        st = decays[ci] * st + kv_ref[ci]
    state_ref[...] = st

    for ci in range(n_chunks):
        rows = slice(ci * c, (ci + 1) * c)
        o_ref[rows, :] += _dot_nt(qt_ref[rows, :], stb_ref[ci])

    y = _head_norm(o_ref[...], ones_ref[...], V_DIM_G) * _silu(g_ref[:, 512:768].astype(F32))
    y_ref[...] = y.astype(BF16)


def _gla(g3, la3, tabs):
    b, seq, _ = g3.shape
    tm = ROW_TILE
    tri, bdt, ones = tabs
    fixed2 = lambda n, i: (0, 0)
    return pl.pallas_call(
        _gla_kernel,
        grid=(b, seq // tm),
        in_specs=[
            pl.BlockSpec((None, tm, 768), lambda n, i: (n, i, 0)),
            pl.BlockSpec((None, tm, LANES), lambda n, i: (n, i, 0)),
            pl.BlockSpec(tri.shape, fixed2),
            pl.BlockSpec(bdt.shape, fixed2),
            pl.BlockSpec(ones.shape, fixed2),
        ],
        out_specs=pl.BlockSpec((None, tm, 256), lambda n, i: (n, i, 0)),
        out_shape=jax.ShapeDtypeStruct((b, seq, 256), BF16),
        scratch_shapes=[
            pltpu.VMEM((256, 128), F32),
            pltpu.VMEM((tm, 128), F32),
            pltpu.VMEM((tm, 128), BF16),
            pltpu.VMEM((tm, 256), F32),
            pltpu.VMEM((tm // GLA_CHUNK, 256, 128), F32),
            pltpu.VMEM((tm // GLA_CHUNK, 256, 128), BF16),
        ],
        compiler_params=pltpu.CompilerParams(
            dimension_semantics=("arbitrary", "arbitrary"), vmem_limit_bytes=VMEM_LIMIT),
        name="gla",
    )(g3, la3, tri, bdt, ones)


def _mix_out_kernel(x_ref, o1_ref, o4_ref, o16_ref, l1_ref, l4_ref, l16_ref, yr_ref, yg_ref,
                    ms_ref, wout_ref, lng_ref, lnb_ref, ones_ref, out_ref, on_ref, ln_ref, y_ref):
    tm = x_ref.shape[0]
    n_slabs = WIDTH_A // LANES

    for bi, (o_ref, l_ref, d) in enumerate(((o4_ref, l4_ref, DILATIONS[1]), (o16_ref, l16_ref, DILATIONS[2]))):
        for r in range(d):
            dst = pl.ds(r, tm // d, stride=d)
            ln_ref[bi, dst, :] = l_ref[r]
            for s in range(n_slabs):
                on_ref[bi, s, dst, :] = o_ref[r, :, s * LANES:(s + 1) * LANES].astype(F32)

    low = _iota((SUB_ROWS, LANES), 1) < HEAD_DIM_A
    ms = ms_ref[...]
    for rb in range(tm // SUB_ROWS):
        rows = slice(rb * SUB_ROWS, (rb + 1) * SUB_ROWS)
        l1 = l1_ref[rows, :]
        l4 = ln_ref[0, rows, :]
        l16 = ln_ref[1, rows, :]
        top = jnp.maximum(jnp.maximum(l1, l4), l16)
        e1 = jnp.exp(l1 - top)
        e4 = jnp.exp(l4 - top)
        e16 = jnp.exp(l16 - top)
        inv = 1.0 / (e1 + e4 + e16)
        w4 = e4 * inv
        w16 = e16 * inv
        for s in range(n_slabs):
            sl = slice(s * LANES, (s + 1) * LANES)

            def expand(w):
                c0 = HEAD_DIM_A + 16 * s
                c1 = 16 * s
                return jnp.where(low, w[:, c0:c0 + 1], w[:, c1:c1 + 1])

            x4 = expand(w4)
            x16 = expand(w16)
            merged = ((1.0 - x4 - x16) * o1_ref[rows, sl].astype(F32) + x4 * on_ref[0, s, rows, :]
                      + x16 * on_ref[1, s, rows, :])
            ya = _head_norm(merged, ones_ref[...], HEAD_DIM_A)
            y_ref[rows, sl] = (ya * ms[:, sl]).astype(BF16)
        y_ref[rows, 512:768] = (yr_ref[rows, :].astype(F32) * ms[:, 512:768]).astype(BF16)
        y_ref[rows, 768:1024] = (yg_ref[rows, :].astype(F32) * ms[:, 768:1024]).astype(BF16)

    z = DEEPNORM_ALPHA * x_ref[...] + _dot(y_ref[...], wout_ref[...])
    out_ref[...] = _layer_norm(z, lng_ref[...], lnb_ref[...])


def _mix_out(x3, o1, o4, o16, l1, l4, l16, yr, yg, ms, w_out, ln_g, ln_b, ones):
    b, seq, _ = x3.shape
    tm = ROW_TILE
    row = lambda n, i: (n, i, 0)
    cls = lambda n, i: (n, 0, i, 0)
    fixed = lambda n, i: (0, 0)
    d4, d16 = DILATIONS[1], DILATIONS[2]
    return pl.pallas_call(
        _mix_out_kernel,
        grid=(b, seq // tm),
        in_specs=[
            pl.BlockSpec((None, tm, D_MODEL), row),
            pl.BlockSpec((None, tm, WIDTH_A), row),
            pl.BlockSpec((None, d4, tm // d4, WIDTH_A), cls),
            pl.BlockSpec((None, d16, tm // d16, WIDTH_A), cls),
            pl.BlockSpec((None, tm, LANES), row),
            pl.BlockSpec((None, d4, tm // d4, LANES), cls),
            pl.BlockSpec((None, d16, tm // d16, LANES), cls),
            pl.BlockSpec((None, tm, 256), row),
            pl.BlockSpec((None, tm, 256), row),
            pl.BlockSpec(ms.shape, fixed),
            pl.BlockSpec(w_out.shape, fixed),
            pl.BlockSpec(ln_g.shape, fixed),
            pl.BlockSpec(ln_b.shape, fixed),
            pl.BlockSpec(ones.shape, fixed),
        ],
        out_specs=pl.BlockSpec((None, tm, D_MODEL), row),
        out_shape=jax.ShapeDtypeStruct((b, seq, D_MODEL), F32),
        scratch_shapes=[
            pltpu.VMEM((2, WIDTH_A // LANES, tm, LANES), F32),
            pltpu.VMEM((2, tm, LANES), F32),
            pltpu.VMEM((tm, D_MODEL), BF16),
        ],
        compiler_params=pltpu.CompilerParams(
            dimension_semantics=("arbitrary", "arbitrary"), vmem_limit_bytes=VMEM_LIMIT),
        name="mix_out",
    )(x3, o1, o4, o16, l1, l4, l16, yr, yg, ms, w_out, ln_g, ln_b, ones)


def _ffn_kernel(x_ref, wup_ref, cw_ref, wdn_ref, lng_ref, lnb_ref, out_ref, u_ref, carry_ref, acc_ref, xb_ref):
    assert N_FF_CHUNKS % 2 == 1 and N_FF_CHUNKS >= 3
    tm = x_ref.shape[0]
    halo = 8

    @pl.when(pl.program_id(1) == 0)
    def _():
        carry_ref[...] = jnp.zeros_like(carry_ref)

    xb_ref[...] = x_ref[...].astype(BF16)

    def produce(j, buf):
        u_ref[buf, 0:halo, :] = carry_ref[j]
        u_ref[buf, halo:halo + tm, :] = _dot(xb_ref[...], wup_ref[j])
        carry_ref[j] = u_ref[buf, tm:tm + halo, :]

    def consume(j, buf, first=False):
        cw = cw_ref[j]
        y = (u_ref[buf, halo:halo + tm, :] * cw[2:3, :] + u_ref[buf, halo - 1:halo - 1 + tm, :] * cw[1:2, :]
             + u_ref[buf, halo - 2:halo - 2 + tm, :] * cw[0:1, :] + cw[3:4, :])
        h = _silu(y[:, :FF_CHUNK]) * y[:, FF_CHUNK:]
        d = _dot(h.astype(BF16), wdn_ref[j])
        if first:
            acc_ref[...] = d
        else:
            acc_ref[...] += d

    produce(0, 0)
    produce(1, 1)
    consume(0, 0, first=True)

    def pair(p, _):
        j = 2 * p + 1
        produce(j + 1, 0)
        consume(j, 1)
        produce(j + 2, 1)
        consume(j + 1, 0)
        return 0

    lax.fori_loop(0, (N_FF_CHUNKS - 3) // 2, pair, 0)
    produce(N_FF_CHUNKS - 1, 0)
    consume(N_FF_CHUNKS - 2, 1)
    consume(N_FF_CHUNKS - 1, 0)
    z = DEEPNORM_ALPHA * x_ref[...] + acc_ref[...]
    out_ref[...] = _layer_norm(z, lng_ref[...], lnb_ref[...])


def _ffn(x3, wup_c, cw_c, wdn_c, ln_g, ln_b):
    b, seq, _ = x3.shape
    tm = ROW_TILE
    fixed3 = lambda n, i: (0, 0, 0)
    fixed2 = lambda n, i: (0, 0)
    return pl.pallas_call(
        _ffn_kernel,
        grid=(b, seq // tm),
        in_specs=[
            pl.BlockSpec((None, tm, D_MODEL), lambda n, i: (n, i, 0)),
            pl.BlockSpec(wup_c.shape, fixed3),
            pl.BlockSpec(cw_c.shape, fixed3),
            pl.BlockSpec(wdn_c.shape, fixed3),
            pl.BlockSpec(ln_g.shape, fixed2),
            pl.BlockSpec(ln_b.shape, fixed2),
        ],
        out_specs=pl.BlockSpec((None, tm, D_MODEL), lambda n, i: (n, i, 0)),
        out_shape=jax.ShapeDtypeStruct((b, seq, D_MODEL), F32),
        scratch_shapes=[
            pltpu.VMEM((2, tm + 8, 2 * FF_CHUNK), F32),
            pltpu.VMEM((N_FF_CHUNKS, 8, 2 * FF_CHUNK), F32),
            pltpu.VMEM((tm, D_MODEL), F32),
            pltpu.VMEM((tm, D_MODEL), BF16),
        ],
        compiler_params=pltpu.CompilerParams(
            dimension_semantics=("arbitrary", "arbitrary"), vmem_limit_bytes=VMEM_LIMIT),
        name="conv_glu_ffn",
    )(x3, wup_c, cw_c, wdn_c, ln_g, ln_b)


def _rope_tables(seq, dim):
    inv = 1.0 / (ROPE_THETA ** (jnp.arange(0, dim, 2, dtype=F32) / dim))
    ang = jnp.arange(seq, dtype=F32)[:, None] * inv[None, :]
    cos, sin = jnp.cos(ang), jnp.sin(ang)
    reps = LANES // dim
    cos_l = jnp.tile(jnp.concatenate([cos, cos], axis=1), (1, reps))
    sin_l = jnp.tile(jnp.concatenate([-sin, sin], axis=1), (1, reps))
    return cos_l, sin_l


def _block_diag_ones(n, group):
    idx = np.arange(n) // group
    return jnp.asarray(idx[:, None] == idx[None, :], BF16)


def _retention_tables():
    c = RET_CHUNK
    h = N_HEADS_R
    lg = jnp.log(1.0 - jnp.power(2.0, -5.0 - jnp.arange(h, dtype=F32)))
    idx = jnp.arange(c, dtype=F32)
    dist = idx[:, None] - idx[None, :]
    dmask = jnp.where(dist >= 0, jnp.exp(lg[:, None, None] * jnp.maximum(dist, 0.0)), 0.0)
    dmask_l = jnp.concatenate([dmask[i] for i in range(h)], axis=1)
    q_dec = jnp.exp(lg[:, None] * (idx + 1.0))
    k_dec = jnp.exp(lg[:, None] * (c - 1.0 - idx))
    chunk_dec = jnp.exp(lg * c)
    qdec_l = jnp.repeat(q_dec.T, QK_DIM_R, axis=1)
    kdec_l = jnp.repeat(k_dec.T, QK_DIM_R, axis=1)
    row_h = np.arange(128) // QK_DIM_R
    col_h = np.arange(256) // V_DIM_R
    bd = jnp.asarray(row_h[:, None] == col_h[None, :], F32)
    cdec = bd * chunk_dec[row_h][:, None]
    return qdec_l, kdec_l, dmask_l, cdec, bd, _block_diag_ones(256, V_DIM_R)


def _gla_tables():
    r = np.arange(CUM_BLOCK)
    tri = jnp.asarray((r[:, None] >= r[None, :]) & (r[:, None] // GLA_CHUNK == r[None, :] // GLA_CHUNK), BF16)
    row_h = np.arange(256) // V_DIM_G
    col_h = np.arange(128) // QK_DIM_G
    bdt = jnp.asarray(row_h[:, None] == col_h[None, :], F32)
    return tri, bdt, _block_diag_ones(256, V_DIM_G)


def kernel(x, w_in, w_alpha, b_alpha, mix_scale, w_out, ln1_g, ln1_b, w_up, conv_w, conv_b, w_down, ln2_g, ln2_b):
    b, seq, d_model = x.shape
    assert (d_model, w_in.shape[0]) == (D_MODEL, DEPTH)
    assert seq % ROW_TILE == 0 and (seq // DILATIONS[2]) % ATT_BLK == 0
    ca, sa = _rope_tables(seq, HEAD_DIM_A)
    cr, sr = _rope_tables(seq, QK_DIM_R)
    ret_tabs = _retention_tables()
    gla_tabs = _gla_tables()
    ones_a = _block_diag_ones(LANES, HEAD_DIM_A)
    pad_cols = 3 * WIDTH_A + 768 + 768 + LANES - w_in.shape[2]

    for l in range(DEPTH):
        w_all = jnp.pad(w_in[l], ((0, 0), (0, pad_cols))).astype(BF16)
        w_al = jnp.pad(w_alpha[l], ((0, LANES - GLA_LOW_RANK), (0, 0))).astype(BF16)
        (q1, k1, v1, q4, k4, v4, q16, k16, v16, r3, g3, la3) = _inproj(
            x, w_all, w_al, b_alpha[l][None, :], ca, sa, cr, sr)

        outs, lses = [], []
        for d, (q, k, v) in zip(DILATIONS, ((q1, k1, v1), (q4, k4, v4), (q16, k16, v16))):
            shp = (b * d, seq // d, WIDTH_A)
            o, lse = _attention(q.reshape(shp), k.reshape(shp), v.reshape(shp))
            if d > 1:
                o = o.reshape(b, d, seq // d, WIDTH_A)
                lse = lse.reshape(b, d, seq // d, LANES)
            outs.append(o)
            lses.append(lse)

        yr = _retention(r3, ret_tabs)
        yg = _gla(g3, la3, gla_tabs)

        x = _mix_out(x, outs[0], outs[1], outs[2], lses[0], lses[1], lses[2], yr, yg,
                     mix_scale[l][None, :], w_out[l].astype(BF16), ln1_g[l][None, :], ln1_b[l][None, :], ones_a)

        wu = w_up[l].astype(BF16)
        wup_c = jnp.concatenate(
            [wu[:, :D_FF].reshape(D_MODEL, N_FF_CHUNKS, FF_CHUNK), wu[:, D_FF:].reshape(D_MODEL, N_FF_CHUNKS, FF_CHUNK)],
            axis=2).transpose(1, 0, 2)
        taps = jnp.concatenate([conv_w[l], conv_b[l][None, :], jnp.zeros((4, 2 * D_FF), F32)], axis=0)
        cw_c = jnp.concatenate(
            [taps[:, :D_FF].reshape(8, N_FF_CHUNKS, FF_CHUNK), taps[:, D_FF:].reshape(8, N_FF_CHUNKS, FF_CHUNK)],
            axis=2).transpose(1, 0, 2)
        wdn_c = w_down[l].astype(BF16).reshape(N_FF_CHUNKS, FF_CHUNK, D_MODEL)
        x = _ffn(x, wup_c, cw_c, wdn_c, ln2_g[l][None, :], ln2_b[l][None, :])
    return x
```

```python
import jax
import jax.numpy as jnp
import numpy as np
from jax import lax
from jax.experimental import pallas as pl
from jax.experimental.pallas import tpu as pltpu

F32 = jnp.float32
BF16 = jnp.bfloat16

D_MODEL = 1024
DEPTH = 4
HEAD_DIM_A = 64
N_HEADS_A = 8
WIDTH_A = 512
DILATIONS = (1, 4, 16)
ATT_BLK = 128
ROPE_THETA = 10000.0
N_HEADS_R = 4
QK_DIM_R = 32
V_DIM_R = 64
RET_CHUNK = 128
N_HEADS_G = 4
QK_DIM_G = 32
V_DIM_G = 64
GLA_LOW_RANK = 16
GLA_TAU = 16.0
GLA_CHUNK = 64
CUM_BLOCK = 256
D_FF = 2816
FF_CHUNK = 256
N_FF_CHUNKS = D_FF // FF_CHUNK
DEEPNORM_ALPHA = (2 * DEPTH) ** 0.25
LOG2E = 1.4426950408889634
LN2 = 0.6931471805599453
LN_EPS = 1e-5
HEAD_NORM_EPS = 1e-6
LANES = 128
ROW_TILE = 512
SUB_ROWS = 128
PART_ROWS = 256
VMEM_LIMIT = 56 * 1024 * 1024


def _iota(shape, dim):
    return lax.broadcasted_iota(jnp.int32, shape, dim)


def _dot(a, b):
    return jnp.dot(a, b, preferred_element_type=F32)


def _dot_nt(a, b):
    return lax.dot_general(a, b, (((1,), (1,)), ((), ())), preferred_element_type=F32)


def _dot_tn(a, b):
    return lax.dot_general(a, b, (((0,), (0,)), ((), ())), preferred_element_type=F32)


def _split_hi_lo(x):
    hi = x.astype(BF16)
    lo = (x - hi.astype(F32)).astype(BF16)
    return hi, lo


def _group_mean(x, ones_bd, group):
    hi, lo = _split_hi_lo(x)
    return (_dot(hi, ones_bd) + _dot(lo, ones_bd)) * (1.0 / group)


def _head_norm(x, ones_bd, group):
    mu = _group_mean(x, ones_bd, group)
    d = x - mu
    var = _group_mean(d * d, ones_bd, group)
    return d * lax.rsqrt(var + HEAD_NORM_EPS)


def _silu(g):
    return g * (1.0 / (1.0 + jnp.exp(-g)))


def _layer_norm(z, g, b):
    mu = jnp.mean(z, axis=-1, keepdims=True)
    d = z - mu
    var = jnp.mean(d * d, axis=-1, keepdims=True)
    return d * lax.rsqrt(var + LN_EPS) * g + b


def _head_stack(t, lane_head, n_heads):
    return jnp.concatenate([jnp.where(lane_head == h, t, jnp.zeros_like(t)) for h in range(n_heads)], axis=0)


def _rope(t, cos, sin_signed, half):
    lane = _iota(t.shape, 1)
    first = (lane % (2 * half)) < half
    partner = jnp.where(first, pltpu.roll(t, LANES - half, 1), pltpu.roll(t, half, 1))
    return t * cos + partner * sin_signed


def _inproj_kernel(x_ref, w_ref, walpha_ref, balpha_ref, ca_ref, sa_ref, cr_ref, sr_ref,
                   q1_ref, k1_ref, v1_ref, q4_ref, k4_ref, v4_ref, q16_ref, k16_ref, v16_ref,
                   r_ref, g_ref, la_ref, slab_ref, cls_ref):
    tm = x_ref.shape[0]
    n_slabs = WIDTH_A // LANES
    xb = x_ref[...].astype(BF16)

    def mm(c0, n):
        return _dot(xb, w_ref[:, c0:c0 + n])

    outs = ((q1_ref, q4_ref, q16_ref), (k1_ref, k4_ref, k16_ref), (v1_ref, v4_ref, v16_ref))
    base = 3 * WIDTH_A

    def finish_a(ti, t):
        for s in range(n_slabs):
            sl = slice(s * LANES, (s + 1) * LANES)
            slab = t[:, sl]
            if ti < 2:
                slab = _rope(slab, ca_ref[...], sa_ref[...], HEAD_DIM_A // 2)
            if ti == 0:
                slab = slab * (LOG2E * HEAD_DIM_A ** -0.5)
            slab_ref[ti * n_slabs + s] = slab
            outs[ti][0][:, sl] = slab.astype(BF16)
        d4, d16 = DILATIONS[1], DILATIONS[2]
        n4 = tm // d4
        for s in range(n_slabs):
            sl = slice(s * LANES, (s + 1) * LANES)
            for r4 in range(d4):
                cls4 = slab_ref[ti * n_slabs + s, pl.ds(r4, n4, stride=d4), :]
                outs[ti][1][r4, :, sl] = cls4.astype(BF16)
                cls_ref[s, r4 * n4:(r4 + 1) * n4, :] = cls4
            for r4 in range(d4):
                for c in range(d16 // d4):
                    cls16 = cls_ref[s, pl.ds(r4 * n4 + c, tm // d16, stride=d16 // d4), :]
                    outs[ti][2][r4 + d4 * c, :, sl] = cls16.astype(BF16)

    def finish_r(rr):
        cr = cr_ref[...]
        sr = sr_ref[...]
        r_ref[:, 0:128] = _rope(rr[:, 0:128], cr, sr, QK_DIM_R // 2).astype(BF16)
        r_ref[:, 128:256] = (_rope(rr[:, 128:256], cr, sr, QK_DIM_R // 2) * (QK_DIM_R ** -0.5)).astype(BF16)
        r_ref[:, 256:768] = rr[:, 256:768].astype(BF16)

    def finish_g(gg):
        g_ref[:, 0:128] = (gg[:, 0:128] * (QK_DIM_G ** -0.5)).astype(BF16)
        g_ref[:, 128:768] = gg[:, 128:768].astype(BF16)
        ag = gg[:, 768:896].astype(BF16)
        z = _dot(ag, walpha_ref[...]) + balpha_ref[...]
        log_sig = jnp.minimum(z, 0.0) - jnp.log(1.0 + jnp.exp(-jnp.abs(z)))
        la_ref[...] = log_sig * (1.0 / GLA_TAU)

    stages = [
        (lambda: mm(0, WIDTH_A), lambda t: finish_a(0, t)),
        (lambda: mm(WIDTH_A, WIDTH_A), lambda t: finish_a(1, t)),
        (lambda: mm(2 * WIDTH_A, WIDTH_A), lambda t: finish_a(2, t)),
        (lambda: mm(base, 768), finish_r),
        (lambda: mm(base + 768, 896), finish_g),
    ]
    prev = None
    for matmul, finish in stages:
        t = matmul()
        if prev is not None:
            prev[0](prev[1])
        prev = (finish, t)
    prev[0](prev[1])


def _inproj(x3, w_all, w_alpha, b_alpha, ca, sa, cr, sr):
    b, seq, _ = x3.shape
    tm = ROW_TILE
    row = lambda n, i: (n, i, 0)
    cls = lambda n, i: (n, 0, i, 0)
    fixed = lambda n, i: (0, 0)
    tab = lambda n, i: (i, 0)
    d4, d16 = DILATIONS[1], DILATIONS[2]
    nat = jax.ShapeDtypeStruct((b, seq, WIDTH_A), BF16)
    c4 = jax.ShapeDtypeStruct((b, d4, seq // d4, WIDTH_A), BF16)
    c16 = jax.ShapeDtypeStruct((b, d16, seq // d16, WIDTH_A), BF16)
    nat_spec = pl.BlockSpec((None, tm, WIDTH_A), row)
    c4_spec = pl.BlockSpec((None, d4, tm // d4, WIDTH_A), cls)
    c16_spec = pl.BlockSpec((None, d16, tm // d16, WIDTH_A), cls)
    return pl.pallas_call(
        _inproj_kernel,
        grid=(b, seq // tm),
        in_specs=[
            pl.BlockSpec((None, tm, D_MODEL), row),
            pl.BlockSpec(w_all.shape, fixed),
            pl.BlockSpec(w_alpha.shape, fixed),
            pl.BlockSpec(b_alpha.shape, fixed),
            pl.BlockSpec((tm, LANES), tab),
            pl.BlockSpec((tm, LANES), tab),
            pl.BlockSpec((tm, LANES), tab),
            pl.BlockSpec((tm, LANES), tab),
        ],
        out_specs=[nat_spec] * 3 + [c4_spec] * 3 + [c16_spec] * 3 + [
            pl.BlockSpec((None, tm, 768), row),
            pl.BlockSpec((None, tm, 768), row),
            pl.BlockSpec((None, tm, LANES), row),
        ],
        out_shape=[nat] * 3 + [c4] * 3 + [c16] * 3 + [
            jax.ShapeDtypeStruct((b, seq, 768), BF16),
            jax.ShapeDtypeStruct((b, seq, 768), BF16),
            jax.ShapeDtypeStruct((b, seq, LANES), F32),
        ],
        scratch_shapes=[
            pltpu.VMEM((3 * (WIDTH_A // LANES), tm, LANES), F32),
            pltpu.VMEM((WIDTH_A // LANES, tm, LANES), F32),
        ],
        compiler_params=pltpu.CompilerParams(
            dimension_semantics=("arbitrary", "arbitrary"), vmem_limit_bytes=VMEM_LIMIT),
        name="inproj",
    )(x3, w_all, w_alpha, b_alpha, ca, sa, cr, sr)


def _attn_kernel(q_ref, k_ref, v_ref, o_ref, lse_ref):
    blk = ATT_BLK
    tq = q_ref.shape[0]
    nsub = tq // blk
    n_slabs = WIDTH_A // LANES
    i = pl.program_id(1)

    qi = _iota((blk, 2 * blk), 0)
    kj = _iota((blk, 2 * blk), 1)
    band = (kj >= qi) & (kj <= qi + blk)
    kj0 = kj + jnp.where(i == 0, blk, 0)
    first = (kj0 >= qi) & (kj0 <= qi + blk)
    lane = _iota((blk, LANES), 1)
    low_q = lane < HEAD_DIM_A
    low_v = _iota((2 * blk, LANES), 1) < HEAD_DIM_A

    units = [(jb, s, hh) for jb in range(nsub) for s in range(n_slabs) for hh in range(2)]

    def key_rows(jb):
        if jb == 0:
            start = jnp.maximum(i * tq - blk, 0)
        else:
            start = i * tq + (jb - 1) * blk
        return pl.ds(pl.multiple_of(start, blk), 2 * blk)

    def scores(jb, s, hh):
        sl = slice(s * LANES, (s + 1) * LANES)
        qs = q_ref[jb * blk:(jb + 1) * blk, sl]
        sel_q = low_q if hh == 0 else jnp.logical_not(low_q)
        qm = jnp.where(sel_q, qs, jnp.zeros_like(qs))
        return _dot_nt(qm, k_ref[key_rows(jb), sl])

    def weighted(jb, s, hh, sc):
        sl = slice(s * LANES, (s + 1) * LANES)
        sc = jnp.where(first if jb == 0 else band, sc, -jnp.inf)
        m = jnp.max(sc, axis=-1, keepdims=True)
        p = jnp.exp2(sc - m).astype(BF16)
        vs = v_ref[key_rows(jb), sl]
        sel_v = low_v if hh == 0 else jnp.logical_not(low_v)
        vm = jnp.where(sel_v, vs, jnp.ones_like(vs))
        r = _dot(p, vm)
        return r, m

    skew = 8
    pending = {}
    done = {}
    den_parts = {}
    max_parts = {}
    grp = (lane % HEAD_DIM_A) // (HEAD_DIM_A // n_slabs)

    def gather_heads(parts):
        tile = parts[n_slabs - 1]
        for s2 in range(n_slabs - 1):
            tile = jnp.where(grp == s2, parts[s2], tile)
        return tile

    for t in range(len(units) + skew):
        if t < len(units):
            pending[t] = scores(*units[t])
        if t >= skew:
            jb, s, hh = units[t - skew]
            done[hh] = weighted(jb, s, hh, pending.pop(t - skew))
            if hh == 1:
                (r0, m0), (r1, m1) = done[0], done[1]
                num = jnp.where(low_q, r0, r1)
                den_parts[s] = jnp.where(low_q, r1, r0)
                max_parts[s] = jnp.where(low_q, m1, m0)
                den = pltpu.roll(den_parts[s], HEAD_DIM_A, 1)
                o_ref[jb * blk:(jb + 1) * blk, s * LANES:(s + 1) * LANES] = (num / den).astype(BF16)
                if s == n_slabs - 1:
                    lse2 = gather_heads(max_parts) + jnp.log2(gather_heads(den_parts))
                    lse_ref[jb * blk:(jb + 1) * blk, :] = lse2 * LN2


def _attention(q, k, v):
    nb, length, _ = q.shape
    blk = min(4 * ATT_BLK, length)
    return pl.pallas_call(
        _attn_kernel,
        grid=(nb, length // blk),
        in_specs=[
            pl.BlockSpec((None, blk, WIDTH_A), lambda n, i: (n, i, 0)),
            pl.BlockSpec((None, length, WIDTH_A), lambda n, i: (n, 0, 0)),
            pl.BlockSpec((None, length, WIDTH_A), lambda n, i: (n, 0, 0)),
        ],
        out_specs=[
            pl.BlockSpec((None, blk, WIDTH_A), lambda n, i: (n, i, 0)),
            pl.BlockSpec((None, blk, LANES), lambda n, i: (n, i, 0)),
        ],
        out_shape=[
            jax.ShapeDtypeStruct((nb, length, WIDTH_A), BF16),
            jax.ShapeDtypeStruct((nb, length, LANES), F32),
        ],
        compiler_params=pltpu.CompilerParams(
            dimension_semantics=("arbitrary", "arbitrary"), vmem_limit_bytes=VMEM_LIMIT),
        name="dilated_attn",
    )(q, k, v)


def _retention_kernel(r_ref, qdec_ref, kdec_ref, dmask_ref, cdec_ref, bd_ref, ones_ref, y_ref,
                      state_ref, qd_ref, o_ref, kv_ref, stb_ref):
    c = RET_CHUNK
    n_chunks = r_ref.shape[0] // c

    @pl.when(pl.program_id(1) == 0)
    def _():
        state_ref[...] = jnp.zeros_like(state_ref)

    lane_q = _iota((c, 128), 1) // QK_DIM_R
    lane_v = _iota((c, 256), 1) // V_DIM_R
    bd = bd_ref[...] > 0

    for ci in range(n_chunks):
        rows = slice(ci * c, (ci + 1) * c)
        q = r_ref[rows, 0:128]
        k = r_ref[rows, 128:256]
        v = r_ref[rows, 256:512]
        sc = _dot_nt(q, _head_stack(k, lane_q, N_HEADS_R)) * dmask_ref[...]
        o_ref[rows, :] = _dot(sc.astype(BF16), _head_stack(v, lane_v, N_HEADS_R))
        qd_ref[rows, :] = (q.astype(F32) * qdec_ref[...]).astype(BF16)
        kd = (k.astype(F32) * kdec_ref[...]).astype(BF16)
        kv_ref[ci] = jnp.where(bd, _dot_tn(kd, v), 0.0)

    st = state_ref[...]
    for ci in range(n_chunks):
        stb_ref[ci] = st.astype(BF16)
        st = cdec_ref[...] * st + kv_ref[ci]
    state_ref[...] = st

    for ci in range(n_chunks):
        rows = slice(ci * c, (ci + 1) * c)
        o_ref[rows, :] += _dot(qd_ref[rows, :], stb_ref[ci])

    y = _head_norm(o_ref[...], ones_ref[...], V_DIM_R) * _silu(r_ref[:, 512:768].astype(F32))
    y_ref[...] = y.astype(BF16)


def _retention(r3, tabs):
    b, seq, _ = r3.shape
    tm = ROW_TILE
    qdec, kdec, dmask, cdec, bd, ones = tabs
    fixed2 = lambda n, i: (0, 0)
    return pl.pallas_call(
        _retention_kernel,
        grid=(b, seq // tm),
        in_specs=[
            pl.BlockSpec((None, tm, 768), lambda n, i: (n, i, 0)),
            pl.BlockSpec(qdec.shape, fixed2),
            pl.BlockSpec(kdec.shape, fixed2),
            pl.BlockSpec(dmask.shape, fixed2),
            pl.BlockSpec(cdec.shape, fixed2),
            pl.BlockSpec(bd.shape, fixed2),
            pl.BlockSpec(ones.shape, fixed2),
        ],
        out_specs=pl.BlockSpec((None, tm, 256), lambda n, i: (n, i, 0)),
        out_shape=jax.ShapeDtypeStruct((b, seq, 256), BF16),
        scratch_shapes=[
            pltpu.VMEM((128, 256), F32),
            pltpu.VMEM((tm, 128), BF16),
            pltpu.VMEM((tm, 256), F32),
            pltpu.VMEM((tm // RET_CHUNK, 128, 256), F32),
            pltpu.VMEM((tm // RET_CHUNK, 128, 256), BF16),
        ],
        compiler_params=pltpu.CompilerParams(
            dimension_semantics=("arbitrary", "arbitrary"), vmem_limit_bytes=VMEM_LIMIT),
        name="retention",
    )(r3, qdec, kdec, dmask, cdec, bd, ones)


def _gla_kernel(g_ref, la_ref, tri_ref, bdt_ref, ones_ref, y_ref,
                state_ref, cum_ref, qt_ref, o_ref, kv_ref, stb_ref):
    c = GLA_CHUNK
    tm = g_ref.shape[0]
    n_chunks = tm // c

    @pl.when(pl.program_id(1) == 0)
    def _():
        state_ref[...] = jnp.zeros_like(state_ref)

    for part in range(tm // CUM_BLOCK):
        rows = slice(part * CUM_BLOCK, (part + 1) * CUM_BLOCK)
        la_hi, la_lo = _split_hi_lo(la_ref[rows, :])
        cum_ref[rows, :] = _dot(tri_ref[...], la_hi) + _dot(tri_ref[...], la_lo)

    lane_q = _iota((c, 128), 1) // QK_DIM_G
    lane_v = _iota((c, 256), 1) // V_DIM_G
    causal = _iota((c, N_HEADS_G * c), 0) >= (_iota((c, N_HEADS_G * c), 1) % c)
    bdt = bdt_ref[...] > 0

    decays = []
    for ci in range(n_chunks):
        rows = slice(ci * c, (ci + 1) * c)
        q = g_ref[rows, 0:128].astype(F32)
        k = g_ref[rows, 128:256].astype(F32)
        v = g_ref[rows, 256:512]
        cum = cum_ref[rows, :]
        last = cum[c - 1:c, :]
        q_t = (q * jnp.exp(cum)).astype(BF16)
        k_t = (k * jnp.exp(-cum)).astype(BF16)
        k_l = (k * jnp.exp(last - cum)).astype(BF16)
        qt_ref[rows, :] = q_t
        att = jnp.where(causal, _dot_nt(q_t, _head_stack(k_t, lane_q, N_HEADS_G)), 0.0)
        o_ref[rows, :] = _dot(att.astype(BF16), _head_stack(v, lane_v, N_HEADS_G))
        kv_ref[ci] = jnp.where(bdt, _dot_tn(v, k_l), 0.0)
        decays.append(jnp.exp(last))

    st = state_ref[...]
    for ci in range(n_chunks):
        stb_ref[ci] = st.astype(BF16)
        st = decays[ci] * st + kv_ref[ci]
    state_ref[...] = st

    for ci in range(n_chunks):
        rows = slice(ci * c, (ci + 1) * c)
        o_ref[rows, :] += _dot_nt(qt_ref[rows, :], stb_ref[ci])

    y = _head_norm(o_ref[...], ones_ref[...], V_DIM_G) * _silu(g_ref[:, 512:768].astype(F32))
    y_ref[...] = y.astype(BF16)


def _gla(g3, la3, tabs):
    b, seq, _ = g3.shape
    tm = ROW_TILE
    tri, bdt, ones = tabs
    fixed2 = lambda n, i: (0, 0)
    return pl.pallas_call(
        _gla_kernel,
        grid=(b, seq // tm),
        in_specs=[
            pl.BlockSpec((None, tm, 768), lambda n, i: (n, i, 0)),
            pl.BlockSpec((None, tm, LANES), lambda n, i: (n, i, 0)),
            pl.BlockSpec(tri.shape, fixed2),
            pl.BlockSpec(bdt.shape, fixed2),
            pl.BlockSpec(ones.shape, fixed2),
        ],
        out_specs=pl.BlockSpec((None, tm, 256), lambda n, i: (n, i, 0)),
        out_shape=jax.ShapeDtypeStruct((b, seq, 256), BF16),
        scratch_shapes=[
            pltpu.VMEM((256, 128), F32),
            pltpu.VMEM((tm, 128), F32),
            pltpu.VMEM((tm, 128), BF16),
            pltpu.VMEM((tm, 256), F32),
            pltpu.VMEM((tm // GLA_CHUNK, 256, 128), F32),
            pltpu.VMEM((tm // GLA_CHUNK, 256, 128), BF16),
        ],
        compiler_params=pltpu.CompilerParams(
            dimension_semantics=("arbitrary", "arbitrary"), vmem_limit_bytes=VMEM_LIMIT),
        name="gla",
    )(g3, la3, tri, bdt, ones)


def _mix_out_kernel(x_ref, o1_ref, o4_ref, o16_ref, l1_ref, l4_ref, l16_ref, yr_ref, yg_ref,
                    ms_ref, wout_ref, lng_ref, lnb_ref, ones_ref, out_ref, on_ref, ln_ref, y_ref):
    tm = x_ref.shape[0]
    n_slabs = WIDTH_A // LANES

    def to_sequence_order(part):
        for bi, (o_ref, l_ref, d) in enumerate(((o4_ref, l4_ref, DILATIONS[1]), (o16_ref, l16_ref, DILATIONS[2]))):
            n = PART_ROWS // d
            src = slice(part * n, (part + 1) * n)
            for r in range(d):
                dst = pl.ds(part * PART_ROWS + r, n, stride=d)
                ln_ref[bi, dst, :] = l_ref[r, src, :]
                for s in range(n_slabs):
                    on_ref[bi, s, dst, :] = o_ref[r, src, s * LANES:(s + 1) * LANES].astype(F32)

    low = _iota((SUB_ROWS, LANES), 1) < HEAD_DIM_A
    ms = ms_ref[...]

    def merge(rb):
        rows = slice(rb * SUB_ROWS, (rb + 1) * SUB_ROWS)
        l1 = l1_ref[rows, :]
        l4 = ln_ref[0, rows, :]
        l16 = ln_ref[1, rows, :]
        top = jnp.maximum(jnp.maximum(l1, l4), l16)
        e1 = jnp.exp(l1 - top)
        e4 = jnp.exp(l4 - top)
        e16 = jnp.exp(l16 - top)
        inv = 1.0 / (e1 + e4 + e16)
        w4 = e4 * inv
        w16 = e16 * inv
        for s in range(n_slabs):
            sl = slice(s * LANES, (s + 1) * LANES)

            def expand(w):
                c0 = HEAD_DIM_A + 16 * s
                c1 = 16 * s
                return jnp.where(low, w[:, c0:c0 + 1], w[:, c1:c1 + 1])

            x4 = expand(w4)
            x16 = expand(w16)
            merged = ((1.0 - x4 - x16) * o1_ref[rows, sl].astype(F32) + x4 * on_ref[0, s, rows, :]
                      + x16 * on_ref[1, s, rows, :])
            ya = _head_norm(merged, ones_ref[...], HEAD_DIM_A)
            y_ref[rows, sl] = (ya * ms[:, sl]).astype(BF16)
        y_ref[rows, 512:768] = (yr_ref[rows, :].astype(F32) * ms[:, 512:768]).astype(BF16)
        y_ref[rows, 768:1024] = (yg_ref[rows, :].astype(F32) * ms[:, 768:1024]).astype(BF16)

    def project(part):
        rows = slice(part * PART_ROWS, (part + 1) * PART_ROWS)
        return _dot(y_ref[rows, :], wout_ref[...])

    def finish(part, proj):
        rows = slice(part * PART_ROWS, (part + 1) * PART_ROWS)
        z = DEEPNORM_ALPHA * x_ref[rows, :] + proj
        out_ref[rows, :] = _layer_norm(z, lng_ref[...], lnb_ref[...])

    n_parts = tm // PART_ROWS
    proj = None
    for part in range(n_parts):
        to_sequence_order(part)
        for rb in range(part * (PART_ROWS // SUB_ROWS), (part + 1) * (PART_ROWS // SUB_ROWS)):
            merge(rb)
        if proj is not None:
            finish(part - 1, proj)
        proj = project(part)
    finish(n_parts - 1, proj)


def _mix_out(x3, o1, o4, o16, l1, l4, l16, yr, yg, ms, w_out, ln_g, ln_b, ones):
    b, seq, _ = x3.shape
    tm = ROW_TILE
    row = lambda n, i: (n, i, 0)
    cls = lambda n, i: (n, 0, i, 0)
    fixed = lambda n, i: (0, 0)
    d4, d16 = DILATIONS[1], DILATIONS[2]
    return pl.pallas_call(
        _mix_out_kernel,
        grid=(b, seq // tm),
        in_specs=[
            pl.BlockSpec((None, tm, D_MODEL), row),
            pl.BlockSpec((None, tm, WIDTH_A), row),
            pl.BlockSpec((None, d4, tm // d4, WIDTH_A), cls),
            pl.BlockSpec((None, d16, tm // d16, WIDTH_A), cls),
            pl.BlockSpec((None, tm, LANES), row),
            pl.BlockSpec((None, d4, tm // d4, LANES), cls),
            pl.BlockSpec((None, d16, tm // d16, LANES), cls),
            pl.BlockSpec((None, tm, 256), row),
            pl.BlockSpec((None, tm, 256), row),
            pl.BlockSpec(ms.shape, fixed),
            pl.BlockSpec(w_out.shape, fixed),
            pl.BlockSpec(ln_g.shape, fixed),
            pl.BlockSpec(ln_b.shape, fixed),
            pl.BlockSpec(ones.shape, fixed),
        ],
        out_specs=pl.BlockSpec((None, tm, D_MODEL), row),
        out_shape=jax.ShapeDtypeStruct((b, seq, D_MODEL), F32),
        scratch_shapes=[
            pltpu.VMEM((2, WIDTH_A // LANES, tm, LANES), F32),
            pltpu.VMEM((2, tm, LANES), F32),
            pltpu.VMEM((tm, D_MODEL), BF16),
        ],
        compiler_params=pltpu.CompilerParams(
            dimension_semantics=("arbitrary", "arbitrary"), vmem_limit_bytes=VMEM_LIMIT),
        name="mix_out",
    )(x3, o1, o4, o16, l1, l4, l16, yr, yg, ms, w_out, ln_g, ln_b, ones)


def _ffn_kernel(x_ref, wup_ref, cw_ref, wdn_ref, lng_ref, lnb_ref, out_ref, u_ref, carry_ref, acc_ref, xb_ref):
    assert N_FF_CHUNKS % 2 == 1 and N_FF_CHUNKS >= 3
    tm = x_ref.shape[0]
    halo = 8

    @pl.when(pl.program_id(1) == 0)
    def _():
        carry_ref[...] = jnp.zeros_like(carry_ref)

    xb_ref[...] = x_ref[...].astype(BF16)

    def produce(j, buf):
        u_ref[buf, 0:halo, :] = carry_ref[j]
        u_ref[buf, halo:halo + tm, :] = _dot(xb_ref[...], wup_ref[j])
        carry_ref[j] = u_ref[buf, tm:tm + halo, :]

    def consume(j, buf, first=False):
        cw = cw_ref[j]
        y = (u_ref[buf, halo:halo + tm, :] * cw[2:3, :] + u_ref[buf, halo - 1:halo - 1 + tm, :] * cw[1:2, :]
             + u_ref[buf, halo - 2:halo - 2 + tm, :] * cw[0:1, :] + cw[3:4, :])
        h = _silu(y[:, :FF_CHUNK]) * y[:, FF_CHUNK:]
        d = _dot(h.astype(BF16), wdn_ref[j])
        if first:
            acc_ref[...] = d
        else:
            acc_ref[...] += d

    produce(0, 0)
    produce(1, 1)
    consume(0, 0, first=True)

    def pair(p, _):
        j = 2 * p + 1
        produce(j + 1, 0)
        consume(j, 1)
        produce(j + 2, 1)
        consume(j + 1, 0)
        return 0

    lax.fori_loop(0, (N_FF_CHUNKS - 3) // 2, pair, 0)
    produce(N_FF_CHUNKS - 1, 0)
    consume(N_FF_CHUNKS - 2, 1)
    consume(N_FF_CHUNKS - 1, 0)
    z = DEEPNORM_ALPHA * x_ref[...] + acc_ref[...]
    out_ref[...] = _layer_norm(z, lng_ref[...], lnb_ref[...])


def _ffn(x3, wup_c, cw_c, wdn_c, ln_g, ln_b):
    b, seq, _ = x3.shape
    tm = ROW_TILE
    fixed3 = lambda n, i: (0, 0, 0)
    fixed2 = lambda n, i: (0, 0)
    return pl.pallas_call(
        _ffn_kernel,
        grid=(b, seq // tm),
        in_specs=[
            pl.BlockSpec((None, tm, D_MODEL), lambda n, i: (n, i, 0)),
            pl.BlockSpec(wup_c.shape, fixed3),
            pl.BlockSpec(cw_c.shape, fixed3),
            pl.BlockSpec(wdn_c.shape, fixed3),
            pl.BlockSpec(ln_g.shape, fixed2),
            pl.BlockSpec(ln_b.shape, fixed2),
        ],
        out_specs=pl.BlockSpec((None, tm, D_MODEL), lambda n, i: (n, i, 0)),
        out_shape=jax.ShapeDtypeStruct((b, seq, D_MODEL), F32),
        scratch_shapes=[
            pltpu.VMEM((2, tm + 8, 2 * FF_CHUNK), F32),
            pltpu.VMEM((N_FF_CHUNKS, 8, 2 * FF_CHUNK), F32),
            pltpu.VMEM((tm, D_MODEL), F32),
            pltpu.VMEM((tm, D_MODEL), BF16),
        ],
        compiler_params=pltpu.CompilerParams(
            dimension_semantics=("arbitrary", "arbitrary"), vmem_limit_bytes=VMEM_LIMIT),
        name="conv_glu_ffn",
    )(x3, wup_c, cw_c, wdn_c, ln_g, ln_b)


def _rope_tables(seq, dim):
    inv = 1.0 / (ROPE_THETA ** (jnp.arange(0, dim, 2, dtype=F32) / dim))
    ang = jnp.arange(seq, dtype=F32)[:, None] * inv[None, :]
    cos, sin = jnp.cos(ang), jnp.sin(ang)
    reps = LANES // dim
    cos_l = jnp.tile(jnp.concatenate([cos, cos], axis=1), (1, reps))
    sin_l = jnp.tile(jnp.concatenate([-sin, sin], axis=1), (1, reps))
    return cos_l, sin_l


def _block_diag_ones(n, group):
    idx = np.arange(n) // group
    return jnp.asarray(idx[:, None] == idx[None, :], BF16)


def _retention_tables():
    c = RET_CHUNK
    h = N_HEADS_R
    lg = jnp.log(1.0 - jnp.power(2.0, -5.0 - jnp.arange(h, dtype=F32)))
    idx = jnp.arange(c, dtype=F32)
    dist = idx[:, None] - idx[None, :]
    dmask = jnp.where(dist >= 0, jnp.exp(lg[:, None, None] * jnp.maximum(dist, 0.0)), 0.0)
    dmask_l = jnp.concatenate([dmask[i] for i in range(h)], axis=1)
    q_dec = jnp.exp(lg[:, None] * (idx + 1.0))
    k_dec = jnp.exp(lg[:, None] * (c - 1.0 - idx))
    chunk_dec = jnp.exp(lg * c)
    qdec_l = jnp.repeat(q_dec.T, QK_DIM_R, axis=1)
    kdec_l = jnp.repeat(k_dec.T, QK_DIM_R, axis=1)
    row_h = np.arange(128) // QK_DIM_R
    col_h = np.arange(256) // V_DIM_R
    bd = jnp.asarray(row_h[:, None] == col_h[None, :], F32)
    cdec = bd * chunk_dec[row_h][:, None]
    return qdec_l, kdec_l, dmask_l, cdec, bd, _block_diag_ones(256, V_DIM_R)


def _gla_tables():
    r = np.arange(CUM_BLOCK)
    tri = jnp.asarray((r[:, None] >= r[None, :]) & (r[:, None] // GLA_CHUNK == r[None, :] // GLA_CHUNK), BF16)
    row_h = np.arange(256) // V_DIM_G
    col_h = np.arange(128) // QK_DIM_G
    bdt = jnp.asarray(row_h[:, None] == col_h[None, :], F32)
    return tri, bdt, _block_diag_ones(256, V_DIM_G)


def kernel(x, w_in, w_alpha, b_alpha, mix_scale, w_out, ln1_g, ln1_b, w_up, conv_w, conv_b, w_down, ln2_g, ln2_b):
    b, seq, d_model = x.shape
    assert (d_model, w_in.shape[0]) == (D_MODEL, DEPTH)
    assert seq % ROW_TILE == 0 and (seq // DILATIONS[2]) % ATT_BLK == 0
    ca, sa = _rope_tables(seq, HEAD_DIM_A)
    cr, sr = _rope_tables(seq, QK_DIM_R)
    ret_tabs = _retention_tables()
    gla_tabs = _gla_tables()
    ones_a = _block_diag_ones(LANES, HEAD_DIM_A)
    pad_cols = 3 * WIDTH_A + 768 + 768 + LANES - w_in.shape[2]

    for l in range(DEPTH):
        w_all = jnp.pad(w_in[l], ((0, 0), (0, pad_cols))).astype(BF16)
        w_al = jnp.pad(w_alpha[l], ((0, LANES - GLA_LOW_RANK), (0, 0))).astype(BF16)
        (q1, k1, v1, q4, k4, v4, q16, k16, v16, r3, g3, la3) = _inproj(
            x, w_all, w_al, b_alpha[l][None, :], ca, sa, cr, sr)

        outs, lses = [], []
        for d, (q, k, v) in zip(DILATIONS, ((q1, k1, v1), (q4, k4, v4), (q16, k16, v16))):
            shp = (b * d, seq // d, WIDTH_A)
            o, lse = _attention(q.reshape(shp), k.reshape(shp), v.reshape(shp))
            if d > 1:
                o = o.reshape(b, d, seq // d, WIDTH_A)
                lse = lse.reshape(b, d, seq // d, LANES)
            outs.append(o)
            lses.append(lse)

        yr = _retention(r3, ret_tabs)
        yg = _gla(g3, la3, gla_tabs)

        x = _mix_out(x, outs[0], outs[1], outs[2], lses[0], lses[1], lses[2], yr, yg,
                     mix_scale[l][None, :], w_out[l].astype(BF16), ln1_g[l][None, :], ln1_b[l][None, :], ones_a)

        wu = w_up[l].astype(BF16)
        wup_c = jnp.concatenate(
            [wu[:, :D_FF].reshape(D_MODEL, N_FF_CHUNKS, FF_CHUNK), wu[:, D_FF:].reshape(D_MODEL, N_FF_CHUNKS, FF_CHUNK)],
            axis=2).transpose(1, 0, 2)
        taps = jnp.concatenate([conv_w[l], conv_b[l][None, :], jnp.zeros((4, 2 * D_FF), F32)], axis=0)
        cw_c = jnp.concatenate(
            [taps[:, :D_FF].reshape(8, N_FF_CHUNKS, FF_CHUNK), taps[:, D_FF:].reshape(8, N_FF_CHUNKS, FF_CHUNK)],
            axis=2).transpose(1, 0, 2)
        wdn_c = w_down[l].astype(BF16).reshape(N_FF_CHUNKS, FF_CHUNK, D_MODEL)
        x = _ffn(x, wup_c, cw_c, wdn_c, ln2_g[l][None, :], ln2_b[l][None, :])
    return x
```

```python
import jax
import jax.numpy as jnp
import numpy as np
from jax import lax
from jax.experimental import pallas as pl
from jax.experimental.pallas import tpu as pltpu

F32 = jnp.float32
BF16 = jnp.bfloat16

D_MODEL = 1024
DEPTH = 4
HEAD_DIM_A = 64
N_HEADS_A = 8
WIDTH_A = 512
DILATIONS = (1, 4, 16)
ATT_BLK = 128
ATT_STEP_ROWS = 1024
ROPE_THETA = 10000.0
N_HEADS_R = 4
QK_DIM_R = 32
V_DIM_R = 64
RET_CHUNK = 128
N_HEADS_G = 4
QK_DIM_G = 32
V_DIM_G = 64
GLA_LOW_RANK = 16
GLA_TAU = 16.0
GLA_CHUNK = 64
CUM_BLOCK = 256
D_FF = 2816
FF_CHUNK = 256
N_FF_CHUNKS = D_FF // FF_CHUNK
DEEPNORM_ALPHA = (2 * DEPTH) ** 0.25
LOG2E = 1.4426950408889634
LN2 = 0.6931471805599453
LN_EPS = 1e-5
HEAD_NORM_EPS = 1e-6
LANES = 128
ROW_TILE = 512
FFN_TILE = 1024
SUB_ROWS = 128
PART_ROWS = 256
VMEM_LIMIT = 56 * 1024 * 1024


def _iota(shape, dim):
    return lax.broadcasted_iota(jnp.int32, shape, dim)


def _dot(a, b):
    return jnp.dot(a, b, preferred_element_type=F32)


def _dot_nt(a, b):
    return lax.dot_general(a, b, (((1,), (1,)), ((), ())), preferred_element_type=F32)


def _dot_tn(a, b):
    return lax.dot_general(a, b, (((0,), (0,)), ((), ())), preferred_element_type=F32)


def _split_hi_lo(x):
    hi = x.astype(BF16)
    lo = (x - hi.astype(F32)).astype(BF16)
    return hi, lo


def _group_mean(x, ones_bd, group):
    hi, lo = _split_hi_lo(x)
    return (_dot(hi, ones_bd) + _dot(lo, ones_bd)) * (1.0 / group)


def _head_norm(x, ones_bd, group):
    mu = _group_mean(x, ones_bd, group)
    d = x - mu
    var = _group_mean(d * d, ones_bd, group)
    return d * lax.rsqrt(var + HEAD_NORM_EPS)


def _silu(g):
    return g * (1.0 / (1.0 + jnp.exp(-g)))


def _layer_norm(z, g, b):
    mu = jnp.mean(z, axis=-1, keepdims=True)
    d = z - mu
    var = jnp.mean(d * d, axis=-1, keepdims=True)
    return d * lax.rsqrt(var + LN_EPS) * g + b


def _head_stack(t, lane_head, n_heads):
    return jnp.concatenate([jnp.where(lane_head == h, t, jnp.zeros_like(t)) for h in range(n_heads)], axis=0)


def _rope(t, cos, sin_signed, half):
    lane = _iota(t.shape, 1)
    first = (lane % (2 * half)) < half
    partner = jnp.where(first, pltpu.roll(t, LANES - half, 1), pltpu.roll(t, half, 1))
    return t * cos + partner * sin_signed


def _inproj_kernel(x_ref, w_ref, walpha_ref, balpha_ref, ca_ref, sa_ref, cr_ref, sr_ref,
                   q1_ref, k1_ref, v1_ref, q4_ref, k4_ref, v4_ref, q16_ref, k16_ref, v16_ref,
                   r_ref, g_ref, la_ref, slab_ref, cls_ref):
    tm = x_ref.shape[0]
    n_slabs = WIDTH_A // LANES
    xb = x_ref[...].astype(BF16)

    def mm(c0, n):
        return _dot(xb, w_ref[:, c0:c0 + n])

    outs = ((q1_ref, q4_ref, q16_ref), (k1_ref, k4_ref, k16_ref), (v1_ref, v4_ref, v16_ref))
    base = 3 * WIDTH_A

    def finish_a(ti, t):
        for s in range(n_slabs):
            sl = slice(s * LANES, (s + 1) * LANES)
            slab = t[:, sl]
            if ti < 2:
                slab = _rope(slab, ca_ref[...], sa_ref[...], HEAD_DIM_A // 2)
            if ti == 0:
                slab = slab * (LOG2E * HEAD_DIM_A ** -0.5)
            slab_ref[ti * n_slabs + s] = slab
            outs[ti][0][:, sl] = slab.astype(BF16)
        d4, d16 = DILATIONS[1], DILATIONS[2]
        n4 = tm // d4
        for s in range(n_slabs):
            sl = slice(s * LANES, (s + 1) * LANES)
            for r4 in range(d4):
                cls4 = slab_ref[ti * n_slabs + s, pl.ds(r4, n4, stride=d4), :]
                outs[ti][1][r4, :, sl] = cls4.astype(BF16)
                cls_ref[s, r4 * n4:(r4 + 1) * n4, :] = cls4
            for r4 in range(d4):
                for c in range(d16 // d4):
                    cls16 = cls_ref[s, pl.ds(r4 * n4 + c, tm // d16, stride=d16 // d4), :]
                    outs[ti][2][r4 + d4 * c, :, sl] = cls16.astype(BF16)

    def finish_r(rr):
        cr = cr_ref[...]
        sr = sr_ref[...]
        r_ref[:, 0:128] = _rope(rr[:, 0:128], cr, sr, QK_DIM_R // 2).astype(BF16)
        r_ref[:, 128:256] = (_rope(rr[:, 128:256], cr, sr, QK_DIM_R // 2) * (QK_DIM_R ** -0.5)).astype(BF16)
        r_ref[:, 256:768] = rr[:, 256:768].astype(BF16)

    def finish_g(gg):
        g_ref[:, 0:128] = (gg[:, 0:128] * (QK_DIM_G ** -0.5)).astype(BF16)
        g_ref[:, 128:768] = gg[:, 128:768].astype(BF16)
        ag = gg[:, 768:896].astype(BF16)
        z = _dot(ag, walpha_ref[...]) + balpha_ref[...]
        log_sig = jnp.minimum(z, 0.0) - jnp.log(1.0 + jnp.exp(-jnp.abs(z)))
        la_ref[...] = log_sig * (1.0 / GLA_TAU)

    stages = [
        (lambda: mm(0, WIDTH_A), lambda t: finish_a(0, t)),
        (lambda: mm(WIDTH_A, WIDTH_A), lambda t: finish_a(1, t)),
        (lambda: mm(2 * WIDTH_A, WIDTH_A), lambda t: finish_a(2, t)),
        (lambda: mm(base, 768), finish_r),
        (lambda: mm(base + 768, 896), finish_g),
    ]
    prev = None
    for matmul, finish in stages:
        t = matmul()
        if prev is not None:
            prev[0](prev[1])
        prev = (finish, t)
    prev[0](prev[1])


def _inproj(x3, w_all, w_alpha, b_alpha, ca, sa, cr, sr):
    b, seq, _ = x3.shape
    tm = ROW_TILE
    row = lambda n, i: (n, i, 0)
    cls = lambda n, i: (n, 0, i, 0)
    fixed = lambda n, i: (0, 0)
    tab = lambda n, i: (i, 0)
    d4, d16 = DILATIONS[1], DILATIONS[2]
    nat = jax.ShapeDtypeStruct((b, seq, WIDTH_A), BF16)
    c4 = jax.ShapeDtypeStruct((b, d4, seq // d4, WIDTH_A), BF16)
    c16 = jax.ShapeDtypeStruct((b, d16, seq // d16, WIDTH_A), BF16)
    nat_spec = pl.BlockSpec((None, tm, WIDTH_A), row)
    c4_spec = pl.BlockSpec((None, d4, tm // d4, WIDTH_A), cls)
    c16_spec = pl.BlockSpec((None, d16, tm // d16, WIDTH_A), cls)
    return pl.pallas_call(
        _inproj_kernel,
        grid=(b, seq // tm),
        in_specs=[
            pl.BlockSpec((None, tm, D_MODEL), row),
            pl.BlockSpec(w_all.shape, fixed),
            pl.BlockSpec(w_alpha.shape, fixed),
            pl.BlockSpec(b_alpha.shape, fixed),
            pl.BlockSpec((tm, LANES), tab),
            pl.BlockSpec((tm, LANES), tab),
            pl.BlockSpec((tm, LANES), tab),
            pl.BlockSpec((tm, LANES), tab),
        ],
        out_specs=[nat_spec] * 3 + [c4_spec] * 3 + [c16_spec] * 3 + [
            pl.BlockSpec((None, tm, 768), row),
            pl.BlockSpec((None, tm, 768), row),
            pl.BlockSpec((None, tm, LANES), row),
        ],
        out_shape=[nat] * 3 + [c4] * 3 + [c16] * 3 + [
            jax.ShapeDtypeStruct((b, seq, 768), BF16),
            jax.ShapeDtypeStruct((b, seq, 768), BF16),
            jax.ShapeDtypeStruct((b, seq, LANES), F32),
        ],
        scratch_shapes=[
            pltpu.VMEM((3 * (WIDTH_A // LANES), tm, LANES), F32),
            pltpu.VMEM((WIDTH_A // LANES, tm, LANES), F32),
        ],
        compiler_params=pltpu.CompilerParams(
            dimension_semantics=("arbitrary", "arbitrary"), vmem_limit_bytes=VMEM_LIMIT),
        name="inproj",
    )(x3, w_all, w_alpha, b_alpha, ca, sa, cr, sr)


def _attn_kernel(q_ref, k_ref, v_ref, o_ref, lse_ref):
    blk = ATT_BLK
    n_cls, tq, _ = q_ref.shape
    nsub = tq // blk
    n_slabs = WIDTH_A // LANES
    i = pl.program_id(1)

    qi = _iota((blk, 2 * blk), 0)
    kj = _iota((blk, 2 * blk), 1)
    band = (kj >= qi) & (kj <= qi + blk)
    kj0 = kj + jnp.where(i == 0, blk, 0)
    first = (kj0 >= qi) & (kj0 <= qi + blk)
    lane = _iota((blk, LANES), 1)
    low_q = lane < HEAD_DIM_A
    low_v = _iota((2 * blk, LANES), 1) < HEAD_DIM_A

    units = [(g, jb, s, hh) for g in range(n_cls) for jb in range(nsub) for s in range(n_slabs) for hh in range(2)]

    def key_rows(jb):
        if jb == 0:
            start = jnp.maximum(i * tq - blk, 0)
        else:
            start = i * tq + (jb - 1) * blk
        return pl.ds(pl.multiple_of(start, blk), 2 * blk)

    def scores(g, jb, s, hh):
        sl = slice(s * LANES, (s + 1) * LANES)
        qs = q_ref[g, jb * blk:(jb + 1) * blk, sl]
        sel_q = low_q if hh == 0 else jnp.logical_not(low_q)
        qm = jnp.where(sel_q, qs, jnp.zeros_like(qs))
        return _dot_nt(qm, k_ref[g, key_rows(jb), sl])

    def weighted(g, jb, s, hh, sc):
        sl = slice(s * LANES, (s + 1) * LANES)
        sc = jnp.where(first if jb == 0 else band, sc, -jnp.inf)
        m = jnp.max(sc, axis=-1, keepdims=True)
        p = jnp.exp2(sc - m).astype(BF16)
        vs = v_ref[g, key_rows(jb), sl]
        sel_v = low_v if hh == 0 else jnp.logical_not(low_v)
        vm = jnp.where(sel_v, vs, jnp.ones_like(vs))
        r = _dot(p, vm)
        return r, m

    skew = 8
    pending = {}
    done = {}
    den_parts = {}
    max_parts = {}
    grp = (lane % HEAD_DIM_A) // (HEAD_DIM_A // n_slabs)

    def gather_heads(parts):
        tile = parts[n_slabs - 1]
        for s2 in range(n_slabs - 1):
            tile = jnp.where(grp == s2, parts[s2], tile)
        return tile

    for t in range(len(units) + skew):
        if t < len(units):
            pending[t] = scores(*units[t])
        if t >= skew:
            g, jb, s, hh = units[t - skew]
            done[hh] = weighted(g, jb, s, hh, pending.pop(t - skew))
            if hh == 1:
                (r0, m0), (r1, m1) = done[0], done[1]
                num = jnp.where(low_q, r0, r1)
                den_parts[s] = jnp.where(low_q, r1, r0)
                max_parts[s] = jnp.where(low_q, m1, m0)
                den = pltpu.roll(den_parts[s], HEAD_DIM_A, 1)
                o_ref[g, jb * blk:(jb + 1) * blk, s * LANES:(s + 1) * LANES] = (num / den).astype(BF16)
                if s == n_slabs - 1:
                    lse2 = gather_heads(max_parts) + jnp.log2(gather_heads(den_parts))
                    lse_ref[g, jb * blk:(jb + 1) * blk, :] = lse2 * LN2


def _attention(q, k, v):
    nb, length, _ = q.shape
    blk = min(ATT_STEP_ROWS, length)
    n_cls = ATT_STEP_ROWS // blk
    return pl.pallas_call(
        _attn_kernel,
        grid=(nb // n_cls, length // blk),
        in_specs=[
            pl.BlockSpec((n_cls, blk, WIDTH_A), lambda n, i: (n, i, 0)),
            pl.BlockSpec((n_cls, length, WIDTH_A), lambda n, i: (n, 0, 0)),
            pl.BlockSpec((n_cls, length, WIDTH_A), lambda n, i: (n, 0, 0)),
        ],
        out_specs=[
            pl.BlockSpec((n_cls, blk, WIDTH_A), lambda n, i: (n, i, 0)),
            pl.BlockSpec((n_cls, blk, LANES), lambda n, i: (n, i, 0)),
        ],
        out_shape=[
            jax.ShapeDtypeStruct((nb, length, WIDTH_A), BF16),
            jax.ShapeDtypeStruct((nb, length, LANES), F32),
        ],
        compiler_params=pltpu.CompilerParams(
            dimension_semantics=("arbitrary", "arbitrary"), vmem_limit_bytes=VMEM_LIMIT),
        name="dilated_attn",
    )(q, k, v)


def _retention_kernel(r_ref, qdec_ref, kdec_ref, dmask_ref, cdec_ref, bd_ref, ones_ref, y_ref,
                      state_ref, qd_ref, o_ref, kv_ref, stb_ref):
    c = RET_CHUNK
    n_chunks = r_ref.shape[0] // c

    @pl.when(pl.program_id(1) == 0)
    def _():
        state_ref[...] = jnp.zeros_like(state_ref)

    lane_q = _iota((c, 128), 1) // QK_DIM_R
    lane_v = _iota((c, 256), 1) // V_DIM_R
    bd = bd_ref[...] > 0

    for ci in range(n_chunks):
        rows = slice(ci * c, (ci + 1) * c)
        q = r_ref[rows, 0:128]
        k = r_ref[rows, 128:256]
        v = r_ref[rows, 256:512]
        sc = _dot_nt(q, _head_stack(k, lane_q, N_HEADS_R)) * dmask_ref[...]
        o_ref[rows, :] = _dot(sc.astype(BF16), _head_stack(v, lane_v, N_HEADS_R))
        qd_ref[rows, :] = (q.astype(F32) * qdec_ref[...]).astype(BF16)
        kd = (k.astype(F32) * kdec_ref[...]).astype(BF16)
        kv_ref[ci] = jnp.where(bd, _dot_tn(kd, v), 0.0)

    st = state_ref[...]
    for ci in range(n_chunks):
        stb_ref[ci] = st.astype(BF16)
        st = cdec_ref[...] * st + kv_ref[ci]
    state_ref[...] = st

    for ci in range(n_chunks):
        rows = slice(ci * c, (ci + 1) * c)
        o_ref[rows, :] += _dot(qd_ref[rows, :], stb_ref[ci])

    y = _head_norm(o_ref[...], ones_ref[...], V_DIM_R) * _silu(r_ref[:, 512:768].astype(F32))
    y_ref[...] = y.astype(BF16)


def _gla_kernel(g_ref, la_ref, tri_ref, bdt_ref, ones_ref, y_ref,
                state_ref, cum_ref, qt_ref, o_ref, kv_ref, stb_ref):
    c = GLA_CHUNK
    tm = g_ref.shape[0]
    n_chunks = tm // c

    @pl.when(pl.program_id(1) == 0)
    def _():
        state_ref[...] = jnp.zeros_like(state_ref)

    for part in range(tm // CUM_BLOCK):
        rows = slice(part * CUM_BLOCK, (part + 1) * CUM_BLOCK)
        la_hi, la_lo = _split_hi_lo(la_ref[rows, :])
        cum_ref[rows, :] = _dot(tri_ref[...], la_hi) + _dot(tri_ref[...], la_lo)

    lane_q = _iota((c, 128), 1) // QK_DIM_G
    lane_v = _iota((c, 256), 1) // V_DIM_G
    causal = _iota((c, N_HEADS_G * c), 0) >= (_iota((c, N_HEADS_G * c), 1) % c)
    bdt = bdt_ref[...] > 0

    decays = []
    for ci in range(n_chunks):
        rows = slice(ci * c, (ci + 1) * c)
        q = g_ref[rows, 0:128].astype(F32)
        k = g_ref[rows, 128:256].astype(F32)
        v = g_ref[rows, 256:512]
        cum = cum_ref[rows, :]
        last = cum[c - 1:c, :]
        q_t = (q * jnp.exp(cum)).astype(BF16)
        k_t = (k * jnp.exp(-cum)).astype(BF16)
        k_l = (k * jnp.exp(last - cum)).astype(BF16)
        qt_ref[rows, :] = q_t
        att = jnp.where(causal, _dot_nt(q_t, _head_stack(k_t, lane_q, N_HEADS_G)), 0.0)
        o_ref[rows, :] = _dot(att.astype(BF16), _head_stack(v, lane_v, N_HEADS_G))
        kv_ref[ci] = jnp.where(bdt, _dot_tn(v, k_l), 0.0)
        decays.append(jnp.exp(last))

    st = state_ref[...]
    for ci in range(n_chunks):
        stb_ref[ci] = st.astype(BF16)
        st = decays[ci] * st + kv_ref[ci]
    state_ref[...] = st

    for ci in range(n_chunks):
        rows = slice(ci * c, (ci + 1) * c)
        o_ref[rows, :] += _dot_nt(qt_ref[rows, :], stb_ref[ci])

    y = _head_norm(o_ref[...], ones_ref[...], V_DIM_G) * _silu(g_ref[:, 512:768].astype(F32))
    y_ref[...] = y.astype(BF16)


N_RET_SCRATCH = 5


def _linear_mixers_kernel(r_ref, qdec_ref, kdec_ref, dmask_ref, cdec_ref, bd_ref, ones_ref,
                          g_ref, la_ref, tri_ref, bdt_ref, yr_ref, yg_ref, *scratch):
    _retention_kernel(r_ref, qdec_ref, kdec_ref, dmask_ref, cdec_ref, bd_ref, ones_ref, yr_ref,
                      *scratch[:N_RET_SCRATCH])
    _gla_kernel(g_ref, la_ref, tri_ref, bdt_ref, ones_ref, yg_ref, *scratch[N_RET_SCRATCH:])


def _linear_mixers(r3, g3, la3, ret_tabs, gla_tabs):
    b, seq, _ = r3.shape
    tm = ROW_TILE
    qdec, kdec, dmask, cdec, bd, ones = ret_tabs
    tri, bdt, _ = gla_tabs
    fixed2 = lambda n, i: (0, 0)
    row = lambda n, i: (n, i, 0)
    y_shape = jax.ShapeDtypeStruct((b, seq, 256), BF16)
    return pl.pallas_call(
        _linear_mixers_kernel,
        grid=(b, seq // tm),
        in_specs=[
            pl.BlockSpec((None, tm, 768), row),
            pl.BlockSpec(qdec.shape, fixed2),
            pl.BlockSpec(kdec.shape, fixed2),
            pl.BlockSpec(dmask.shape, fixed2),
            pl.BlockSpec(cdec.shape, fixed2),
            pl.BlockSpec(bd.shape, fixed2),
            pl.BlockSpec(ones.shape, fixed2),
            pl.BlockSpec((None, tm, 768), row),
            pl.BlockSpec((None, tm, LANES), row),
            pl.BlockSpec(tri.shape, fixed2),
            pl.BlockSpec(bdt.shape, fixed2),
        ],
        out_specs=[pl.BlockSpec((None, tm, 256), row), pl.BlockSpec((None, tm, 256), row)],
        out_shape=[y_shape, y_shape],
        scratch_shapes=[
            pltpu.VMEM((128, 256), F32),
            pltpu.VMEM((tm, 128), BF16),
            pltpu.VMEM((tm, 256), F32),
            pltpu.VMEM((tm // RET_CHUNK, 128, 256), F32),
            pltpu.VMEM((tm // RET_CHUNK, 128, 256), BF16),
            pltpu.VMEM((256, 128), F32),
            pltpu.VMEM((tm, 128), F32),
            pltpu.VMEM((tm, 128), BF16),
            pltpu.VMEM((tm, 256), F32),
            pltpu.VMEM((tm // GLA_CHUNK, 256, 128), F32),
            pltpu.VMEM((tm // GLA_CHUNK, 256, 128), BF16),
        ],
        compiler_params=pltpu.CompilerParams(
            dimension_semantics=("arbitrary", "arbitrary"), vmem_limit_bytes=VMEM_LIMIT),
        name="linear_mixers",
    )(r3, qdec, kdec, dmask, cdec, bd, ones, g3, la3, tri, bdt)


def _mix_out_kernel(x_ref, o1_ref, o4_ref, o16_ref, l1_ref, l4_ref, l16_ref, yr_ref, yg_ref,
                    ms_ref, wout_ref, lng_ref, lnb_ref, ones_ref, out_ref, on_ref, ln_ref, y_ref):
    tm = x_ref.shape[0]
    n_slabs = WIDTH_A // LANES

    def to_sequence_order(part):
        for bi, (o_ref, l_ref, d) in enumerate(((o4_ref, l4_ref, DILATIONS[1]), (o16_ref, l16_ref, DILATIONS[2]))):
            n = PART_ROWS // d
            src = slice(part * n, (part + 1) * n)
            for r in range(d):
                dst = pl.ds(part * PART_ROWS + r, n, stride=d)
                ln_ref[bi, dst, :] = l_ref[r, src, :]
                for s in range(n_slabs):
                    on_ref[bi, s, dst, :] = o_ref[r, src, s * LANES:(s + 1) * LANES].astype(F32)

    low = _iota((SUB_ROWS, LANES), 1) < HEAD_DIM_A
    ms = ms_ref[...]

    def merge(rb):
        rows = slice(rb * SUB_ROWS, (rb + 1) * SUB_ROWS)
        l1 = l1_ref[rows, :]
        l4 = ln_ref[0, rows, :]
        l16 = ln_ref[1, rows, :]
        top = jnp.maximum(jnp.maximum(l1, l4), l16)
        e1 = jnp.exp(l1 - top)
        e4 = jnp.exp(l4 - top)
        e16 = jnp.exp(l16 - top)
        inv = 1.0 / (e1 + e4 + e16)
        w4 = e4 * inv
        w16 = e16 * inv
        for s in range(n_slabs):
            sl = slice(s * LANES, (s + 1) * LANES)

            def expand(w):
                c0 = HEAD_DIM_A + 16 * s
                c1 = 16 * s
                return jnp.where(low, w[:, c0:c0 + 1], w[:, c1:c1 + 1])

            x4 = expand(w4)
            x16 = expand(w16)
            merged = ((1.0 - x4 - x16) * o1_ref[rows, sl].astype(F32) + x4 * on_ref[0, s, rows, :]
                      + x16 * on_ref[1, s, rows, :])
            ya = _head_norm(merged, ones_ref[...], HEAD_DIM_A)
            y_ref[rows, sl] = (ya * ms[:, sl]).astype(BF16)
        y_ref[rows, 512:768] = (yr_ref[rows, :].astype(F32) * ms[:, 512:768]).astype(BF16)
        y_ref[rows, 768:1024] = (yg_ref[rows, :].astype(F32) * ms[:, 768:1024]).astype(BF16)

    def project(part):
        rows = slice(part * PART_ROWS, (part + 1) * PART_ROWS)
        return _dot(y_ref[rows, :], wout_ref[...])

    def finish(part, proj):
        rows = slice(part * PART_ROWS, (part + 1) * PART_ROWS)
        z = DEEPNORM_ALPHA * x_ref[rows, :] + proj
        out_ref[rows, :] = _layer_norm(z, lng_ref[...], lnb_ref[...])

    n_parts = tm // PART_ROWS
    proj = None
    for part in range(n_parts):
        to_sequence_order(part)
        for rb in range(part * (PART_ROWS // SUB_ROWS), (part + 1) * (PART_ROWS // SUB_ROWS)):
            merge(rb)
        if proj is not None:
            finish(part - 1, proj)
        proj = project(part)
    finish(n_parts - 1, proj)


def _mix_out(x3, o1, o4, o16, l1, l4, l16, yr, yg, ms, w_out, ln_g, ln_b, ones):
    b, seq, _ = x3.shape
    tm = ROW_TILE
    row = lambda n, i: (n, i, 0)
    cls = lambda n, i: (n, 0, i, 0)
    fixed = lambda n, i: (0, 0)
    d4, d16 = DILATIONS[1], DILATIONS[2]
    return pl.pallas_call(
        _mix_out_kernel,
        grid=(b, seq // tm),
        in_specs=[
            pl.BlockSpec((None, tm, D_MODEL), row),
            pl.BlockSpec((None, tm, WIDTH_A), row),
            pl.BlockSpec((None, d4, tm // d4, WIDTH_A), cls),
            pl.BlockSpec((None, d16, tm // d16, WIDTH_A), cls),
            pl.BlockSpec((None, tm, LANES), row),
            pl.BlockSpec((None, d4, tm // d4, LANES), cls),
            pl.BlockSpec((None, d16, tm // d16, LANES), cls),
            pl.BlockSpec((None, tm, 256), row),
            pl.BlockSpec((None, tm, 256), row),
            pl.BlockSpec(ms.shape, fixed),
            pl.BlockSpec(w_out.shape, fixed),
            pl.BlockSpec(ln_g.shape, fixed),
            pl.BlockSpec(ln_b.shape, fixed),
            pl.BlockSpec(ones.shape, fixed),
        ],
        out_specs=pl.BlockSpec((None, tm, D_MODEL), row),
        out_shape=jax.ShapeDtypeStruct((b, seq, D_MODEL), F32),
        scratch_shapes=[
            pltpu.VMEM((2, WIDTH_A // LANES, tm, LANES), F32),
            pltpu.VMEM((2, tm, LANES), F32),
            pltpu.VMEM((tm, D_MODEL), BF16),
        ],
        compiler_params=pltpu.CompilerParams(
            dimension_semantics=("arbitrary", "arbitrary"), vmem_limit_bytes=VMEM_LIMIT),
        name="mix_out",
    )(x3, o1, o4, o16, l1, l4, l16, yr, yg, ms, w_out, ln_g, ln_b, ones)


def _ffn_kernel(x_ref, wup_ref, cw_ref, wdn_ref, lng_ref, lnb_ref, out_ref, u_ref, carry_ref, acc_ref, xb_ref):
    assert N_FF_CHUNKS % 2 == 1 and N_FF_CHUNKS >= 3
    tm = x_ref.shape[0]
    halo = 8

    @pl.when(pl.program_id(1) == 0)
    def _():
        carry_ref[...] = jnp.zeros_like(carry_ref)

    xb_ref[...] = x_ref[...].astype(BF16)

    def produce(j, buf):
        u_ref[buf, 0:halo, :] = carry_ref[j]
        u_ref[buf, halo:halo + tm, :] = _dot(xb_ref[...], wup_ref[j])
        carry_ref[j] = u_ref[buf, tm:tm + halo, :]

    def consume(j, buf, first=False):
        cw = cw_ref[j]
        y = (u_ref[buf, halo:halo + tm, :] * cw[2:3, :] + u_ref[buf, halo - 1:halo - 1 + tm, :] * cw[1:2, :]
             + u_ref[buf, halo - 2:halo - 2 + tm, :] * cw[0:1, :] + cw[3:4, :])
        h = _silu(y[:, :FF_CHUNK]) * y[:, FF_CHUNK:]
        d = _dot(h.astype(BF16), wdn_ref[j])
        if first:
            acc_ref[...] = d
        else:
            acc_ref[...] += d

    produce(0, 0)
    produce(1, 1)
    consume(0, 0, first=True)

    def pair(p, _):
        j = 2 * p + 1
        produce(j + 1, 0)
        consume(j, 1)
        produce(j + 2, 1)
        consume(j + 1, 0)
        return 0

    lax.fori_loop(0, (N_FF_CHUNKS - 3) // 2, pair, 0)
    produce(N_FF_CHUNKS - 1, 0)
    consume(N_FF_CHUNKS - 2, 1)
    consume(N_FF_CHUNKS - 1, 0)
    z = DEEPNORM_ALPHA * x_ref[...] + acc_ref[...]
    out_ref[...] = _layer_norm(z, lng_ref[...], lnb_ref[...])


def _ffn(x3, wup_c, cw_c, wdn_c, ln_g, ln_b):
    b, seq, _ = x3.shape
    tm = FFN_TILE
    fixed3 = lambda n, i: (0, 0, 0)
    fixed2 = lambda n, i: (0, 0)
    return pl.pallas_call(
        _ffn_kernel,
        grid=(b, seq // tm),
        in_specs=[
            pl.BlockSpec((None, tm, D_MODEL), lambda n, i: (n, i, 0)),
            pl.BlockSpec(wup_c.shape, fixed3, pipeline_mode=pl.Buffered(1)),
            pl.BlockSpec(cw_c.shape, fixed3),
            pl.BlockSpec(wdn_c.shape, fixed3, pipeline_mode=pl.Buffered(1)),
            pl.BlockSpec(ln_g.shape, fixed2),
            pl.BlockSpec(ln_b.shape, fixed2),
        ],
        out_specs=pl.BlockSpec((None, tm, D_MODEL), lambda n, i: (n, i, 0)),
        out_shape=jax.ShapeDtypeStruct((b, seq, D_MODEL), F32),
        scratch_shapes=[
            pltpu.VMEM((2, tm + 8, 2 * FF_CHUNK), F32),
            pltpu.VMEM((N_FF_CHUNKS, 8, 2 * FF_CHUNK), F32),
            pltpu.VMEM((tm, D_MODEL), F32),
            pltpu.VMEM((tm, D_MODEL), BF16),
        ],
        compiler_params=pltpu.CompilerParams(
            dimension_semantics=("arbitrary", "arbitrary"), vmem_limit_bytes=VMEM_LIMIT),
        name="conv_glu_ffn",
    )(x3, wup_c, cw_c, wdn_c, ln_g, ln_b)


def _rope_tables(seq, dim):
    inv = 1.0 / (ROPE_THETA ** (jnp.arange(0, dim, 2, dtype=F32) / dim))
    ang = jnp.arange(seq, dtype=F32)[:, None] * inv[None, :]
    cos, sin = jnp.cos(ang), jnp.sin(ang)
    reps = LANES // dim
    cos_l = jnp.tile(jnp.concatenate([cos, cos], axis=1), (1, reps))
    sin_l = jnp.tile(jnp.concatenate([-sin, sin], axis=1), (1, reps))
    return cos_l, sin_l


def _block_diag_ones(n, group):
    idx = np.arange(n) // group
    return jnp.asarray(idx[:, None] == idx[None, :], BF16)


def _retention_tables():
    c = RET_CHUNK
    h = N_HEADS_R
    lg = jnp.log(1.0 - jnp.power(2.0, -5.0 - jnp.arange(h, dtype=F32)))
    idx = jnp.arange(c, dtype=F32)
    dist = idx[:, None] - idx[None, :]
    dmask = jnp.where(dist >= 0, jnp.exp(lg[:, None, None] * jnp.maximum(dist, 0.0)), 0.0)
    dmask_l = jnp.concatenate([dmask[i] for i in range(h)], axis=1)
    q_dec = jnp.exp(lg[:, None] * (idx + 1.0))
    k_dec = jnp.exp(lg[:, None] * (c - 1.0 - idx))
    chunk_dec = jnp.exp(lg * c)
    qdec_l = jnp.repeat(q_dec.T, QK_DIM_R, axis=1)
    kdec_l = jnp.repeat(k_dec.T, QK_DIM_R, axis=1)
    row_h = np.arange(128) // QK_DIM_R
    col_h = np.arange(256) // V_DIM_R
    bd = jnp.asarray(row_h[:, None] == col_h[None, :], F32)
    cdec = bd * chunk_dec[row_h][:, None]
    return qdec_l, kdec_l, dmask_l, cdec, bd, _block_diag_ones(256, V_DIM_R)


def _gla_tables():
    r = np.arange(CUM_BLOCK)
    tri = jnp.asarray((r[:, None] >= r[None, :]) & (r[:, None] // GLA_CHUNK == r[None, :] // GLA_CHUNK), BF16)
    row_h = np.arange(256) // V_DIM_G
    col_h = np.arange(128) // QK_DIM_G
    bdt = jnp.asarray(row_h[:, None] == col_h[None, :], F32)
    return tri, bdt, _block_diag_ones(256, V_DIM_G)


def kernel(x, w_in, w_alpha, b_alpha, mix_scale, w_out, ln1_g, ln1_b, w_up, conv_w, conv_b, w_down, ln2_g, ln2_b):
    b, seq, d_model = x.shape
    assert (d_model, w_in.shape[0]) == (D_MODEL, DEPTH)
    assert seq % ROW_TILE == 0 and (seq // DILATIONS[2]) % ATT_BLK == 0
    ca, sa = _rope_tables(seq, HEAD_DIM_A)
    cr, sr = _rope_tables(seq, QK_DIM_R)
    ret_tabs = _retention_tables()
    gla_tabs = _gla_tables()
    ones_a = _block_diag_ones(LANES, HEAD_DIM_A)
    pad_cols = 3 * WIDTH_A + 768 + 768 + LANES - w_in.shape[2]

    for l in range(DEPTH):
        w_all = jnp.pad(w_in[l], ((0, 0), (0, pad_cols))).astype(BF16)
        w_al = jnp.pad(w_alpha[l], ((0, LANES - GLA_LOW_RANK), (0, 0))).astype(BF16)
        (q1, k1, v1, q4, k4, v4, q16, k16, v16, r3, g3, la3) = _inproj(
            x, w_all, w_al, b_alpha[l][None, :], ca, sa, cr, sr)

        outs, lses = [], []
        for d, (q, k, v) in zip(DILATIONS, ((q1, k1, v1), (q4, k4, v4), (q16, k16, v16))):
            shp = (b * d, seq // d, WIDTH_A)
            o, lse = _attention(q.reshape(shp), k.reshape(shp), v.reshape(shp))
            if d > 1:
                o = o.reshape(b, d, seq // d, WIDTH_A)
                lse = lse.reshape(b, d, seq // d, LANES)
            outs.append(o)
            lses.append(lse)

        yr, yg = _linear_mixers(r3, g3, la3, ret_tabs, gla_tabs)

        x = _mix_out(x, outs[0], outs[1], outs[2], lses[0], lses[1], lses[2], yr, yg,
                     mix_scale[l][None, :], w_out[l].astype(BF16), ln1_g[l][None, :], ln1_b[l][None, :], ones_a)

        wu = w_up[l].astype(BF16)
        wup_c = jnp.concatenate(
            [wu[:, :D_FF].reshape(D_MODEL, N_FF_CHUNKS, FF_CHUNK), wu[:, D_FF:].reshape(D_MODEL, N_FF_CHUNKS, FF_CHUNK)],
            axis=2).transpose(1, 0, 2)
        taps = jnp.concatenate([conv_w[l], conv_b[l][None, :], jnp.zeros((4, 2 * D_FF), F32)], axis=0)
        cw_c = jnp.concatenate(
            [taps[:, :D_FF].reshape(8, N_FF_CHUNKS, FF_CHUNK), taps[:, D_FF:].reshape(8, N_FF_CHUNKS, FF_CHUNK)],
            axis=2).transpose(1, 0, 2)
        wdn_c = w_down[l].astype(BF16).reshape(N_FF_CHUNKS, FF_CHUNK, D_MODEL)
        x = _ffn(x, wup_c, cw_c, wdn_c, ln2_g[l][None, :], ln2_b[l][None, :])
    return x
```

```python
import jax
import jax.numpy as jnp
import numpy as np
from jax import lax
from jax.experimental import pallas as pl
from jax.experimental.pallas import tpu as pltpu

F32 = jnp.float32
BF16 = jnp.bfloat16

D_MODEL = 1024
DEPTH = 4
HEAD_DIM_A = 64
N_HEADS_A = 8
WIDTH_A = 512
DILATIONS = (1, 4, 16)
ATT_BLK = 128
ATT_STEP_ROWS = 1024
ROPE_THETA = 10000.0
N_HEADS_R = 4
QK_DIM_R = 32
V_DIM_R = 64
RET_CHUNK = 128
N_HEADS_G = 4
QK_DIM_G = 32
V_DIM_G = 64
GLA_LOW_RANK = 16
GLA_TAU = 16.0
GLA_CHUNK = 64
CUM_BLOCK = 256
D_FF = 2816
FF_CHUNK = 256
N_FF_CHUNKS = D_FF // FF_CHUNK
DEEPNORM_ALPHA = (2 * DEPTH) ** 0.25
LOG2E = 1.4426950408889634
LN2 = 0.6931471805599453
LN_EPS = 1e-5
HEAD_NORM_EPS = 1e-6
LANES = 128
ROW_TILE = 512
FFN_TILE = 1024
SUB_ROWS = 128
PART_ROWS = 256
VMEM_LIMIT = 56 * 1024 * 1024


def _iota(shape, dim):
    return lax.broadcasted_iota(jnp.int32, shape, dim)


def _dot(a, b):
    return jnp.dot(a, b, preferred_element_type=F32)


def _dot_nt(a, b):
    return lax.dot_general(a, b, (((1,), (1,)), ((), ())), preferred_element_type=F32)


def _dot_tn(a, b):
    return lax.dot_general(a, b, (((0,), (0,)), ((), ())), preferred_element_type=F32)


def _split_hi_lo(x):
    hi = x.astype(BF16)
    lo = (x - hi.astype(F32)).astype(BF16)
    return hi, lo


def _group_mean(x, ones_bd, group):
    hi, lo = _split_hi_lo(x)
    return (_dot(hi, ones_bd) + _dot(lo, ones_bd)) * (1.0 / group)


def _head_norm(x, ones_bd, group):
    mu = _group_mean(x, ones_bd, group)
    d = x - mu
    var = _group_mean(d * d, ones_bd, group)
    return d * lax.rsqrt(var + HEAD_NORM_EPS)


def _silu(g):
    return g * (1.0 / (1.0 + jnp.exp(-g)))


def _layer_norm(z, g, b):
    mu = jnp.mean(z, axis=-1, keepdims=True)
    d = z - mu
    var = jnp.mean(d * d, axis=-1, keepdims=True)
    return d * lax.rsqrt(var + LN_EPS) * g + b


def _head_stack(t, lane_head, n_heads):
    return jnp.concatenate([jnp.where(lane_head == h, t, jnp.zeros_like(t)) for h in range(n_heads)], axis=0)


def _rope(t, cos, sin_signed, half):
    lane = _iota(t.shape, 1)
    first = (lane % (2 * half)) < half
    partner = jnp.where(first, pltpu.roll(t, LANES - half, 1), pltpu.roll(t, half, 1))
    return t * cos + partner * sin_signed


def _inproj_kernel(x_ref, w_ref, walpha_ref, balpha_ref, ca_ref, sa_ref, cr_ref, sr_ref,
                   q1_ref, k1_ref, v1_ref, q4_ref, k4_ref, v4_ref, q16_ref, k16_ref, v16_ref,
                   r_ref, g_ref, la_ref, slab_ref, cls_ref):
    tm = x_ref.shape[0]
    n_slabs = WIDTH_A // LANES
    xb = x_ref[...].astype(BF16)

    def mm(c0, n):
        return _dot(xb, w_ref[:, c0:c0 + n])

    outs = ((q1_ref, q4_ref, q16_ref), (k1_ref, k4_ref, k16_ref), (v1_ref, v4_ref, v16_ref))
    base = 3 * WIDTH_A

    def finish_a(ti, t):
        for s in range(n_slabs):
            sl = slice(s * LANES, (s + 1) * LANES)
            slab = t[:, sl]
            if ti < 2:
                slab = _rope(slab, ca_ref[...], sa_ref[...], HEAD_DIM_A // 2)
            if ti == 0:
                slab = slab * (LOG2E * HEAD_DIM_A ** -0.5)
            slab_ref[ti * n_slabs + s] = slab
            outs[ti][0][:, sl] = slab.astype(BF16)
        d4, d16 = DILATIONS[1], DILATIONS[2]
        n4 = tm // d4
        for s in range(n_slabs):
            sl = slice(s * LANES, (s + 1) * LANES)
            for r4 in range(d4):
                cls4 = slab_ref[ti * n_slabs + s, pl.ds(r4, n4, stride=d4), :]
                outs[ti][1][r4, :, sl] = cls4.astype(BF16)
                cls_ref[s, r4 * n4:(r4 + 1) * n4, :] = cls4
            for r4 in range(d4):
                for c in range(d16 // d4):
                    cls16 = cls_ref[s, pl.ds(r4 * n4 + c, tm // d16, stride=d16 // d4), :]
                    outs[ti][2][r4 + d4 * c, :, sl] = cls16.astype(BF16)

    def finish_r(rr):
        cr = cr_ref[...]
        sr = sr_ref[...]
        r_ref[:, 0:128] = _rope(rr[:, 0:128], cr, sr, QK_DIM_R // 2).astype(BF16)
        r_ref[:, 128:256] = (_rope(rr[:, 128:256], cr, sr, QK_DIM_R // 2) * (QK_DIM_R ** -0.5)).astype(BF16)
        r_ref[:, 256:768] = rr[:, 256:768].astype(BF16)

    def finish_g(gg):
        g_ref[:, 0:128] = (gg[:, 0:128] * (QK_DIM_G ** -0.5)).astype(BF16)
        g_ref[:, 128:768] = gg[:, 128:768].astype(BF16)
        ag = gg[:, 768:896].astype(BF16)
        z = _dot(ag, walpha_ref[...]) + balpha_ref[...]
        log_sig = jnp.minimum(z, 0.0) - jnp.log(1.0 + jnp.exp(-jnp.abs(z)))
        la_ref[...] = log_sig * (1.0 / GLA_TAU)

    stages = [
        (lambda: mm(0, WIDTH_A), lambda t: finish_a(0, t)),
        (lambda: mm(WIDTH_A, WIDTH_A), lambda t: finish_a(1, t)),
        (lambda: mm(2 * WIDTH_A, WIDTH_A), lambda t: finish_a(2, t)),
        (lambda: mm(base, 768), finish_r),
        (lambda: mm(base + 768, 896), finish_g),
    ]
    prev = None
    for matmul, finish in stages:
        t = matmul()
        if prev is not None:
            prev[0](prev[1])
        prev = (finish, t)
    prev[0](prev[1])


def _inproj(x3, w_all, w_alpha, b_alpha, ca, sa, cr, sr):
    b, seq, _ = x3.shape
    tm = ROW_TILE
    row = lambda n, i: (n, i, 0)
    cls = lambda n, i: (n, 0, i, 0)
    fixed = lambda n, i: (0, 0)
    tab = lambda n, i: (i, 0)
    d4, d16 = DILATIONS[1], DILATIONS[2]
    nat = jax.ShapeDtypeStruct((b, seq, WIDTH_A), BF16)
    c4 = jax.ShapeDtypeStruct((b, d4, seq // d4, WIDTH_A), BF16)
    c16 = jax.ShapeDtypeStruct((b, d16, seq // d16, WIDTH_A), BF16)
    nat_spec = pl.BlockSpec((None, tm, WIDTH_A), row)
    c4_spec = pl.BlockSpec((None, d4, tm // d4, WIDTH_A), cls)
    c16_spec = pl.BlockSpec((None, d16, tm // d16, WIDTH_A), cls)
    return pl.pallas_call(
        _inproj_kernel,
        grid=(b, seq // tm),
        in_specs=[
            pl.BlockSpec((None, tm, D_MODEL), row),
            pl.BlockSpec(w_all.shape, fixed),
            pl.BlockSpec(w_alpha.shape, fixed),
            pl.BlockSpec(b_alpha.shape, fixed),
            pl.BlockSpec((tm, LANES), tab),
            pl.BlockSpec((tm, LANES), tab),
            pl.BlockSpec((tm, LANES), tab),
            pl.BlockSpec((tm, LANES), tab),
        ],
        out_specs=[nat_spec] * 3 + [c4_spec] * 3 + [c16_spec] * 3 + [
            pl.BlockSpec((None, tm, 768), row),
            pl.BlockSpec((None, tm, 768), row),
            pl.BlockSpec((None, tm, LANES), row),
        ],
        out_shape=[nat] * 3 + [c4] * 3 + [c16] * 3 + [
            jax.ShapeDtypeStruct((b, seq, 768), BF16),
            jax.ShapeDtypeStruct((b, seq, 768), BF16),
            jax.ShapeDtypeStruct((b, seq, LANES), F32),
        ],
        scratch_shapes=[
            pltpu.VMEM((3 * (WIDTH_A // LANES), tm, LANES), F32),
            pltpu.VMEM((WIDTH_A // LANES, tm, LANES), F32),
        ],
        compiler_params=pltpu.CompilerParams(
            dimension_semantics=("arbitrary", "arbitrary"), vmem_limit_bytes=VMEM_LIMIT),
        name="inproj",
    )(x3, w_all, w_alpha, b_alpha, ca, sa, cr, sr)


def _attn_kernel(q_ref, k_ref, v_ref, o_ref, lse_ref):
    blk = ATT_BLK
    n_cls, tq, _ = q_ref.shape
    nsub = tq // blk
    n_slabs = WIDTH_A // LANES
    i = pl.program_id(1)

    qi = _iota((blk, 2 * blk), 0)
    kj = _iota((blk, 2 * blk), 1)
    band = (kj >= qi) & (kj <= qi + blk)
    kj0 = kj + jnp.where(i == 0, blk, 0)
    first = (kj0 >= qi) & (kj0 <= qi + blk)
    lane = _iota((blk, LANES), 1)
    low_q = lane < HEAD_DIM_A
    low_v = _iota((2 * blk, LANES), 1) < HEAD_DIM_A

    units = [(g, jb, s, hh) for g in range(n_cls) for jb in range(nsub) for s in range(n_slabs) for hh in range(2)]

    def key_rows(jb):
        if jb == 0:
            start = jnp.maximum(i * tq - blk, 0)
        else:
            start = i * tq + (jb - 1) * blk
        return pl.ds(pl.multiple_of(start, blk), 2 * blk)

    def scores(g, jb, s, hh):
        sl = slice(s * LANES, (s + 1) * LANES)
        qs = q_ref[g, jb * blk:(jb + 1) * blk, sl]
        sel_q = low_q if hh == 0 else jnp.logical_not(low_q)
        qm = jnp.where(sel_q, qs, jnp.zeros_like(qs))
        return _dot_nt(qm, k_ref[g, key_rows(jb), sl])

    def weighted(g, jb, s, hh, sc):
        sl = slice(s * LANES, (s + 1) * LANES)
        sc = jnp.where(first if jb == 0 else band, sc, -jnp.inf)
        m = jnp.max(sc, axis=-1, keepdims=True)
        p = jnp.exp2(sc - m).astype(BF16)
        vs = v_ref[g, key_rows(jb), sl]
        sel_v = low_v if hh == 0 else jnp.logical_not(low_v)
        vm = jnp.where(sel_v, vs, jnp.ones_like(vs))
        r = _dot(p, vm)
        return r, m

    skew = 5
    pending = {}
    done = {}
    den_parts = {}
    max_parts = {}
    grp = (lane % HEAD_DIM_A) // (HEAD_DIM_A // n_slabs)

    def gather_heads(parts):
        tile = parts[n_slabs - 1]
        for s2 in range(n_slabs - 1):
            tile = jnp.where(grp == s2, parts[s2], tile)
        return tile

    for t in range(len(units) + skew):
        if t < len(units):
            pending[t] = scores(*units[t])
        if t >= skew:
            g, jb, s, hh = units[t - skew]
            done[hh] = weighted(g, jb, s, hh, pending.pop(t - skew))
            if hh == 1:
                (r0, m0), (r1, m1) = done[0], done[1]
                num = jnp.where(low_q, r0, r1)
                den_parts[s] = jnp.where(low_q, r1, r0)
                max_parts[s] = jnp.where(low_q, m1, m0)
                den = pltpu.roll(den_parts[s], HEAD_DIM_A, 1)
                o_ref[g, jb * blk:(jb + 1) * blk, s * LANES:(s + 1) * LANES] = (num / den).astype(BF16)
                if s == n_slabs - 1:
                    lse2 = gather_heads(max_parts) + jnp.log2(gather_heads(den_parts))
                    lse_ref[g, jb * blk:(jb + 1) * blk, :] = lse2 * LN2


def _attention(q, k, v):
    nb, length, _ = q.shape
    blk = min(ATT_STEP_ROWS, length)
    n_cls = ATT_STEP_ROWS // blk
    return pl.pallas_call(
        _attn_kernel,
        grid=(nb // n_cls, length // blk),
        in_specs=[
            pl.BlockSpec((n_cls, blk, WIDTH_A), lambda n, i: (n, i, 0)),
            pl.BlockSpec((n_cls, length, WIDTH_A), lambda n, i: (n, 0, 0)),
            pl.BlockSpec((n_cls, length, WIDTH_A), lambda n, i: (n, 0, 0)),
        ],
        out_specs=[
            pl.BlockSpec((n_cls, blk, WIDTH_A), lambda n, i: (n, i, 0)),
            pl.BlockSpec((n_cls, blk, LANES), lambda n, i: (n, i, 0)),
        ],
        out_shape=[
            jax.ShapeDtypeStruct((nb, length, WIDTH_A), BF16),
            jax.ShapeDtypeStruct((nb, length, LANES), F32),
        ],
        compiler_params=pltpu.CompilerParams(
            dimension_semantics=("arbitrary", "arbitrary"), vmem_limit_bytes=VMEM_LIMIT),
        name="dilated_attn",
    )(q, k, v)


def _retention_phases(r_ref, qdec_ref, kdec_ref, dmask_ref, cdec_ref, bd_ref, ones_ref, y_ref,
                      state_ref, qd_ref, o_ref, kv_ref, stb_ref):
    c = RET_CHUNK
    n_chunks = r_ref.shape[0] // c

    @pl.when(pl.program_id(1) == 0)
    def _():
        state_ref[...] = jnp.zeros_like(state_ref)

    lane_q = _iota((c, 128), 1) // QK_DIM_R
    lane_v = _iota((c, 256), 1) // V_DIM_R
    chunk_rows = [slice(ci * c, (ci + 1) * c) for ci in range(n_chunks)]
    sc = {}

    def scores():
        bd = bd_ref[...] > 0
        for ci, rows in enumerate(chunk_rows):
            q = r_ref[rows, 0:128]
            k = r_ref[rows, 128:256]
            sc[ci] = _dot_nt(q, _head_stack(k, lane_q, N_HEADS_R))
            qd_ref[rows, :] = (q.astype(F32) * qdec_ref[...]).astype(BF16)
            kd = (k.astype(F32) * kdec_ref[...]).astype(BF16)
            kv_ref[ci] = jnp.where(bd, _dot_tn(kd, r_ref[rows, 256:512]), 0.0)

    def values():
        for ci, rows in enumerate(chunk_rows):
            p = (sc[ci] * dmask_ref[...]).astype(BF16)
            o_ref[rows, :] = _dot(p, _head_stack(r_ref[rows, 256:512], lane_v, N_HEADS_R))

    def scan():
        st = state_ref[...]
        for ci in range(n_chunks):
            stb_ref[ci] = st.astype(BF16)
            st = cdec_ref[...] * st + kv_ref[ci]
        state_ref[...] = st

    def inter():
        for ci, rows in enumerate(chunk_rows):
            o_ref[rows, :] += _dot(qd_ref[rows, :], stb_ref[ci])

    def finish():
        y = _head_norm(o_ref[...], ones_ref[...], V_DIM_R) * _silu(r_ref[:, 512:768].astype(F32))
        y_ref[...] = y.astype(BF16)

    return [scores, values, scan, inter, finish]


def _gla_phases(g_ref, la_ref, tri_ref, bdt_ref, ones_ref, y_ref,
                state_ref, cum_ref, qt_ref, o_ref, kv_ref, stb_ref):
    c = GLA_CHUNK
    tm = g_ref.shape[0]
    n_chunks = tm // c

    @pl.when(pl.program_id(1) == 0)
    def _():
        state_ref[...] = jnp.zeros_like(state_ref)

    lane_q = _iota((c, 128), 1) // QK_DIM_G
    lane_v = _iota((c, 256), 1) // V_DIM_G
    chunk_rows = [slice(ci * c, (ci + 1) * c) for ci in range(n_chunks)]
    sc = {}
    decays = {}

    def cumsum():
        for part in range(tm // CUM_BLOCK):
            rows = slice(part * CUM_BLOCK, (part + 1) * CUM_BLOCK)
            la_hi, la_lo = _split_hi_lo(la_ref[rows, :])
            cum_ref[rows, :] = _dot(tri_ref[...], la_hi) + _dot(tri_ref[...], la_lo)

    def scores():
        bdt = bdt_ref[...] > 0
        for ci, rows in enumerate(chunk_rows):
            q = g_ref[rows, 0:128].astype(F32)
            k = g_ref[rows, 128:256].astype(F32)
            cum = cum_ref[rows, :]
            last = cum[c - 1:c, :]
            q_t = (q * jnp.exp(cum)).astype(BF16)
            k_t = (k * jnp.exp(-cum)).astype(BF16)
            k_l = (k * jnp.exp(last - cum)).astype(BF16)
            qt_ref[rows, :] = q_t
            sc[ci] = _dot_nt(q_t, _head_stack(k_t, lane_q, N_HEADS_G))
            kv_ref[ci] = jnp.where(bdt, _dot_tn(g_ref[rows, 256:512], k_l), 0.0)
            decays[ci] = jnp.exp(last)

    def values():
        causal = _iota((c, N_HEADS_G * c), 0) >= (_iota((c, N_HEADS_G * c), 1) % c)
        for ci, rows in enumerate(chunk_rows):
            att = jnp.where(causal, sc[ci], 0.0).astype(BF16)
            o_ref[rows, :] = _dot(att, _head_stack(g_ref[rows, 256:512], lane_v, N_HEADS_G))

    def scan():
        st = state_ref[...]
        for ci in range(n_chunks):
            stb_ref[ci] = st.astype(BF16)
            st = decays[ci] * st + kv_ref[ci]
        state_ref[...] = st

    def inter():
        for ci, rows in enumerate(chunk_rows):
            o_ref[rows, :] += _dot_nt(qt_ref[rows, :], stb_ref[ci])

    def finish():
        y = _head_norm(o_ref[...], ones_ref[...], V_DIM_G) * _silu(g_ref[:, 512:768].astype(F32))
        y_ref[...] = y.astype(BF16)

    return [cumsum, scores, values, scan, inter, finish]


N_RET_SCRATCH = 5


def _linear_mixers_kernel(r_ref, qdec_ref, kdec_ref, dmask_ref, cdec_ref, bd_ref, ones_ref,
                          g_ref, la_ref, tri_ref, bdt_ref, yr_ref, yg_ref, *scratch):
    ret = _retention_phases(r_ref, qdec_ref, kdec_ref, dmask_ref, cdec_ref, bd_ref, ones_ref, yr_ref,
                            *scratch[:N_RET_SCRATCH])
    gla = _gla_phases(g_ref, la_ref, tri_ref, bdt_ref, ones_ref, yg_ref, *scratch[N_RET_SCRATCH:])
    gla.pop(0)()
    for ret_phase, gla_phase in zip(ret, gla):
        ret_phase()
        gla_phase()


def _linear_mixers(r3, g3, la3, ret_tabs, gla_tabs):
    b, seq, _ = r3.shape
    tm = ROW_TILE
    qdec, kdec, dmask, cdec, bd, ones = ret_tabs
    tri, bdt, _ = gla_tabs
    fixed2 = lambda n, i: (0, 0)
    row = lambda n, i: (n, i, 0)
    y_shape = jax.ShapeDtypeStruct((b, seq, 256), BF16)
    return pl.pallas_call(
        _linear_mixers_kernel,
        grid=(b, seq // tm),
        in_specs=[
            pl.BlockSpec((None, tm, 768), row),
            pl.BlockSpec(qdec.shape, fixed2),
            pl.BlockSpec(kdec.shape, fixed2),
            pl.BlockSpec(dmask.shape, fixed2),
            pl.BlockSpec(cdec.shape, fixed2),
            pl.BlockSpec(bd.shape, fixed2),
            pl.BlockSpec(ones.shape, fixed2),
            pl.BlockSpec((None, tm, 768), row),
            pl.BlockSpec((None, tm, LANES), row),
            pl.BlockSpec(tri.shape, fixed2),
            pl.BlockSpec(bdt.shape, fixed2),
        ],
        out_specs=[pl.BlockSpec((None, tm, 256), row), pl.BlockSpec((None, tm, 256), row)],
        out_shape=[y_shape, y_shape],
        scratch_shapes=[
            pltpu.VMEM((128, 256), F32),
            pltpu.VMEM((tm, 128), BF16),
            pltpu.VMEM((tm, 256), F32),
            pltpu.VMEM((tm // RET_CHUNK, 128, 256), F32),
            pltpu.VMEM((tm // RET_CHUNK, 128, 256), BF16),
            pltpu.VMEM((256, 128), F32),
            pltpu.VMEM((tm, 128), F32),
            pltpu.VMEM((tm, 128), BF16),
            pltpu.VMEM((tm, 256), F32),
            pltpu.VMEM((tm // GLA_CHUNK, 256, 128), F32),
            pltpu.VMEM((tm // GLA_CHUNK, 256, 128), BF16),
        ],
        compiler_params=pltpu.CompilerParams(
            dimension_semantics=("arbitrary", "arbitrary"), vmem_limit_bytes=VMEM_LIMIT),
        name="linear_mixers",
    )(r3, qdec, kdec, dmask, cdec, bd, ones, g3, la3, tri, bdt)


def _mix_out_kernel(x_ref, o1_ref, o4_ref, o16_ref, l1_ref, l4_ref, l16_ref, yr_ref, yg_ref,
                    ms_ref, wout_ref, lng_ref, lnb_ref, ones_ref, out_ref, on_ref, ln_ref, y_ref):
    tm = x_ref.shape[0]
    n_slabs = WIDTH_A // LANES

    def to_sequence_order(part):
        for bi, (o_ref, l_ref, d) in enumerate(((o4_ref, l4_ref, DILATIONS[1]), (o16_ref, l16_ref, DILATIONS[2]))):
            n = PART_ROWS // d
            src = slice(part * n, (part + 1) * n)
            for r in range(d):
                dst = pl.ds(part * PART_ROWS + r, n, stride=d)
                ln_ref[bi, dst, :] = l_ref[r, src, :]
                for s in range(n_slabs):
                    on_ref[bi, s, dst, :] = o_ref[r, src, s * LANES:(s + 1) * LANES].astype(F32)

    low = _iota((SUB_ROWS, LANES), 1) < HEAD_DIM_A
    ms = ms_ref[...]

    def merge(part):
        units = [(rb, s) for rb in range(part * (PART_ROWS // SUB_ROWS), (part + 1) * (PART_ROWS // SUB_ROWS))
                 for s in range(n_slabs)]
        weights = {}
        mean, dev, var = {}, {}, {}
        for rb, s in units:
            rows = slice(rb * SUB_ROWS, (rb + 1) * SUB_ROWS)
            sl = slice(s * LANES, (s + 1) * LANES)
            if rb not in weights:
                l1 = l1_ref[rows, :]
                l4 = ln_ref[0, rows, :]
                l16 = ln_ref[1, rows, :]
                top = jnp.maximum(jnp.maximum(l1, l4), l16)
                e1 = jnp.exp(l1 - top)
                e4 = jnp.exp(l4 - top)
                e16 = jnp.exp(l16 - top)
                inv = 1.0 / (e1 + e4 + e16)
                weights[rb] = (e4 * inv, e16 * inv)
                y_ref[rows, 512:768] = (yr_ref[rows, :].astype(F32) * ms[:, 512:768]).astype(BF16)
                y_ref[rows, 768:1024] = (yg_ref[rows, :].astype(F32) * ms[:, 768:1024]).astype(BF16)

            def expand(w):
                c0 = HEAD_DIM_A + 16 * s
                c1 = 16 * s
                return jnp.where(low, w[:, c0:c0 + 1], w[:, c1:c1 + 1])

            x4 = expand(weights[rb][0])
            x16 = expand(weights[rb][1])
            merged = ((1.0 - x4 - x16) * o1_ref[rows, sl].astype(F32) + x4 * on_ref[0, s, rows, :]
                      + x16 * on_ref[1, s, rows, :])
            mean[rb, s] = (merged, _group_mean(merged, ones_ref[...], HEAD_DIM_A))
        for u in units:
            merged, mu = mean[u]
            dev[u] = merged - mu
            var[u] = _group_mean(dev[u] * dev[u], ones_ref[...], HEAD_DIM_A)
        for rb, s in units:
            rows = slice(rb * SUB_ROWS, (rb + 1) * SUB_ROWS)
            sl = slice(s * LANES, (s + 1) * LANES)
            ya = dev[rb, s] * lax.rsqrt(var[rb, s] + HEAD_NORM_EPS)
            y_ref[rows, sl] = (ya * ms[:, sl]).astype(BF16)

    def project(part):
        rows = slice(part * PART_ROWS, (part + 1) * PART_ROWS)
        return _dot(y_ref[rows, :], wout_ref[...])

    def finish(part, proj):
        rows = slice(part * PART_ROWS, (part + 1) * PART_ROWS)
        z = DEEPNORM_ALPHA * x_ref[rows, :] + proj
        out_ref[rows, :] = _layer_norm(z, lng_ref[...], lnb_ref[...])

    n_parts = tm // PART_ROWS
    proj = None
    for part in range(n_parts):
        to_sequence_order(part)
        merge(part)
        if proj is not None:
            finish(part - 1, proj)
        proj = project(part)
    finish(n_parts - 1, proj)


def _mix_out(x3, o1, o4, o16, l1, l4, l16, yr, yg, ms, w_out, ln_g, ln_b, ones):
    b, seq, _ = x3.shape
    tm = ROW_TILE
    row = lambda n, i: (n, i, 0)
    cls = lambda n, i: (n, 0, i, 0)
    fixed = lambda n, i: (0, 0)
    d4, d16 = DILATIONS[1], DILATIONS[2]
    return pl.pallas_call(
        _mix_out_kernel,
        grid=(b, seq // tm),
        in_specs=[
            pl.BlockSpec((None, tm, D_MODEL), row),
            pl.BlockSpec((None, tm, WIDTH_A), row),
            pl.BlockSpec((None, d4, tm // d4, WIDTH_A), cls),
            pl.BlockSpec((None, d16, tm // d16, WIDTH_A), cls),
            pl.BlockSpec((None, tm, LANES), row),
            pl.BlockSpec((None, d4, tm // d4, LANES), cls),
            pl.BlockSpec((None, d16, tm // d16, LANES), cls),
            pl.BlockSpec((None, tm, 256), row),
            pl.BlockSpec((None, tm, 256), row),
            pl.BlockSpec(ms.shape, fixed),
            pl.BlockSpec(w_out.shape, fixed),
            pl.BlockSpec(ln_g.shape, fixed),
            pl.BlockSpec(ln_b.shape, fixed),
            pl.BlockSpec(ones.shape, fixed),
        ],
        out_specs=pl.BlockSpec((None, tm, D_MODEL), row),
        out_shape=jax.ShapeDtypeStruct((b, seq, D_MODEL), F32),
        scratch_shapes=[
            pltpu.VMEM((2, WIDTH_A // LANES, tm, LANES), F32),
            pltpu.VMEM((2, tm, LANES), F32),
            pltpu.VMEM((tm, D_MODEL), BF16),
        ],
        compiler_params=pltpu.CompilerParams(
            dimension_semantics=("arbitrary", "arbitrary"), vmem_limit_bytes=VMEM_LIMIT),
        name="mix_out",
    )(x3, o1, o4, o16, l1, l4, l16, yr, yg, ms, w_out, ln_g, ln_b, ones)


def _ffn_kernel(x_ref, wup_ref, cw_ref, wdn_ref, lng_ref, lnb_ref, out_ref, u_ref, carry_ref, acc_ref, xb_ref):
    assert N_FF_CHUNKS % 2 == 1 and N_FF_CHUNKS >= 3
    tm = x_ref.shape[0]
    halo = 8

    @pl.when(pl.program_id(1) == 0)
    def _():
        carry_ref[...] = jnp.zeros_like(carry_ref)

    xb_ref[...] = x_ref[...].astype(BF16)

    n_parts = tm // PART_ROWS

    def produce(j, buf, part):
        rows = slice(part * PART_ROWS, (part + 1) * PART_ROWS)
        if part == 0:
            u_ref[buf, 0:halo, :] = carry_ref[j]
        u_ref[buf, halo + part * PART_ROWS:halo + (part + 1) * PART_ROWS, :] = _dot(xb_ref[rows, :], wup_ref[j])
        if part == n_parts - 1:
            carry_ref[j] = u_ref[buf, tm:tm + halo, :]

    def consume(j, buf, part, first=False):
        cw = cw_ref[j]
        r0 = halo + part * PART_ROWS
        y = (u_ref[buf, r0:r0 + PART_ROWS, :] * cw[2:3, :] + u_ref[buf, r0 - 1:r0 - 1 + PART_ROWS, :] * cw[1:2, :]
             + u_ref[buf, r0 - 2:r0 - 2 + PART_ROWS, :] * cw[0:1, :] + cw[3:4, :])
        h = _silu(y[:, :FF_CHUNK]) * y[:, FF_CHUNK:]
        d = _dot(h.astype(BF16), wdn_ref[j])
        rows = slice(part * PART_ROWS, (part + 1) * PART_ROWS)
        if first:
            acc_ref[rows, :] = d
        else:
            acc_ref[rows, :] += d

    def step(jp, bp, jc, bc, first=False):
        for part in range(n_parts):
            if jp is not None:
                produce(jp, bp, part)
            if jc is not None:
                consume(jc, bc, part, first)

    step(0, 0, None, None)
    step(1, 1, 0, 0, first=True)

    def pair(p, _):
        j = 2 * p + 1
        step(j + 1, 0, j, 1)
        step(j + 2, 1, j + 1, 0)
        return 0

    lax.fori_loop(0, (N_FF_CHUNKS - 3) // 2, pair, 0)
    step(N_FF_CHUNKS - 1, 0, N_FF_CHUNKS - 2, 1)
    step(None, None, N_FF_CHUNKS - 1, 0)
    z = DEEPNORM_ALPHA * x_ref[...] + acc_ref[...]
    out_ref[...] = _layer_norm(z, lng_ref[...], lnb_ref[...])


def _ffn(x3, wup_c, cw_c, wdn_c, ln_g, ln_b):
    b, seq, _ = x3.shape
    tm = FFN_TILE
    fixed3 = lambda n, i: (0, 0, 0)
    fixed2 = lambda n, i: (0, 0)
    return pl.pallas_call(
        _ffn_kernel,
        grid=(b, seq // tm),
        in_specs=[
            pl.BlockSpec((None, tm, D_MODEL), lambda n, i: (n, i, 0)),
            pl.BlockSpec(wup_c.shape, fixed3, pipeline_mode=pl.Buffered(1)),
            pl.BlockSpec(cw_c.shape, fixed3),
            pl.BlockSpec(wdn_c.shape, fixed3, pipeline_mode=pl.Buffered(1)),
            pl.BlockSpec(ln_g.shape, fixed2),
            pl.BlockSpec(ln_b.shape, fixed2),
        ],
        out_specs=pl.BlockSpec((None, tm, D_MODEL), lambda n, i: (n, i, 0)),
        out_shape=jax.ShapeDtypeStruct((b, seq, D_MODEL), F32),
        scratch_shapes=[
            pltpu.VMEM((2, tm + 8, 2 * FF_CHUNK), F32),
            pltpu.VMEM((N_FF_CHUNKS, 8, 2 * FF_CHUNK), F32),
            pltpu.VMEM((tm, D_MODEL), F32),
            pltpu.VMEM((tm, D_MODEL), BF16),
        ],
        compiler_params=pltpu.CompilerParams(
            dimension_semantics=("arbitrary", "arbitrary"), vmem_limit_bytes=VMEM_LIMIT),
        name="conv_glu_ffn",
    )(x3, wup_c, cw_c, wdn_c, ln_g, ln_b)


def _rope_tables(seq, dim):
    inv = 1.0 / (ROPE_THETA ** (jnp.arange(0, dim, 2, dtype=F32) / dim))
    ang = jnp.arange(seq, dtype=F32)[:, None] * inv[None, :]
    cos, sin = jnp.cos(ang), jnp.sin(ang)
    reps = LANES // dim
    cos_l = jnp.tile(jnp.concatenate([cos, cos], axis=1), (1, reps))
    sin_l = jnp.tile(jnp.concatenate([-sin, sin], axis=1), (1, reps))
    return cos_l, sin_l


def _block_diag_ones(n, group):
    idx = np.arange(n) // group
    return jnp.asarray(idx[:, None] == idx[None, :], BF16)


def _retention_tables():
    c = RET_CHUNK
    h = N_HEADS_R
    lg = jnp.log(1.0 - jnp.power(2.0, -5.0 - jnp.arange(h, dtype=F32)))
    idx = jnp.arange(c, dtype=F32)
    dist = idx[:, None] - idx[None, :]
    dmask = jnp.where(dist >= 0, jnp.exp(lg[:, None, None] * jnp.maximum(dist, 0.0)), 0.0)
    dmask_l = jnp.concatenate([dmask[i] for i in range(h)], axis=1)
    q_dec = jnp.exp(lg[:, None] * (idx + 1.0))
    k_dec = jnp.exp(lg[:, None] * (c - 1.0 - idx))
    chunk_dec = jnp.exp(lg * c)
    qdec_l = jnp.repeat(q_dec.T, QK_DIM_R, axis=1)
    kdec_l = jnp.repeat(k_dec.T, QK_DIM_R, axis=1)
    row_h = np.arange(128) // QK_DIM_R
    col_h = np.arange(256) // V_DIM_R
    bd = jnp.asarray(row_h[:, None] == col_h[None, :], F32)
    cdec = bd * chunk_dec[row_h][:, None]
    return qdec_l, kdec_l, dmask_l, cdec, bd, _block_diag_ones(256, V_DIM_R)


def _gla_tables():
    r = np.arange(CUM_BLOCK)
    tri = jnp.asarray((r[:, None] >= r[None, :]) & (r[:, None] // GLA_CHUNK == r[None, :] // GLA_CHUNK), BF16)
    row_h = np.arange(256) // V_DIM_G
    col_h = np.arange(128) // QK_DIM_G
    bdt = jnp.asarray(row_h[:, None] == col_h[None, :], F32)
    return tri, bdt, _block_diag_ones(256, V_DIM_G)


def kernel(x, w_in, w_alpha, b_alpha, mix_scale, w_out, ln1_g, ln1_b, w_up, conv_w, conv_b, w_down, ln2_g, ln2_b):
    b, seq, d_model = x.shape
    assert (d_model, w_in.shape[0]) == (D_MODEL, DEPTH)
    assert seq % ROW_TILE == 0 and (seq // DILATIONS[2]) % ATT_BLK == 0
    ca, sa = _rope_tables(seq, HEAD_DIM_A)
    cr, sr = _rope_tables(seq, QK_DIM_R)
    ret_tabs = _retention_tables()
    gla_tabs = _gla_tables()
    ones_a = _block_diag_ones(LANES, HEAD_DIM_A)
    pad_cols = 3 * WIDTH_A + 768 + 768 + LANES - w_in.shape[2]

    for l in range(DEPTH):
        w_all = jnp.pad(w_in[l], ((0, 0), (0, pad_cols))).astype(BF16)
        w_al = jnp.pad(w_alpha[l], ((0, LANES - GLA_LOW_RANK), (0, 0))).astype(BF16)
        (q1, k1, v1, q4, k4, v4, q16, k16, v16, r3, g3, la3) = _inproj(
            x, w_all, w_al, b_alpha[l][None, :], ca, sa, cr, sr)

        outs, lses = [], []
        for d, (q, k, v) in zip(DILATIONS, ((q1, k1, v1), (q4, k4, v4), (q16, k16, v16))):
            shp = (b * d, seq // d, WIDTH_A)
            o, lse = _attention(q.reshape(shp), k.reshape(shp), v.reshape(shp))
            if d > 1:
                o = o.reshape(b, d, seq // d, WIDTH_A)
                lse = lse.reshape(b, d, seq // d, LANES)
            outs.append(o)
            lses.append(lse)

        yr, yg = _linear_mixers(r3, g3, la3, ret_tabs, gla_tabs)

        x = _mix_out(x, outs[0], outs[1], outs[2], lses[0], lses[1], lses[2], yr, yg,
                     mix_scale[l][None, :], w_out[l].astype(BF16), ln1_g[l][None, :], ln1_b[l][None, :], ones_a)

        wu = w_up[l].astype(BF16)
        wup_c = jnp.concatenate(
            [wu[:, :D_FF].reshape(D_MODEL, N_FF_CHUNKS, FF_CHUNK), wu[:, D_FF:].reshape(D_MODEL, N_FF_CHUNKS, FF_CHUNK)],
            axis=2).transpose(1, 0, 2)
        taps = jnp.concatenate([conv_w[l], conv_b[l][None, :], jnp.zeros((4, 2 * D_FF), F32)], axis=0)
        cw_c = jnp.concatenate(
            [taps[:, :D_FF].reshape(8, N_FF_CHUNKS, FF_CHUNK), taps[:, D_FF:].reshape(8, N_FF_CHUNKS, FF_CHUNK)],
            axis=2).transpose(1, 0, 2)
        wdn_c = w_down[l].astype(BF16).reshape(N_FF_CHUNKS, FF_CHUNK, D_MODEL)
        x = _ffn(x, wup_c, cw_c, wdn_c, ln2_g[l][None, :], ln2_b[l][None, :])
    return x
```

```python
import jax
import jax.numpy as jnp
import numpy as np
from jax import lax
from jax.experimental import pallas as pl
from jax.experimental.pallas import tpu as pltpu

F32 = jnp.float32
BF16 = jnp.bfloat16

D_MODEL = 1024
DEPTH = 4
HEAD_DIM_A = 64
N_HEADS_A = 8
WIDTH_A = 512
DILATIONS = (1, 4, 16)
ATT_BLK = 128
ATT_STEP_ROWS = 1024
ROPE_THETA = 10000.0
N_HEADS_R = 4
QK_DIM_R = 32
V_DIM_R = 64
RET_CHUNK = 128
N_HEADS_G = 4
QK_DIM_G = 32
V_DIM_G = 64
GLA_LOW_RANK = 16
GLA_TAU = 16.0
GLA_CHUNK = 64
CUM_BLOCK = 256
D_FF = 2816
FF_CHUNK = 256
N_FF_CHUNKS = D_FF // FF_CHUNK
DEEPNORM_ALPHA = (2 * DEPTH) ** 0.25
LOG2E = 1.4426950408889634
LN2 = 0.6931471805599453
LN_EPS = 1e-5
HEAD_NORM_EPS = 1e-6
LANES = 128
ROW_TILE = 512
FFN_TILE = 1024
INPROJ_TILE = 1024
SUB_ROWS = 128
PART_ROWS = 256
VMEM_LIMIT = 56 * 1024 * 1024


def _iota(shape, dim):
    return lax.broadcasted_iota(jnp.int32, shape, dim)


def _dot(a, b):
    return jnp.dot(a, b, preferred_element_type=F32)


def _dot_nt(a, b):
    return lax.dot_general(a, b, (((1,), (1,)), ((), ())), preferred_element_type=F32)


def _dot_tn(a, b):
    return lax.dot_general(a, b, (((0,), (0,)), ((), ())), preferred_element_type=F32)


def _split_hi_lo(x):
    hi = x.astype(BF16)
    lo = (x - hi.astype(F32)).astype(BF16)
    return hi, lo


def _group_mean(x, ones_bd, group):
    hi, lo = _split_hi_lo(x)
    return (_dot(hi, ones_bd) + _dot(lo, ones_bd)) * (1.0 / group)


def _head_norm(x, ones_bd, group):
    mu = _group_mean(x, ones_bd, group)
    d = x - mu
    var = _group_mean(d * d, ones_bd, group)
    return d * lax.rsqrt(var + HEAD_NORM_EPS)


def _silu(g):
    return g * (1.0 / (1.0 + jnp.exp(-g)))


def _layer_norm(z, g, b):
    mu = jnp.mean(z, axis=-1, keepdims=True)
    d = z - mu
    var = jnp.mean(d * d, axis=-1, keepdims=True)
    return d * lax.rsqrt(var + LN_EPS) * g + b


def _head_stack(t, lane_head, n_heads):
    return jnp.concatenate([jnp.where(lane_head == h, t, jnp.zeros_like(t)) for h in range(n_heads)], axis=0)


def _rope(t, cos, sin_signed, half):
    lane = _iota(t.shape, 1)
    first = (lane % (2 * half)) < half
    partner = jnp.where(first, pltpu.roll(t, LANES - half, 1), pltpu.roll(t, half, 1))
    return t * cos + partner * sin_signed


def _inproj_kernel(x_ref, w_ref, walpha_ref, balpha_ref, ca_ref, sa_ref, cr_ref, sr_ref,
                   q1_ref, k1_ref, v1_ref, q4_ref, k4_ref, v4_ref, q16_ref, k16_ref, v16_ref,
                   r_ref, g_ref, la_ref, slab_ref, cls_ref):
    tm = x_ref.shape[0]
    n_slabs = WIDTH_A // LANES
    xb = x_ref[...].astype(BF16)

    def mm(c0, n):
        return _dot(xb, w_ref[:, c0:c0 + n])

    outs = ((q1_ref, q4_ref, q16_ref), (k1_ref, k4_ref, k16_ref), (v1_ref, v4_ref, v16_ref))
    base = 3 * WIDTH_A

    def finish_a(ti, t):
        for s in range(n_slabs):
            sl = slice(s * LANES, (s + 1) * LANES)
            slab = t[:, sl]
            if ti < 2:
                slab = _rope(slab, ca_ref[...], sa_ref[...], HEAD_DIM_A // 2)
            if ti == 0:
                slab = slab * (LOG2E * HEAD_DIM_A ** -0.5)
            slab_ref[ti * n_slabs + s] = slab
            outs[ti][0][:, sl] = slab.astype(BF16)
        d4, d16 = DILATIONS[1], DILATIONS[2]
        n4 = tm // d4
        for s in range(n_slabs):
            sl = slice(s * LANES, (s + 1) * LANES)
            for r4 in range(d4):
                cls4 = slab_ref[ti * n_slabs + s, pl.ds(r4, n4, stride=d4), :]
                outs[ti][1][r4, :, sl] = cls4.astype(BF16)
                cls_ref[s, r4 * n4:(r4 + 1) * n4, :] = cls4
            for r4 in range(d4):
                for c in range(d16 // d4):
                    cls16 = cls_ref[s, pl.ds(r4 * n4 + c, tm // d16, stride=d16 // d4), :]
                    outs[ti][2][r4 + d4 * c, :, sl] = cls16.astype(BF16)

    def finish_r(rr):
        cr = cr_ref[...]
        sr = sr_ref[...]
        r_ref[:, 0:128] = _rope(rr[:, 0:128], cr, sr, QK_DIM_R // 2).astype(BF16)
        r_ref[:, 128:256] = (_rope(rr[:, 128:256], cr, sr, QK_DIM_R // 2) * (QK_DIM_R ** -0.5)).astype(BF16)
        r_ref[:, 256:768] = rr[:, 256:768].astype(BF16)

    def finish_g(gg):
        g_ref[:, 0:128] = (gg[:, 0:128] * (QK_DIM_G ** -0.5)).astype(BF16)
        g_ref[:, 128:768] = gg[:, 128:768].astype(BF16)
        ag = gg[:, 768:896].astype(BF16)
        z = _dot(ag, walpha_ref[...]) + balpha_ref[...]
        log_sig = jnp.minimum(z, 0.0) - jnp.log(1.0 + jnp.exp(-jnp.abs(z)))
        la_ref[...] = log_sig * (1.0 / GLA_TAU)

    stages = [
        (lambda: mm(0, WIDTH_A), lambda t: finish_a(0, t)),
        (lambda: mm(WIDTH_A, WIDTH_A), lambda t: finish_a(1, t)),
        (lambda: mm(2 * WIDTH_A, WIDTH_A), lambda t: finish_a(2, t)),
        (lambda: mm(base, 768), finish_r),
        (lambda: mm(base + 768, 896), finish_g),
    ]
    prev = None
    for matmul, finish in stages:
        t = matmul()
        if prev is not None:
            prev[0](prev[1])
        prev = (finish, t)
    prev[0](prev[1])


def _inproj(x3, w_all, w_alpha, b_alpha, ca, sa, cr, sr):
    b, seq, _ = x3.shape
    tm = INPROJ_TILE
    row = lambda n, i: (n, i, 0)
    cls = lambda n, i: (n, 0, i, 0)
    fixed = lambda n, i: (0, 0)
    tab = lambda n, i: (i, 0)
    d4, d16 = DILATIONS[1], DILATIONS[2]
    nat = jax.ShapeDtypeStruct((b, seq, WIDTH_A), BF16)
    c4 = jax.ShapeDtypeStruct((b, d4, seq // d4, WIDTH_A), BF16)
    c16 = jax.ShapeDtypeStruct((b, d16, seq // d16, WIDTH_A), BF16)
    nat_spec = pl.BlockSpec((None, tm, WIDTH_A), row)
    c4_spec = pl.BlockSpec((None, d4, tm // d4, WIDTH_A), cls)
    c16_spec = pl.BlockSpec((None, d16, tm // d16, WIDTH_A), cls)
    return pl.pallas_call(
        _inproj_kernel,
        grid=(b, seq // tm),
        in_specs=[
            pl.BlockSpec((None, tm, D_MODEL), row),
            pl.BlockSpec(w_all.shape, fixed, pipeline_mode=pl.Buffered(1)),
            pl.BlockSpec(w_alpha.shape, fixed),
            pl.BlockSpec(b_alpha.shape, fixed),
            pl.BlockSpec((tm, LANES), tab),
            pl.BlockSpec((tm, LANES), tab),
            pl.BlockSpec((tm, LANES), tab),
            pl.BlockSpec((tm, LANES), tab),
        ],
        out_specs=[nat_spec] * 3 + [c4_spec] * 3 + [c16_spec] * 3 + [
            pl.BlockSpec((None, tm, 768), row),
            pl.BlockSpec((None, tm, 768), row),
            pl.BlockSpec((None, tm, LANES), row),
        ],
        out_shape=[nat] * 3 + [c4] * 3 + [c16] * 3 + [
            jax.ShapeDtypeStruct((b, seq, 768), BF16),
            jax.ShapeDtypeStruct((b, seq, 768), BF16),
            jax.ShapeDtypeStruct((b, seq, LANES), F32),
        ],
        scratch_shapes=[
            pltpu.VMEM((3 * (WIDTH_A // LANES), tm, LANES), F32),
            pltpu.VMEM((WIDTH_A // LANES, tm, LANES), F32),
        ],
        compiler_params=pltpu.CompilerParams(
            dimension_semantics=("arbitrary", "arbitrary"), vmem_limit_bytes=VMEM_LIMIT),
        name="inproj",
    )(x3, w_all, w_alpha, b_alpha, ca, sa, cr, sr)


def _attn_kernel(q_ref, k_ref, v_ref, o_ref, lse_ref):
    blk = ATT_BLK
    n_cls, tq, _ = q_ref.shape
    nsub = tq // blk
    n_slabs = WIDTH_A // LANES
    i = pl.program_id(1)

    qi = _iota((blk, 2 * blk), 0)
    kj = _iota((blk, 2 * blk), 1)
    band = (kj >= qi) & (kj <= qi + blk)
    kj0 = kj + jnp.where(i == 0, blk, 0)
    first = (kj0 >= qi) & (kj0 <= qi + blk)
    lane = _iota((blk, LANES), 1)
    low_q = lane < HEAD_DIM_A
    low_v = _iota((2 * blk, LANES), 1) < HEAD_DIM_A

    units = [(g, jb, s, hh) for g in range(n_cls) for jb in range(nsub) for s in range(n_slabs) for hh in range(2)]

    def key_rows(jb):
        if jb == 0:
            start = jnp.maximum(i * tq - blk, 0)
        else:
            start = i * tq + (jb - 1) * blk
        return pl.ds(pl.multiple_of(start, blk), 2 * blk)

    def scores(g, jb, s, hh):
        sl = slice(s * LANES, (s + 1) * LANES)
        qs = q_ref[g, jb * blk:(jb + 1) * blk, sl]
        sel_q = low_q if hh == 0 else jnp.logical_not(low_q)
        qm = jnp.where(sel_q, qs, jnp.zeros_like(qs))
        return _dot_nt(qm, k_ref[g, key_rows(jb), sl])

    def weighted(g, jb, s, hh, sc):
        sl = slice(s * LANES, (s + 1) * LANES)
        sc = jnp.where(first if jb == 0 else band, sc, -jnp.inf)
        m = jnp.max(sc, axis=-1, keepdims=True)
        p = jnp.exp2(sc - m).astype(BF16)
        vs = v_ref[g, key_rows(jb), sl]
        sel_v = low_v if hh == 0 else jnp.logical_not(low_v)
        vm = jnp.where(sel_v, vs, jnp.ones_like(vs))
        r = _dot(p, vm)
        return r, m

    skew = 5
    pending = {}
    done = {}
    den_parts = {}
    max_parts = {}
    grp = (lane % HEAD_DIM_A) // (HEAD_DIM_A // n_slabs)

    def gather_heads(parts):
        tile = parts[n_slabs - 1]
        for s2 in range(n_slabs - 1):
            tile = jnp.where(grp == s2, parts[s2], tile)
        return tile

    for t in range(len(units) + skew):
        if t < len(units):
            pending[t] = scores(*units[t])
        if t >= skew:
            g, jb, s, hh = units[t - skew]
            done[hh] = weighted(g, jb, s, hh, pending.pop(t - skew))
            if hh == 1:
                (r0, m0), (r1, m1) = done[0], done[1]
                num = jnp.where(low_q, r0, r1)
                den_parts[s] = jnp.where(low_q, r1, r0)
                max_parts[s] = jnp.where(low_q, m1, m0)
                den = pltpu.roll(den_parts[s], HEAD_DIM_A, 1)
                o_ref[g, jb * blk:(jb + 1) * blk, s * LANES:(s + 1) * LANES] = (num / den).astype(BF16)
                if s == n_slabs - 1:
                    lse2 = gather_heads(max_parts) + jnp.log2(gather_heads(den_parts))
                    lse_ref[g, jb * blk:(jb + 1) * blk, :] = lse2 * LN2


def _attention(q, k, v):
    nb, length, _ = q.shape
    blk = min(ATT_STEP_ROWS, length)
    n_cls = ATT_STEP_ROWS // blk
    return pl.pallas_call(
        _attn_kernel,
        grid=(nb // n_cls, length // blk),
        in_specs=[
            pl.BlockSpec((n_cls, blk, WIDTH_A), lambda n, i: (n, i, 0)),
            pl.BlockSpec((n_cls, length, WIDTH_A), lambda n, i: (n, 0, 0)),
            pl.BlockSpec((n_cls, length, WIDTH_A), lambda n, i: (n, 0, 0)),
        ],
        out_specs=[
            pl.BlockSpec((n_cls, blk, WIDTH_A), lambda n, i: (n, i, 0)),
            pl.BlockSpec((n_cls, blk, LANES), lambda n, i: (n, i, 0)),
        ],
        out_shape=[
            jax.ShapeDtypeStruct((nb, length, WIDTH_A), BF16),
            jax.ShapeDtypeStruct((nb, length, LANES), F32),
        ],
        compiler_params=pltpu.CompilerParams(
            dimension_semantics=("arbitrary", "arbitrary"), vmem_limit_bytes=VMEM_LIMIT),
        name="dilated_attn",
    )(q, k, v)


def _retention_phases(r_ref, qdec_ref, kdec_ref, dmask_ref, cdec_ref, bd_ref, ones_ref, y_ref,
                      state_ref, qd_ref, o_ref, kv_ref, stb_ref):
    c = RET_CHUNK
    n_chunks = r_ref.shape[0] // c

    @pl.when(pl.program_id(1) == 0)
    def _():
        state_ref[...] = jnp.zeros_like(state_ref)

    lane_q = _iota((c, 128), 1) // QK_DIM_R
    lane_v = _iota((c, 256), 1) // V_DIM_R
    chunk_rows = [slice(ci * c, (ci + 1) * c) for ci in range(n_chunks)]
    sc = {}

    def scores():
        bd = bd_ref[...] > 0
        for ci, rows in enumerate(chunk_rows):
            q = r_ref[rows, 0:128]
            k = r_ref[rows, 128:256]
            sc[ci] = _dot_nt(q, _head_stack(k, lane_q, N_HEADS_R))
            qd_ref[rows, :] = (q.astype(F32) * qdec_ref[...]).astype(BF16)
            kd = (k.astype(F32) * kdec_ref[...]).astype(BF16)
            kv_ref[ci] = jnp.where(bd, _dot_tn(kd, r_ref[rows, 256:512]), 0.0)

    def values():
        for ci, rows in enumerate(chunk_rows):
            p = (sc[ci] * dmask_ref[...]).astype(BF16)
            o_ref[rows, :] = _dot(p, _head_stack(r_ref[rows, 256:512], lane_v, N_HEADS_R))

    def scan():
        st = state_ref[...]
        for ci in range(n_chunks):
            stb_ref[ci] = st.astype(BF16)
            st = cdec_ref[...] * st + kv_ref[ci]
        state_ref[...] = st

    def inter():
        for ci, rows in enumerate(chunk_rows):
            o_ref[rows, :] += _dot(qd_ref[rows, :], stb_ref[ci])

    norm = {}

    def center():
        o = o_ref[...]
        norm["dev"] = o - _group_mean(o, ones_ref[...], V_DIM_R)

    def spread():
        norm["var"] = _group_mean(norm["dev"] * norm["dev"], ones_ref[...], V_DIM_R)

    def finish():
        y = norm["dev"] * lax.rsqrt(norm["var"] + HEAD_NORM_EPS) * _silu(r_ref[:, 512:768].astype(F32))
        y_ref[...] = y.astype(BF16)

    return [scores, values, scan, inter, center, spread, finish]


def _gla_phases(g_ref, la_ref, tri_ref, bdt_ref, ones_ref, y_ref,
                state_ref, cum_ref, qt_ref, o_ref, kv_ref, stb_ref):
    c = GLA_CHUNK
    tm = g_ref.shape[0]
    n_chunks = tm // c

    @pl.when(pl.program_id(1) == 0)
    def _():
        state_ref[...] = jnp.zeros_like(state_ref)

    lane_q = _iota((c, 128), 1) // QK_DIM_G
    lane_v = _iota((c, 256), 1) // V_DIM_G
    chunk_rows = [slice(ci * c, (ci + 1) * c) for ci in range(n_chunks)]
    sc = {}
    decays = {}

    def cumsum():
        for part in range(tm // CUM_BLOCK):
            rows = slice(part * CUM_BLOCK, (part + 1) * CUM_BLOCK)
            la_hi, la_lo = _split_hi_lo(la_ref[rows, :])
            cum_ref[rows, :] = _dot(tri_ref[...], la_hi) + _dot(tri_ref[...], la_lo)

    def scores():
        bdt = bdt_ref[...] > 0
        for ci, rows in enumerate(chunk_rows):
            q = g_ref[rows, 0:128].astype(F32)
            k = g_ref[rows, 128:256].astype(F32)
            cum = cum_ref[rows, :]
            last = cum[c - 1:c, :]
            q_t = (q * jnp.exp(cum)).astype(BF16)
            k_t = (k * jnp.exp(-cum)).astype(BF16)
            k_l = (k * jnp.exp(last - cum)).astype(BF16)
            qt_ref[rows, :] = q_t
            sc[ci] = _dot_nt(q_t, _head_stack(k_t, lane_q, N_HEADS_G))
            kv_ref[ci] = jnp.where(bdt, _dot_tn(g_ref[rows, 256:512], k_l), 0.0)
            decays[ci] = jnp.exp(last)

    def values():
        causal = _iota((c, N_HEADS_G * c), 0) >= (_iota((c, N_HEADS_G * c), 1) % c)
        for ci, rows in enumerate(chunk_rows):
            att = jnp.where(causal, sc[ci], 0.0).astype(BF16)
            o_ref[rows, :] = _dot(att, _head_stack(g_ref[rows, 256:512], lane_v, N_HEADS_G))

    def scan():
        st = state_ref[...]
        for ci in range(n_chunks):
            stb_ref[ci] = st.astype(BF16)
            st = decays[ci] * st + kv_ref[ci]
        state_ref[...] = st

    def inter():
        for ci, rows in enumerate(chunk_rows):
            o_ref[rows, :] += _dot_nt(qt_ref[rows, :], stb_ref[ci])

    norm = {}

    def center():
        o = o_ref[...]
        norm["dev"] = o - _group_mean(o, ones_ref[...], V_DIM_G)

    def spread():
        norm["var"] = _group_mean(norm["dev"] * norm["dev"], ones_ref[...], V_DIM_G)

    def finish():
        y = norm["dev"] * lax.rsqrt(norm["var"] + HEAD_NORM_EPS) * _silu(g_ref[:, 512:768].astype(F32))
        y_ref[...] = y.astype(BF16)

    return [cumsum, scores, values, scan, inter, center, spread, finish]


N_RET_SCRATCH = 5


def _linear_mixers_kernel(r_ref, qdec_ref, kdec_ref, dmask_ref, cdec_ref, bd_ref, ones_ref,
                          g_ref, la_ref, tri_ref, bdt_ref, yr_ref, yg_ref, *scratch):
    ret = _retention_phases(r_ref, qdec_ref, kdec_ref, dmask_ref, cdec_ref, bd_ref, ones_ref, yr_ref,
                            *scratch[:N_RET_SCRATCH])
    gla = _gla_phases(g_ref, la_ref, tri_ref, bdt_ref, ones_ref, yg_ref, *scratch[N_RET_SCRATCH:])
    gla.pop(0)()
    for ret_phase, gla_phase in zip(ret, gla):
        ret_phase()
        gla_phase()


def _linear_mixers(r3, g3, la3, ret_tabs, gla_tabs):
    b, seq, _ = r3.shape
    tm = ROW_TILE
    qdec, kdec, dmask, cdec, bd, ones = ret_tabs
    tri, bdt, _ = gla_tabs
    fixed2 = lambda n, i: (0, 0)
    row = lambda n, i: (n, i, 0)
    y_shape = jax.ShapeDtypeStruct((b, seq, 256), BF16)
    return pl.pallas_call(
        _linear_mixers_kernel,
        grid=(b, seq // tm),
        in_specs=[
            pl.BlockSpec((None, tm, 768), row),
            pl.BlockSpec(qdec.shape, fixed2),
            pl.BlockSpec(kdec.shape, fixed2),
            pl.BlockSpec(dmask.shape, fixed2),
            pl.BlockSpec(cdec.shape, fixed2),
            pl.BlockSpec(bd.shape, fixed2),
            pl.BlockSpec(ones.shape, fixed2),
            pl.BlockSpec((None, tm, 768), row),
            pl.BlockSpec((None, tm, LANES), row),
            pl.BlockSpec(tri.shape, fixed2),
            pl.BlockSpec(bdt.shape, fixed2),
        ],
        out_specs=[pl.BlockSpec((None, tm, 256), row), pl.BlockSpec((None, tm, 256), row)],
        out_shape=[y_shape, y_shape],
        scratch_shapes=[
            pltpu.VMEM((128, 256), F32),
            pltpu.VMEM((tm, 128), BF16),
            pltpu.VMEM((tm, 256), F32),
            pltpu.VMEM((tm // RET_CHUNK, 128, 256), F32),
            pltpu.VMEM((tm // RET_CHUNK, 128, 256), BF16),
            pltpu.VMEM((256, 128), F32),
            pltpu.VMEM((tm, 128), F32),
            pltpu.VMEM((tm, 128), BF16),
            pltpu.VMEM((tm, 256), F32),
            pltpu.VMEM((tm // GLA_CHUNK, 256, 128), F32),
            pltpu.VMEM((tm // GLA_CHUNK, 256, 128), BF16),
        ],
        compiler_params=pltpu.CompilerParams(
            dimension_semantics=("arbitrary", "arbitrary"), vmem_limit_bytes=VMEM_LIMIT),
        name="linear_mixers",
    )(r3, qdec, kdec, dmask, cdec, bd, ones, g3, la3, tri, bdt)


def _mix_out_kernel(x_ref, o1_ref, o4_ref, o16_ref, l1_ref, l4_ref, l16_ref, yr_ref, yg_ref,
                    ms_ref, wout_ref, lng_ref, lnb_ref, ones_ref, out_ref, on_ref, ln_ref, y_ref):
    tm = x_ref.shape[0]
    n_slabs = WIDTH_A // LANES

    def to_sequence_order(part):
        for bi, (o_ref, l_ref, d) in enumerate(((o4_ref, l4_ref, DILATIONS[1]), (o16_ref, l16_ref, DILATIONS[2]))):
            n = PART_ROWS // d
            src = slice(part * n, (part + 1) * n)
            for r in range(d):
                dst = pl.ds(part * PART_ROWS + r, n, stride=d)
                ln_ref[bi, dst, :] = l_ref[r, src, :]
                for s in range(n_slabs):
                    on_ref[bi, s, dst, :] = o_ref[r, src, s * LANES:(s + 1) * LANES].astype(F32)

    low = _iota((SUB_ROWS, LANES), 1) < HEAD_DIM_A
    ms = ms_ref[...]

    def merge(rb):
        rows = slice(rb * SUB_ROWS, (rb + 1) * SUB_ROWS)
        l1 = l1_ref[rows, :]
        l4 = ln_ref[0, rows, :]
        l16 = ln_ref[1, rows, :]
        top = jnp.maximum(jnp.maximum(l1, l4), l16)
        e1 = jnp.exp(l1 - top)
        e4 = jnp.exp(l4 - top)
        e16 = jnp.exp(l16 - top)
        inv = 1.0 / (e1 + e4 + e16)
        w4 = e4 * inv
        w16 = e16 * inv
        for s in range(n_slabs):
            sl = slice(s * LANES, (s + 1) * LANES)

            def expand(w):
                c0 = HEAD_DIM_A + 16 * s
                c1 = 16 * s
                return jnp.where(low, w[:, c0:c0 + 1], w[:, c1:c1 + 1])

            x4 = expand(w4)
            x16 = expand(w16)
            merged = ((1.0 - x4 - x16) * o1_ref[rows, sl].astype(F32) + x4 * on_ref[0, s, rows, :]
                      + x16 * on_ref[1, s, rows, :])
            ya = _head_norm(merged, ones_ref[...], HEAD_DIM_A)
            y_ref[rows, sl] = (ya * ms[:, sl]).astype(BF16)
        y_ref[rows, 512:768] = (yr_ref[rows, :].astype(F32) * ms[:, 512:768]).astype(BF16)
        y_ref[rows, 768:1024] = (yg_ref[rows, :].astype(F32) * ms[:, 768:1024]).astype(BF16)

    def project(part):
        rows = slice(part * PART_ROWS, (part + 1) * PART_ROWS)
        return _dot(y_ref[rows, :], wout_ref[...])

    def finish(part, proj):
        rows = slice(part * PART_ROWS, (part + 1) * PART_ROWS)
        z = DEEPNORM_ALPHA * x_ref[rows, :] + proj
        out_ref[rows, :] = _layer_norm(z, lng_ref[...], lnb_ref[...])

    n_parts = tm // PART_ROWS
    proj = None
    for part in range(n_parts):
        to_sequence_order(part)
        for rb in range(part * (PART_ROWS // SUB_ROWS), (part + 1) * (PART_ROWS // SUB_ROWS)):
            merge(rb)
        if proj is not None:
            finish(part - 1, proj)
        proj = project(part)
    finish(n_parts - 1, proj)


def _mix_out(x3, o1, o4, o16, l1, l4, l16, yr, yg, ms, w_out, ln_g, ln_b, ones):
    b, seq, _ = x3.shape
    tm = ROW_TILE
    row = lambda n, i: (n, i, 0)
    cls = lambda n, i: (n, 0, i, 0)
    fixed = lambda n, i: (0, 0)
    d4, d16 = DILATIONS[1], DILATIONS[2]
    return pl.pallas_call(
        _mix_out_kernel,
        grid=(b, seq // tm),
        in_specs=[
            pl.BlockSpec((None, tm, D_MODEL), row),
            pl.BlockSpec((None, tm, WIDTH_A), row),
            pl.BlockSpec((None, d4, tm // d4, WIDTH_A), cls),
            pl.BlockSpec((None, d16, tm // d16, WIDTH_A), cls),
            pl.BlockSpec((None, tm, LANES), row),
            pl.BlockSpec((None, d4, tm // d4, LANES), cls),
            pl.BlockSpec((None, d16, tm // d16, LANES), cls),
            pl.BlockSpec((None, tm, 256), row),
            pl.BlockSpec((None, tm, 256), row),
            pl.BlockSpec(ms.shape, fixed),
            pl.BlockSpec(w_out.shape, fixed),
            pl.BlockSpec(ln_g.shape, fixed),
            pl.BlockSpec(ln_b.shape, fixed),
            pl.BlockSpec(ones.shape, fixed),
        ],
        out_specs=pl.BlockSpec((None, tm, D_MODEL), row),
        out_shape=jax.ShapeDtypeStruct((b, seq, D_MODEL), F32),
        scratch_shapes=[
            pltpu.VMEM((2, WIDTH_A // LANES, tm, LANES), F32),
            pltpu.VMEM((2, tm, LANES), F32),
            pltpu.VMEM((tm, D_MODEL), BF16),
        ],
        compiler_params=pltpu.CompilerParams(
            dimension_semantics=("arbitrary", "arbitrary"), vmem_limit_bytes=VMEM_LIMIT),
        name="mix_out",
    )(x3, o1, o4, o16, l1, l4, l16, yr, yg, ms, w_out, ln_g, ln_b, ones)


def _ffn_kernel(x_ref, wup_ref, cw_ref, wdn_ref, lng_ref, lnb_ref, out_ref, u_ref, carry_ref, acc_ref, xb_ref):
    assert N_FF_CHUNKS % 2 == 1 and N_FF_CHUNKS >= 3
    tm = x_ref.shape[0]
    halo = 8

    @pl.when(pl.program_id(1) == 0)
    def _():
        carry_ref[...] = jnp.zeros_like(carry_ref)

    xb_ref[...] = x_ref[...].astype(BF16)

    def produce(j, buf):
        u_ref[buf, 0:halo, :] = carry_ref[j]
        u_ref[buf, halo:halo + tm, :] = _dot(xb_ref[...], wup_ref[j])
        carry_ref[j] = u_ref[buf, tm:tm + halo, :]

    def consume(j, buf, first=False):
        cw = cw_ref[j]
        y = (u_ref[buf, halo:halo + tm, :] * cw[2:3, :] + u_ref[buf, halo - 1:halo - 1 + tm, :] * cw[1:2, :]
             + u_ref[buf, halo - 2:halo - 2 + tm, :] * cw[0:1, :] + cw[3:4, :])
        h = _silu(y[:, :FF_CHUNK]) * y[:, FF_CHUNK:]
        d = _dot(h.astype(BF16), wdn_ref[j])
        if first:
            acc_ref[...] = d
        else:
            acc_ref[...] += d

    produce(0, 0)
    produce(1, 1)
    consume(0, 0, first=True)

    def pair(p, _):
        j = 2 * p + 1
        produce(j + 1, 0)
        consume(j, 1)
        produce(j + 2, 1)
        consume(j + 1, 0)
        return 0

    lax.fori_loop(0, (N_FF_CHUNKS - 3) // 2, pair, 0)
    produce(N_FF_CHUNKS - 1, 0)
    consume(N_FF_CHUNKS - 2, 1)
    consume(N_FF_CHUNKS - 1, 0)
    z = DEEPNORM_ALPHA * x_ref[...] + acc_ref[...]
    out_ref[...] = _layer_norm(z, lng_ref[...], lnb_ref[...])


def _ffn(x3, wup_c, cw_c, wdn_c, ln_g, ln_b):
    b, seq, _ = x3.shape
    tm = FFN_TILE
    fixed3 = lambda n, i: (0, 0, 0)
    fixed2 = lambda n, i: (0, 0)
    return pl.pallas_call(
        _ffn_kernel,
        grid=(b, seq // tm),
        in_specs=[
            pl.BlockSpec((None, tm, D_MODEL), lambda n, i: (n, i, 0)),
            pl.BlockSpec(wup_c.shape, fixed3, pipeline_mode=pl.Buffered(1)),
            pl.BlockSpec(cw_c.shape, fixed3),
            pl.BlockSpec(wdn_c.shape, fixed3, pipeline_mode=pl.Buffered(1)),
            pl.BlockSpec(ln_g.shape, fixed2),
            pl.BlockSpec(ln_b.shape, fixed2),
        ],
        out_specs=pl.BlockSpec((None, tm, D_MODEL), lambda n, i: (n, i, 0)),
        out_shape=jax.ShapeDtypeStruct((b, seq, D_MODEL), F32),
        scratch_shapes=[
            pltpu.VMEM((2, tm + 8, 2 * FF_CHUNK), F32),
            pltpu.VMEM((N_FF_CHUNKS, 8, 2 * FF_CHUNK), F32),
            pltpu.VMEM((tm, D_MODEL), F32),
            pltpu.VMEM((tm, D_MODEL), BF16),
        ],
        compiler_params=pltpu.CompilerParams(
            dimension_semantics=("arbitrary", "arbitrary"), vmem_limit_bytes=VMEM_LIMIT),
        name="conv_glu_ffn",
    )(x3, wup_c, cw_c, wdn_c, ln_g, ln_b)


def _rope_tables(seq, dim):
    inv = 1.0 / (ROPE_THETA ** (jnp.arange(0, dim, 2, dtype=F32) / dim))
    ang = jnp.arange(seq, dtype=F32)[:, None] * inv[None, :]
    cos, sin = jnp.cos(ang), jnp.sin(ang)
    reps = LANES // dim
    cos_l = jnp.tile(jnp.concatenate([cos, cos], axis=1), (1, reps))
    sin_l = jnp.tile(jnp.concatenate([-sin, sin], axis=1), (1, reps))
    return cos_l, sin_l


def _block_diag_ones(n, group):
    idx = np.arange(n) // group
    return jnp.asarray(idx[:, None] == idx[None, :], BF16)


def _retention_tables():
    c = RET_CHUNK
    h = N_HEADS_R
    lg = jnp.log(1.0 - jnp.power(2.0, -5.0 - jnp.arange(h, dtype=F32)))
    idx = jnp.arange(c, dtype=F32)
    dist = idx[:, None] - idx[None, :]
    dmask = jnp.where(dist >= 0, jnp.exp(lg[:, None, None] * jnp.maximum(dist, 0.0)), 0.0)
    dmask_l = jnp.concatenate([dmask[i] for i in range(h)], axis=1)
    q_dec = jnp.exp(lg[:, None] * (idx + 1.0))
    k_dec = jnp.exp(lg[:, None] * (c - 1.0 - idx))
    chunk_dec = jnp.exp(lg * c)
    qdec_l = jnp.repeat(q_dec.T, QK_DIM_R, axis=1)
    kdec_l = jnp.repeat(k_dec.T, QK_DIM_R, axis=1)
    row_h = np.arange(128) // QK_DIM_R
    col_h = np.arange(256) // V_DIM_R
    bd = jnp.asarray(row_h[:, None] == col_h[None, :], F32)
    cdec = bd * chunk_dec[row_h][:, None]
    return qdec_l, kdec_l, dmask_l, cdec, bd, _block_diag_ones(256, V_DIM_R)


def _gla_tables():
    r = np.arange(CUM_BLOCK)
    tri = jnp.asarray((r[:, None] >= r[None, :]) & (r[:, None] // GLA_CHUNK == r[None, :] // GLA_CHUNK), BF16)
    row_h = np.arange(256) // V_DIM_G
    col_h = np.arange(128) // QK_DIM_G
    bdt = jnp.asarray(row_h[:, None] == col_h[None, :], F32)
    return tri, bdt, _block_diag_ones(256, V_DIM_G)


def kernel(x, w_in, w_alpha, b_alpha, mix_scale, w_out, ln1_g, ln1_b, w_up, conv_w, conv_b, w_down, ln2_g, ln2_b):
    b, seq, d_model = x.shape
    assert (d_model, w_in.shape[0]) == (D_MODEL, DEPTH)
    assert seq % max(ROW_TILE, FFN_TILE, INPROJ_TILE) == 0 and (seq // DILATIONS[2]) % ATT_BLK == 0
    ca, sa = _rope_tables(seq, HEAD_DIM_A)
    cr, sr = _rope_tables(seq, QK_DIM_R)
    ret_tabs = _retention_tables()
    gla_tabs = _gla_tables()
    ones_a = _block_diag_ones(LANES, HEAD_DIM_A)
    pad_cols = 3 * WIDTH_A + 768 + 768 + LANES - w_in.shape[2]

    for l in range(DEPTH):
        w_all = jnp.pad(w_in[l], ((0, 0), (0, pad_cols))).astype(BF16)
        w_al = jnp.pad(w_alpha[l], ((0, LANES - GLA_LOW_RANK), (0, 0))).astype(BF16)
        (q1, k1, v1, q4, k4, v4, q16, k16, v16, r3, g3, la3) = _inproj(
            x, w_all, w_al, b_alpha[l][None, :], ca, sa, cr, sr)

        outs, lses = [], []
        for d, (q, k, v) in zip(DILATIONS, ((q1, k1, v1), (q4, k4, v4), (q16, k16, v16))):
            shp = (b * d, seq // d, WIDTH_A)
            o, lse = _attention(q.reshape(shp), k.reshape(shp), v.reshape(shp))
            if d > 1:
                o = o.reshape(b, d, seq // d, WIDTH_A)
                lse = lse.reshape(b, d, seq // d, LANES)
            outs.append(o)
            lses.append(lse)

        yr, yg = _linear_mixers(r3, g3, la3, ret_tabs, gla_tabs)

        x = _mix_out(x, outs[0], outs[1], outs[2], lses[0], lses[1], lses[2], yr, yg,
                     mix_scale[l][None, :], w_out[l].astype(BF16), ln1_g[l][None, :], ln1_b[l][None, :], ones_a)

        wu = w_up[l].astype(BF16)
        wup_c = jnp.concatenate(
            [wu[:, :D_FF].reshape(D_MODEL, N_FF_CHUNKS, FF_CHUNK), wu[:, D_FF:].reshape(D_MODEL, N_FF_CHUNKS, FF_CHUNK)],
            axis=2).transpose(1, 0, 2)
        taps = jnp.concatenate([conv_w[l], conv_b[l][None, :], jnp.zeros((4, 2 * D_FF), F32)], axis=0)
        cw_c = jnp.concatenate(
            [taps[:, :D_FF].reshape(8, N_FF_CHUNKS, FF_CHUNK), taps[:, D_FF:].reshape(8, N_FF_CHUNKS, FF_CHUNK)],
            axis=2).transpose(1, 0, 2)
        wdn_c = w_down[l].astype(BF16).reshape(N_FF_CHUNKS, FF_CHUNK, D_MODEL)
        x = _ffn(x, wup_c, cw_c, wdn_c, ln2_g[l][None, :], ln2_b[l][None, :])
    return x
```

```python
import jax
import jax.numpy as jnp
import numpy as np
from jax import lax
from jax.experimental import pallas as pl
from jax.experimental.pallas import tpu as pltpu

F32 = jnp.float32
BF16 = jnp.bfloat16

D_MODEL = 1024
DEPTH = 4
HEAD_DIM_A = 64
N_HEADS_A = 8
WIDTH_A = 512
DILATIONS = (1, 4, 16)
ATT_BLK = 128
ATT_STEP_ROWS = 1024
ROPE_THETA = 10000.0
N_HEADS_R = 4
QK_DIM_R = 32
V_DIM_R = 64
RET_CHUNK = 128
N_HEADS_G = 4
QK_DIM_G = 32
V_DIM_G = 64
GLA_LOW_RANK = 16
GLA_TAU = 16.0
GLA_CHUNK = 64
CUM_BLOCK = 256
D_FF = 2816
FF_CHUNK = 256
N_FF_CHUNKS = D_FF // FF_CHUNK
DEEPNORM_ALPHA = (2 * DEPTH) ** 0.25
LOG2E = 1.4426950408889634
LN2 = 0.6931471805599453
LN_EPS = 1e-5
HEAD_NORM_EPS = 1e-6
LANES = 128
ROW_TILE = 1024
FFN_TILE = 1024
INPROJ_TILE = 1024
SUB_ROWS = 128
PART_ROWS = 256
VMEM_LIMIT = 56 * 1024 * 1024


def _iota(shape, dim):
    return lax.broadcasted_iota(jnp.int32, shape, dim)


def _dot(a, b):
    return jnp.dot(a, b, preferred_element_type=F32)


def _dot_nt(a, b):
    return lax.dot_general(a, b, (((1,), (1,)), ((), ())), preferred_element_type=F32)


def _dot_tn(a, b):
    return lax.dot_general(a, b, (((0,), (0,)), ((), ())), preferred_element_type=F32)


def _split_hi_lo(x):
    hi = x.astype(BF16)
    lo = (x - hi.astype(F32)).astype(BF16)
    return hi, lo


def _group_mean(x, ones_bd, group):
    hi, lo = _split_hi_lo(x)
    return (_dot(hi, ones_bd) + _dot(lo, ones_bd)) * (1.0 / group)


def _head_norm(x, ones_bd, group):
    mu = _group_mean(x, ones_bd, group)
    d = x - mu
    var = _group_mean(d * d, ones_bd, group)
    return d * lax.rsqrt(var + HEAD_NORM_EPS)


def _silu(g):
    return g * (1.0 / (1.0 + jnp.exp(-g)))


def _layer_norm(z, g, b):
    mu = jnp.mean(z, axis=-1, keepdims=True)
    d = z - mu
    var = jnp.mean(d * d, axis=-1, keepdims=True)
    return d * lax.rsqrt(var + LN_EPS) * g + b


def _head_stack(t, lane_head, n_heads):
    return jnp.concatenate([jnp.where(lane_head == h, t, jnp.zeros_like(t)) for h in range(n_heads)], axis=0)


def _rope(t, cos, sin_signed, half):
    lane = _iota(t.shape, 1)
    first = (lane % (2 * half)) < half
    partner = jnp.where(first, pltpu.roll(t, LANES - half, 1), pltpu.roll(t, half, 1))
    return t * cos + partner * sin_signed


def _inproj_kernel(x_ref, w_ref, walpha_ref, balpha_ref, ca_ref, sa_ref, cr_ref, sr_ref,
                   q1_ref, k1_ref, v1_ref, q4_ref, k4_ref, v4_ref, q16_ref, k16_ref, v16_ref,
                   r_ref, g_ref, la_ref, slab_ref, cls_ref):
    tm = x_ref.shape[0]
    n_slabs = WIDTH_A // LANES
    xb = x_ref[...].astype(BF16)

    def mm(c0, n):
        return _dot(xb, w_ref[:, c0:c0 + n])

    outs = ((q1_ref, q4_ref, q16_ref), (k1_ref, k4_ref, k16_ref), (v1_ref, v4_ref, v16_ref))
    base = 3 * WIDTH_A

    def finish_a(ti, t):
        for s in range(n_slabs):
            sl = slice(s * LANES, (s + 1) * LANES)
            slab = t[:, sl]
            if ti < 2:
                slab = _rope(slab, ca_ref[...], sa_ref[...], HEAD_DIM_A // 2)
            if ti == 0:
                slab = slab * (LOG2E * HEAD_DIM_A ** -0.5)
            slab_ref[ti * n_slabs + s] = slab
            outs[ti][0][:, sl] = slab.astype(BF16)
        d4, d16 = DILATIONS[1], DILATIONS[2]
        n4 = tm // d4
        for s in range(n_slabs):
            sl = slice(s * LANES, (s + 1) * LANES)
            for r4 in range(d4):
                cls4 = slab_ref[ti * n_slabs + s, pl.ds(r4, n4, stride=d4), :]
                outs[ti][1][r4, :, sl] = cls4.astype(BF16)
                cls_ref[s, r4 * n4:(r4 + 1) * n4, :] = cls4
            for r4 in range(d4):
                for c in range(d16 // d4):
                    cls16 = cls_ref[s, pl.ds(r4 * n4 + c, tm // d16, stride=d16 // d4), :]
                    outs[ti][2][r4 + d4 * c, :, sl] = cls16.astype(BF16)

    def finish_r(rr):
        cr = cr_ref[...]
        sr = sr_ref[...]
        r_ref[:, 0:128] = _rope(rr[:, 0:128], cr, sr, QK_DIM_R // 2).astype(BF16)
        r_ref[:, 128:256] = (_rope(rr[:, 128:256], cr, sr, QK_DIM_R // 2) * (QK_DIM_R ** -0.5)).astype(BF16)
        r_ref[:, 256:768] = rr[:, 256:768].astype(BF16)

    def finish_g(gg):
        g_ref[:, 0:128] = (gg[:, 0:128] * (QK_DIM_G ** -0.5)).astype(BF16)
        g_ref[:, 128:768] = gg[:, 128:768].astype(BF16)
        ag = gg[:, 768:896].astype(BF16)
        z = _dot(ag, walpha_ref[...]) + balpha_ref[...]
        log_sig = jnp.minimum(z, 0.0) - jnp.log(1.0 + jnp.exp(-jnp.abs(z)))
        la_ref[...] = log_sig * (1.0 / GLA_TAU)

    stages = [
        (lambda: mm(0, WIDTH_A), lambda t: finish_a(0, t)),
        (lambda: mm(WIDTH_A, WIDTH_A), lambda t: finish_a(1, t)),
        (lambda: mm(2 * WIDTH_A, WIDTH_A), lambda t: finish_a(2, t)),
        (lambda: mm(base, 768), finish_r),
        (lambda: mm(base + 768, 896), finish_g),
    ]
    prev = None
    for matmul, finish in stages:
        t = matmul()
        if prev is not None:
            prev[0](prev[1])
        prev = (finish, t)
    prev[0](prev[1])


def _inproj(x3, w_all, w_alpha, b_alpha, ca, sa, cr, sr):
    b, seq, _ = x3.shape
    tm = INPROJ_TILE
    row = lambda n, i: (n, i, 0)
    cls = lambda n, i: (n, 0, i, 0)
    fixed = lambda n, i: (0, 0)
    tab = lambda n, i: (i, 0)
    d4, d16 = DILATIONS[1], DILATIONS[2]
    nat = jax.ShapeDtypeStruct((b, seq, WIDTH_A), BF16)
    c4 = jax.ShapeDtypeStruct((b, d4, seq // d4, WIDTH_A), BF16)
    c16 = jax.ShapeDtypeStruct((b, d16, seq // d16, WIDTH_A), BF16)
    nat_spec = pl.BlockSpec((None, tm, WIDTH_A), row)
    c4_spec = pl.BlockSpec((None, d4, tm // d4, WIDTH_A), cls)
    c16_spec = pl.BlockSpec((None, d16, tm // d16, WIDTH_A), cls)
    return pl.pallas_call(
        _inproj_kernel,
        grid=(b, seq // tm),
        in_specs=[
            pl.BlockSpec((None, tm, D_MODEL), row),
            pl.BlockSpec(w_all.shape, fixed, pipeline_mode=pl.Buffered(1)),
            pl.BlockSpec(w_alpha.shape, fixed),
            pl.BlockSpec(b_alpha.shape, fixed),
            pl.BlockSpec((tm, LANES), tab),
            pl.BlockSpec((tm, LANES), tab),
            pl.BlockSpec((tm, LANES), tab),
            pl.BlockSpec((tm, LANES), tab),
        ],
        out_specs=[nat_spec] * 3 + [c4_spec] * 3 + [c16_spec] * 3 + [
            pl.BlockSpec((None, tm, 768), row),
            pl.BlockSpec((None, tm, 768), row),
            pl.BlockSpec((None, tm, LANES), row),
        ],
        out_shape=[nat] * 3 + [c4] * 3 + [c16] * 3 + [
            jax.ShapeDtypeStruct((b, seq, 768), BF16),
            jax.ShapeDtypeStruct((b, seq, 768), BF16),
            jax.ShapeDtypeStruct((b, seq, LANES), F32),
        ],
        scratch_shapes=[
            pltpu.VMEM((3 * (WIDTH_A // LANES), tm, LANES), F32),
            pltpu.VMEM((WIDTH_A // LANES, tm, LANES), F32),
        ],
        compiler_params=pltpu.CompilerParams(
            dimension_semantics=("arbitrary", "arbitrary"), vmem_limit_bytes=VMEM_LIMIT),
        name="inproj",
    )(x3, w_all, w_alpha, b_alpha, ca, sa, cr, sr)


def _attn_kernel(q_ref, k_ref, v_ref, o_ref, lse_ref):
    blk = ATT_BLK
    n_cls, tq, _ = q_ref.shape
    nsub = tq // blk
    n_slabs = WIDTH_A // LANES
    i = pl.program_id(1)

    qi = _iota((blk, 2 * blk), 0)
    kj = _iota((blk, 2 * blk), 1)
    band = (kj >= qi) & (kj <= qi + blk)
    kj0 = kj + jnp.where(i == 0, blk, 0)
    first = (kj0 >= qi) & (kj0 <= qi + blk)
    lane = _iota((blk, LANES), 1)
    low_q = lane < HEAD_DIM_A
    low_v = _iota((2 * blk, LANES), 1) < HEAD_DIM_A

    units = [(g, jb, s, hh) for g in range(n_cls) for jb in range(nsub) for s in range(n_slabs) for hh in range(2)]

    def key_rows(jb):
        if jb == 0:
            start = jnp.maximum(i * tq - blk, 0)
        else:
            start = i * tq + (jb - 1) * blk
        return pl.ds(pl.multiple_of(start, blk), 2 * blk)

    def scores(g, jb, s, hh):
        sl = slice(s * LANES, (s + 1) * LANES)
        qs = q_ref[g, jb * blk:(jb + 1) * blk, sl]
        sel_q = low_q if hh == 0 else jnp.logical_not(low_q)
        qm = jnp.where(sel_q, qs, jnp.zeros_like(qs))
        return _dot_nt(qm, k_ref[g, key_rows(jb), sl])

    def weighted(g, jb, s, hh, sc):
        sl = slice(s * LANES, (s + 1) * LANES)
        sc = jnp.where(first if jb == 0 else band, sc, -jnp.inf)
        m = jnp.max(sc, axis=-1, keepdims=True)
        p = jnp.exp2(sc - m).astype(BF16)
        vs = v_ref[g, key_rows(jb), sl]
        sel_v = low_v if hh == 0 else jnp.logical_not(low_v)
        vm = jnp.where(sel_v, vs, jnp.ones_like(vs))
        r = _dot(p, vm)
        return r, m

    skew = 5
    pending = {}
    done = {}
    den_parts = {}
    max_parts = {}
    grp = (lane % HEAD_DIM_A) // (HEAD_DIM_A // n_slabs)

    def gather_heads(parts):
        tile = parts[n_slabs - 1]
        for s2 in range(n_slabs - 1):
            tile = jnp.where(grp == s2, parts[s2], tile)
        return tile

    for t in range(len(units) + skew):
        if t < len(units):
            pending[t] = scores(*units[t])
        if t >= skew:
            g, jb, s, hh = units[t - skew]
            done[hh] = weighted(g, jb, s, hh, pending.pop(t - skew))
            if hh == 1:
                (r0, m0), (r1, m1) = done[0], done[1]
                num = jnp.where(low_q, r0, r1)
                den_parts[s] = jnp.where(low_q, r1, r0)
                max_parts[s] = jnp.where(low_q, m1, m0)
                den = pltpu.roll(den_parts[s], HEAD_DIM_A, 1)
                o_ref[g, jb * blk:(jb + 1) * blk, s * LANES:(s + 1) * LANES] = (num / den).astype(BF16)
                if s == n_slabs - 1:
                    lse2 = gather_heads(max_parts) + jnp.log2(gather_heads(den_parts))
                    lse_ref[g, jb * blk:(jb + 1) * blk, :] = lse2 * LN2


def _attention(q, k, v):
    nb, length, _ = q.shape
    blk = min(ATT_STEP_ROWS, length)
    n_cls = ATT_STEP_ROWS // blk
    return pl.pallas_call(
        _attn_kernel,
        grid=(nb // n_cls, length // blk),
        in_specs=[
            pl.BlockSpec((n_cls, blk, WIDTH_A), lambda n, i: (n, i, 0)),
            pl.BlockSpec((n_cls, length, WIDTH_A), lambda n, i: (n, 0, 0)),
            pl.BlockSpec((n_cls, length, WIDTH_A), lambda n, i: (n, 0, 0)),
        ],
        out_specs=[
            pl.BlockSpec((n_cls, blk, WIDTH_A), lambda n, i: (n, i, 0)),
            pl.BlockSpec((n_cls, blk, LANES), lambda n, i: (n, i, 0)),
        ],
        out_shape=[
            jax.ShapeDtypeStruct((nb, length, WIDTH_A), BF16),
            jax.ShapeDtypeStruct((nb, length, LANES), F32),
        ],
        compiler_params=pltpu.CompilerParams(
            dimension_semantics=("arbitrary", "arbitrary"), vmem_limit_bytes=VMEM_LIMIT),
        name="dilated_attn",
    )(q, k, v)


def _retention_phases(r_ref, qdec_ref, kdec_ref, dmask_ref, cdec_ref, bd_ref, ones_ref, y_ref,
                      state_ref, qd_ref, o_ref, kv_ref, stb_ref):
    c = RET_CHUNK
    n_chunks = r_ref.shape[0] // c

    @pl.when(pl.program_id(1) == 0)
    def _():
        state_ref[...] = jnp.zeros_like(state_ref)

    lane_q = _iota((c, 128), 1) // QK_DIM_R
    lane_v = _iota((c, 256), 1) // V_DIM_R
    chunk_rows = [slice(ci * c, (ci + 1) * c) for ci in range(n_chunks)]
    sc = {}

    def scores():
        bd = bd_ref[...] > 0
        for ci, rows in enumerate(chunk_rows):
            q = r_ref[rows, 0:128]
            k = r_ref[rows, 128:256]
            sc[ci] = _dot_nt(q, _head_stack(k, lane_q, N_HEADS_R))
            qd_ref[rows, :] = (q.astype(F32) * qdec_ref[...]).astype(BF16)
            kd = (k.astype(F32) * kdec_ref[...]).astype(BF16)
            kv_ref[ci] = jnp.where(bd, _dot_tn(kd, r_ref[rows, 256:512]), 0.0)

    def values():
        for ci, rows in enumerate(chunk_rows):
            p = (sc[ci] * dmask_ref[...]).astype(BF16)
            o_ref[rows, :] = _dot(p, _head_stack(r_ref[rows, 256:512], lane_v, N_HEADS_R))

    def scan():
        st = state_ref[...]
        for ci in range(n_chunks):
            stb_ref[ci] = st.astype(BF16)
            st = cdec_ref[...] * st + kv_ref[ci]
        state_ref[...] = st

    def inter():
        for ci, rows in enumerate(chunk_rows):
            o_ref[rows, :] += _dot(qd_ref[rows, :], stb_ref[ci])

    norm = {}

    def center():
        o = o_ref[...]
        norm["dev"] = o - _group_mean(o, ones_ref[...], V_DIM_R)

    def spread():
        norm["var"] = _group_mean(norm["dev"] * norm["dev"], ones_ref[...], V_DIM_R)

    def finish():
        y = norm["dev"] * lax.rsqrt(norm["var"] + HEAD_NORM_EPS) * _silu(r_ref[:, 512:768].astype(F32))
        y_ref[...] = y.astype(BF16)

    return [scores, values, scan, inter, center, spread, finish]


def _gla_phases(g_ref, la_ref, tri_ref, bdt_ref, ones_ref, y_ref,
                state_ref, cum_ref, qt_ref, o_ref, kv_ref, stb_ref):
    c = GLA_CHUNK
    tm = g_ref.shape[0]
    n_chunks = tm // c

    @pl.when(pl.program_id(1) == 0)
    def _():
        state_ref[...] = jnp.zeros_like(state_ref)

    lane_q = _iota((c, 128), 1) // QK_DIM_G
    lane_v = _iota((c, 256), 1) // V_DIM_G
    chunk_rows = [slice(ci * c, (ci + 1) * c) for ci in range(n_chunks)]
    sc = {}
    decays = {}

    def cumsum():
        for part in range(tm // CUM_BLOCK):
            rows = slice(part * CUM_BLOCK, (part + 1) * CUM_BLOCK)
            la_hi, la_lo = _split_hi_lo(la_ref[rows, :])
            cum_ref[rows, :] = _dot(tri_ref[...], la_hi) + _dot(tri_ref[...], la_lo)

    def scores():
        bdt = bdt_ref[...] > 0
        for ci, rows in enumerate(chunk_rows):
            q = g_ref[rows, 0:128].astype(F32)
            k = g_ref[rows, 128:256].astype(F32)
            cum = cum_ref[rows, :]
            last = cum[c - 1:c, :]
            q_t = (q * jnp.exp(cum)).astype(BF16)
            k_t = (k * jnp.exp(-cum)).astype(BF16)
            k_l = (k * jnp.exp(last - cum)).astype(BF16)
            qt_ref[rows, :] = q_t
            sc[ci] = _dot_nt(q_t, _head_stack(k_t, lane_q, N_HEADS_G))
            kv_ref[ci] = jnp.where(bdt, _dot_tn(g_ref[rows, 256:512], k_l), 0.0)
            decays[ci] = jnp.exp(last)

    def values():
        causal = _iota((c, N_HEADS_G * c), 0) >= (_iota((c, N_HEADS_G * c), 1) % c)
        for ci, rows in enumerate(chunk_rows):
            att = jnp.where(causal, sc[ci], 0.0).astype(BF16)
            o_ref[rows, :] = _dot(att, _head_stack(g_ref[rows, 256:512], lane_v, N_HEADS_G))

    def scan():
        st = state_ref[...]
        for ci in range(n_chunks):
            stb_ref[ci] = st.astype(BF16)
            st = decays[ci] * st + kv_ref[ci]
        state_ref[...] = st

    def inter():
        for ci, rows in enumerate(chunk_rows):
            o_ref[rows, :] += _dot_nt(qt_ref[rows, :], stb_ref[ci])

    norm = {}

    def center():
        o = o_ref[...]
        norm["dev"] = o - _group_mean(o, ones_ref[...], V_DIM_G)

    def spread():
        norm["var"] = _group_mean(norm["dev"] * norm["dev"], ones_ref[...], V_DIM_G)

    def finish():
        y = norm["dev"] * lax.rsqrt(norm["var"] + HEAD_NORM_EPS) * _silu(g_ref[:, 512:768].astype(F32))
        y_ref[...] = y.astype(BF16)

    return [cumsum, scores, values, scan, inter, center, spread, finish]


N_RET_SCRATCH = 5


def _linear_mixers_kernel(r_ref, qdec_ref, kdec_ref, dmask_ref, cdec_ref, bd_ref, ones_ref,
                          g_ref, la_ref, tri_ref, bdt_ref, yr_ref, yg_ref, *scratch):
    ret = _retention_phases(r_ref, qdec_ref, kdec_ref, dmask_ref, cdec_ref, bd_ref, ones_ref, yr_ref,
                            *scratch[:N_RET_SCRATCH])
    gla = _gla_phases(g_ref, la_ref, tri_ref, bdt_ref, ones_ref, yg_ref, *scratch[N_RET_SCRATCH:])
    gla.pop(0)()
    for ret_phase, gla_phase in zip(ret, gla):
        ret_phase()
        gla_phase()


def _linear_mixers(r3, g3, la3, ret_tabs, gla_tabs):
    b, seq, _ = r3.shape
    tm = ROW_TILE
    qdec, kdec, dmask, cdec, bd, ones = ret_tabs
    tri, bdt, _ = gla_tabs
    fixed2 = lambda n, i: (0, 0)
    row = lambda n, i: (n, i, 0)
    y_shape = jax.ShapeDtypeStruct((b, seq, 256), BF16)
    return pl.pallas_call(
        _linear_mixers_kernel,
        grid=(b, seq // tm),
        in_specs=[
            pl.BlockSpec((None, tm, 768), row),
            pl.BlockSpec(qdec.shape, fixed2),
            pl.BlockSpec(kdec.shape, fixed2),
            pl.BlockSpec(dmask.shape, fixed2),
            pl.BlockSpec(cdec.shape, fixed2),
            pl.BlockSpec(bd.shape, fixed2),
            pl.BlockSpec(ones.shape, fixed2),
            pl.BlockSpec((None, tm, 768), row),
            pl.BlockSpec((None, tm, LANES), row),
            pl.BlockSpec(tri.shape, fixed2),
            pl.BlockSpec(bdt.shape, fixed2),
        ],
        out_specs=[pl.BlockSpec((None, tm, 256), row), pl.BlockSpec((None, tm, 256), row)],
        out_shape=[y_shape, y_shape],
        scratch_shapes=[
            pltpu.VMEM((128, 256), F32),
            pltpu.VMEM((tm, 128), BF16),
            pltpu.VMEM((tm, 256), F32),
            pltpu.VMEM((tm // RET_CHUNK, 128, 256), F32),
            pltpu.VMEM((tm // RET_CHUNK, 128, 256), BF16),
            pltpu.VMEM((256, 128), F32),
            pltpu.VMEM((tm, 128), F32),
            pltpu.VMEM((tm, 128), BF16),
            pltpu.VMEM((tm, 256), F32),
            pltpu.VMEM((tm // GLA_CHUNK, 256, 128), F32),
            pltpu.VMEM((tm // GLA_CHUNK, 256, 128), BF16),
        ],
        compiler_params=pltpu.CompilerParams(
            dimension_semantics=("arbitrary", "arbitrary"), vmem_limit_bytes=VMEM_LIMIT),
        name="linear_mixers",
    )(r3, qdec, kdec, dmask, cdec, bd, ones, g3, la3, tri, bdt)


def _mix_out_kernel(x_ref, o1_ref, o4_ref, o16_ref, l1_ref, l4_ref, l16_ref, yr_ref, yg_ref,
                    ms_ref, wout_ref, lng_ref, lnb_ref, ones_ref, out_ref, on_ref, ln_ref, y_ref):
    tm = x_ref.shape[0]
    n_slabs = WIDTH_A // LANES

    def to_sequence_order(part):
        for bi, (o_ref, l_ref, d) in enumerate(((o4_ref, l4_ref, DILATIONS[1]), (o16_ref, l16_ref, DILATIONS[2]))):
            n = PART_ROWS // d
            src = slice(part * n, (part + 1) * n)
            for r in range(d):
                dst = pl.ds(part * PART_ROWS + r, n, stride=d)
                ln_ref[bi, dst, :] = l_ref[r, src, :]
                for s in range(n_slabs):
                    on_ref[bi, s, dst, :] = o_ref[r, src, s * LANES:(s + 1) * LANES].astype(F32)

    low = _iota((SUB_ROWS, LANES), 1) < HEAD_DIM_A
    ms = ms_ref[...]

    def merge(rb):
        rows = slice(rb * SUB_ROWS, (rb + 1) * SUB_ROWS)
        l1 = l1_ref[rows, :]
        l4 = ln_ref[0, rows, :]
        l16 = ln_ref[1, rows, :]
        top = jnp.maximum(jnp.maximum(l1, l4), l16)
        e1 = jnp.exp(l1 - top)
        e4 = jnp.exp(l4 - top)
        e16 = jnp.exp(l16 - top)
        inv = 1.0 / (e1 + e4 + e16)
        w4 = e4 * inv
        w16 = e16 * inv
        for s in range(n_slabs):
            sl = slice(s * LANES, (s + 1) * LANES)

            def expand(w):
                c0 = HEAD_DIM_A + 16 * s
                c1 = 16 * s
                return jnp.where(low, w[:, c0:c0 + 1], w[:, c1:c1 + 1])

            x4 = expand(w4)
            x16 = expand(w16)
            merged = ((1.0 - x4 - x16) * o1_ref[rows, sl].astype(F32) + x4 * on_ref[0, s, rows, :]
                      + x16 * on_ref[1, s, rows, :])
            ya = _head_norm(merged, ones_ref[...], HEAD_DIM_A)
            y_ref[rows, sl] = (ya * ms[:, sl]).astype(BF16)
        y_ref[rows, 512:768] = (yr_ref[rows, :].astype(F32) * ms[:, 512:768]).astype(BF16)
        y_ref[rows, 768:1024] = (yg_ref[rows, :].astype(F32) * ms[:, 768:1024]).astype(BF16)

    def project(part):
        rows = slice(part * PART_ROWS, (part + 1) * PART_ROWS)
        return _dot(y_ref[rows, :], wout_ref[...])

    def finish(part, proj):
        rows = slice(part * PART_ROWS, (part + 1) * PART_ROWS)
        z = DEEPNORM_ALPHA * x_ref[rows, :] + proj
        out_ref[rows, :] = _layer_norm(z, lng_ref[...], lnb_ref[...])

    n_parts = tm // PART_ROWS
    proj = None
    for part in range(n_parts):
        to_sequence_order(part)
        for rb in range(part * (PART_ROWS // SUB_ROWS), (part + 1) * (PART_ROWS // SUB_ROWS)):
            merge(rb)
        if proj is not None:
            finish(part - 1, proj)
        proj = project(part)
    finish(n_parts - 1, proj)


def _mix_out(x3, o1, o4, o16, l1, l4, l16, yr, yg, ms, w_out, ln_g, ln_b, ones):
    b, seq, _ = x3.shape
    tm = ROW_TILE
    row = lambda n, i: (n, i, 0)
    cls = lambda n, i: (n, 0, i, 0)
    fixed = lambda n, i: (0, 0)
    d4, d16 = DILATIONS[1], DILATIONS[2]
    return pl.pallas_call(
        _mix_out_kernel,
        grid=(b, seq // tm),
        in_specs=[
            pl.BlockSpec((None, tm, D_MODEL), row),
            pl.BlockSpec((None, tm, WIDTH_A), row),
            pl.BlockSpec((None, d4, tm // d4, WIDTH_A), cls),
            pl.BlockSpec((None, d16, tm // d16, WIDTH_A), cls),
            pl.BlockSpec((None, tm, LANES), row),
            pl.BlockSpec((None, d4, tm // d4, LANES), cls),
            pl.BlockSpec((None, d16, tm // d16, LANES), cls),
            pl.BlockSpec((None, tm, 256), row),
            pl.BlockSpec((None, tm, 256), row),
            pl.BlockSpec(ms.shape, fixed),
            pl.BlockSpec(w_out.shape, fixed),
            pl.BlockSpec(ln_g.shape, fixed),
            pl.BlockSpec(ln_b.shape, fixed),
            pl.BlockSpec(ones.shape, fixed),
        ],
        out_specs=pl.BlockSpec((None, tm, D_MODEL), row),
        out_shape=jax.ShapeDtypeStruct((b, seq, D_MODEL), F32),
        scratch_shapes=[
            pltpu.VMEM((2, WIDTH_A // LANES, tm, LANES), F32),
            pltpu.VMEM((2, tm, LANES), F32),
            pltpu.VMEM((tm, D_MODEL), BF16),
        ],
        compiler_params=pltpu.CompilerParams(
            dimension_semantics=("arbitrary", "arbitrary"), vmem_limit_bytes=VMEM_LIMIT),
        name="mix_out",
    )(x3, o1, o4, o16, l1, l4, l16, yr, yg, ms, w_out, ln_g, ln_b, ones)


def _ffn_kernel(x_ref, wup_ref, cw_ref, wdn_ref, lng_ref, lnb_ref, out_ref, u_ref, carry_ref, acc_ref, xb_ref):
    assert N_FF_CHUNKS % 2 == 1 and N_FF_CHUNKS >= 3
    tm = x_ref.shape[0]
    halo = 8

    @pl.when(pl.program_id(1) == 0)
    def _():
        carry_ref[...] = jnp.zeros_like(carry_ref)

    xb_ref[...] = x_ref[...].astype(BF16)

    def produce(j, buf):
        u_ref[buf, 0:halo, :] = carry_ref[j]
        u_ref[buf, halo:halo + tm, :] = _dot(xb_ref[...], wup_ref[j])
        carry_ref[j] = u_ref[buf, tm:tm + halo, :]

    def consume(j, buf, first=False):
        cw = cw_ref[j]
        y = (u_ref[buf, halo:halo + tm, :] * cw[2:3, :] + u_ref[buf, halo - 1:halo - 1 + tm, :] * cw[1:2, :]
             + u_ref[buf, halo - 2:halo - 2 + tm, :] * cw[0:1, :] + cw[3:4, :])
        h = _silu(y[:, :FF_CHUNK]) * y[:, FF_CHUNK:]
        d = _dot(h.astype(BF16), wdn_ref[j])
        if first:
            acc_ref[...] = d
        else:
            acc_ref[...] += d

    produce(0, 0)
    produce(1, 1)
    consume(0, 0, first=True)

    def pair(p, _):
        j = 2 * p + 1
        produce(j + 1, 0)
        consume(j, 1)
        produce(j + 2, 1)
        consume(j + 1, 0)
        return 0

    lax.fori_loop(0, (N_FF_CHUNKS - 3) // 2, pair, 0)
    produce(N_FF_CHUNKS - 1, 0)
    consume(N_FF_CHUNKS - 2, 1)
    consume(N_FF_CHUNKS - 1, 0)
    z = DEEPNORM_ALPHA * x_ref[...] + acc_ref[...]
    out_ref[...] = _layer_norm(z, lng_ref[...], lnb_ref[...])


def _ffn(x3, wup_c, cw_c, wdn_c, ln_g, ln_b):
    b, seq, _ = x3.shape
    tm = FFN_TILE
    fixed3 = lambda n, i: (0, 0, 0)
    fixed2 = lambda n, i: (0, 0)
    return pl.pallas_call(
        _ffn_kernel,
        grid=(b, seq // tm),
        in_specs=[
            pl.BlockSpec((None, tm, D_MODEL), lambda n, i: (n, i, 0)),
            pl.BlockSpec(wup_c.shape, fixed3, pipeline_mode=pl.Buffered(1)),
            pl.BlockSpec(cw_c.shape, fixed3),
            pl.BlockSpec(wdn_c.shape, fixed3, pipeline_mode=pl.Buffered(1)),
            pl.BlockSpec(ln_g.shape, fixed2),
            pl.BlockSpec(ln_b.shape, fixed2),
        ],
        out_specs=pl.BlockSpec((None, tm, D_MODEL), lambda n, i: (n, i, 0)),
        out_shape=jax.ShapeDtypeStruct((b, seq, D_MODEL), F32),
        scratch_shapes=[
            pltpu.VMEM((2, tm + 8, 2 * FF_CHUNK), F32),
            pltpu.VMEM((N_FF_CHUNKS, 8, 2 * FF_CHUNK), F32),
            pltpu.VMEM((tm, D_MODEL), F32),
            pltpu.VMEM((tm, D_MODEL), BF16),
        ],
        compiler_params=pltpu.CompilerParams(
            dimension_semantics=("arbitrary", "arbitrary"), vmem_limit_bytes=VMEM_LIMIT),
        name="conv_glu_ffn",
    )(x3, wup_c, cw_c, wdn_c, ln_g, ln_b)


def _rope_tables(seq, dim):
    inv = 1.0 / (ROPE_THETA ** (jnp.arange(0, dim, 2, dtype=F32) / dim))
    ang = jnp.arange(seq, dtype=F32)[:, None] * inv[None, :]
    cos, sin = jnp.cos(ang), jnp.sin(ang)
    reps = LANES // dim
    cos_l = jnp.tile(jnp.concatenate([cos, cos], axis=1), (1, reps))
    sin_l = jnp.tile(jnp.concatenate([-sin, sin], axis=1), (1, reps))
    return cos_l, sin_l


def _block_diag_ones(n, group):
    idx = np.arange(n) // group
    return jnp.asarray(idx[:, None] == idx[None, :], BF16)


def _retention_tables():
    c = RET_CHUNK
    h = N_HEADS_R
    lg = jnp.log(1.0 - jnp.power(2.0, -5.0 - jnp.arange(h, dtype=F32)))
    idx = jnp.arange(c, dtype=F32)
    dist = idx[:, None] - idx[None, :]
    dmask = jnp.where(dist >= 0, jnp.exp(lg[:, None, None] * jnp.maximum(dist, 0.0)), 0.0)
    dmask_l = jnp.concatenate([dmask[i] for i in range(h)], axis=1)
    q_dec = jnp.exp(lg[:, None] * (idx + 1.0))
    k_dec = jnp.exp(lg[:, None] * (c - 1.0 - idx))
    chunk_dec = jnp.exp(lg * c)
    qdec_l = jnp.repeat(q_dec.T, QK_DIM_R, axis=1)
    kdec_l = jnp.repeat(k_dec.T, QK_DIM_R, axis=1)
    row_h = np.arange(128) // QK_DIM_R
    col_h = np.arange(256) // V_DIM_R
    bd = jnp.asarray(row_h[:, None] == col_h[None, :], F32)
    cdec = bd * chunk_dec[row_h][:, None]
    return qdec_l, kdec_l, dmask_l, cdec, bd, _block_diag_ones(256, V_DIM_R)


def _gla_tables():
    r = np.arange(CUM_BLOCK)
    tri = jnp.asarray((r[:, None] >= r[None, :]) & (r[:, None] // GLA_CHUNK == r[None, :] // GLA_CHUNK), BF16)
    row_h = np.arange(256) // V_DIM_G
    col_h = np.arange(128) // QK_DIM_G
    bdt = jnp.asarray(row_h[:, None] == col_h[None, :], F32)
    return tri, bdt, _block_diag_ones(256, V_DIM_G)


def kernel(x, w_in, w_alpha, b_alpha, mix_scale, w_out, ln1_g, ln1_b, w_up, conv_w, conv_b, w_down, ln2_g, ln2_b):
    b, seq, d_model = x.shape
    assert (d_model, w_in.shape[0]) == (D_MODEL, DEPTH)
    assert seq % max(ROW_TILE, FFN_TILE, INPROJ_TILE) == 0 and (seq // DILATIONS[2]) % ATT_BLK == 0
    ca, sa = _rope_tables(seq, HEAD_DIM_A)
    cr, sr = _rope_tables(seq, QK_DIM_R)
    ret_tabs = _retention_tables()
    gla_tabs = _gla_tables()
    ones_a = _block_diag_ones(LANES, HEAD_DIM_A)
    pad_cols = 3 * WIDTH_A + 768 + 768 + LANES - w_in.shape[2]

    for l in range(DEPTH):
        w_all = jnp.pad(w_in[l], ((0, 0), (0, pad_cols))).astype(BF16)
        w_al = jnp.pad(w_alpha[l], ((0, LANES - GLA_LOW_RANK), (0, 0))).astype(BF16)
        (q1, k1, v1, q4, k4, v4, q16, k16, v16, r3, g3, la3) = _inproj(
            x, w_all, w_al, b_alpha[l][None, :], ca, sa, cr, sr)

        outs, lses = [], []
        for d, (q, k, v) in zip(DILATIONS, ((q1, k1, v1), (q4, k4, v4), (q16, k16, v16))):
            shp = (b * d, seq // d, WIDTH_A)
            o, lse = _attention(q.reshape(shp), k.reshape(shp), v.reshape(shp))
            if d > 1:
                o = o.reshape(b, d, seq // d, WIDTH_A)
                lse = lse.reshape(b, d, seq // d, LANES)
            outs.append(o)
            lses.append(lse)

        yr, yg = _linear_mixers(r3, g3, la3, ret_tabs, gla_tabs)

        x = _mix_out(x, outs[0], outs[1], outs[2], lses[0], lses[1], lses[2], yr, yg,
                     mix_scale[l][None, :], w_out[l].astype(BF16), ln1_g[l][None, :], ln1_b[l][None, :], ones_a)

        wu = w_up[l].astype(BF16)
        wup_c = jnp.concatenate(
            [wu[:, :D_FF].reshape(D_MODEL, N_FF_CHUNKS, FF_CHUNK), wu[:, D_FF:].reshape(D_MODEL, N_FF_CHUNKS, FF_CHUNK)],
            axis=2).transpose(1, 0, 2)
        taps = jnp.concatenate([conv_w[l], conv_b[l][None, :], jnp.zeros((4, 2 * D_FF), F32)], axis=0)
        cw_c = jnp.concatenate(
            [taps[:, :D_FF].reshape(8, N_FF_CHUNKS, FF_CHUNK), taps[:, D_FF:].reshape(8, N_FF_CHUNKS, FF_CHUNK)],
            axis=2).transpose(1, 0, 2)
        wdn_c = w_down[l].astype(BF16).reshape(N_FF_CHUNKS, FF_CHUNK, D_MODEL)
        x = _ffn(x, wup_c, cw_c, wdn_c, ln2_g[l][None, :], ln2_b[l][None, :])
    return x
```

```python
import jax
import jax.numpy as jnp
import numpy as np
from jax import lax
from jax.experimental import pallas as pl
from jax.experimental.pallas import tpu as pltpu

F32 = jnp.float32
BF16 = jnp.bfloat16

D_MODEL = 1024
DEPTH = 4
HEAD_DIM_A = 64
N_HEADS_A = 8
WIDTH_A = 512
DILATIONS = (1, 4, 16)
ATT_BLK = 128
ATT_STEP_ROWS = 1024
ROPE_THETA = 10000.0
N_HEADS_R = 4
QK_DIM_R = 32
V_DIM_R = 64
RET_CHUNK = 128
N_HEADS_G = 4
QK_DIM_G = 32
V_DIM_G = 64
GLA_LOW_RANK = 16
GLA_TAU = 16.0
GLA_CHUNK = 64
CUM_BLOCK = 256
D_FF = 2816
FF_CHUNK = 256
N_FF_CHUNKS = D_FF // FF_CHUNK
DEEPNORM_ALPHA = (2 * DEPTH) ** 0.25
LOG2E = 1.4426950408889634
LN2 = 0.6931471805599453
LN_EPS = 1e-5
HEAD_NORM_EPS = 1e-6
LANES = 128
ROW_TILE = 1024
FFN_TILE = 1024
INPROJ_TILE = 1024
SUB_ROWS = 128
PART_ROWS = 256
VMEM_LIMIT = 56 * 1024 * 1024


def _iota(shape, dim):
    return lax.broadcasted_iota(jnp.int32, shape, dim)


def _dot(a, b):
    return jnp.dot(a, b, preferred_element_type=F32)


def _dot_nt(a, b):
    return lax.dot_general(a, b, (((1,), (1,)), ((), ())), preferred_element_type=F32)


def _dot_tn(a, b):
    return lax.dot_general(a, b, (((0,), (0,)), ((), ())), preferred_element_type=F32)


def _split_hi_lo(x):
    hi = x.astype(BF16)
    lo = (x - hi.astype(F32)).astype(BF16)
    return hi, lo


def _group_mean(x, ones_bd, group):
    hi, lo = _split_hi_lo(x)
    return (_dot(hi, ones_bd) + _dot(lo, ones_bd)) * (1.0 / group)


def _head_norm(x, ones_bd, group):
    mu = _group_mean(x, ones_bd, group)
    d = x - mu
    var = _group_mean(d * d, ones_bd, group)
    return d * lax.rsqrt(var + HEAD_NORM_EPS)


def _silu(g):
    return g * (1.0 / (1.0 + jnp.exp(-g)))


def _layer_norm(z, g, b):
    mu = jnp.mean(z, axis=-1, keepdims=True)
    d = z - mu
    var = jnp.mean(d * d, axis=-1, keepdims=True)
    return d * lax.rsqrt(var + LN_EPS) * g + b


def _head_stack(t, lane_head, n_heads):
    return jnp.concatenate([jnp.where(lane_head == h, t, jnp.zeros_like(t)) for h in range(n_heads)], axis=0)


def _rope(t, cos, sin_signed, half):
    lane = _iota(t.shape, 1)
    first = (lane % (2 * half)) < half
    partner = jnp.where(first, pltpu.roll(t, LANES - half, 1), pltpu.roll(t, half, 1))
    return t * cos + partner * sin_signed


def _inproj_kernel(x_ref, w_ref, walpha_ref, balpha_ref, ca_ref, sa_ref, cr_ref, sr_ref,
                   q1_ref, k1_ref, v1_ref, q4_ref, k4_ref, v4_ref, q16_ref, k16_ref, v16_ref,
                   r_ref, g_ref, la_ref, slab_ref, cls_ref):
    tm = x_ref.shape[0]
    n_slabs = WIDTH_A // LANES
    xb = x_ref[...].astype(BF16)

    def mm(c0, n):
        return _dot(xb, w_ref[:, c0:c0 + n])

    outs = ((q1_ref, q4_ref, q16_ref), (k1_ref, k4_ref, k16_ref), (v1_ref, v4_ref, v16_ref))
    base = 3 * WIDTH_A

    def finish_a(ti, t):
        for s in range(n_slabs):
            sl = slice(s * LANES, (s + 1) * LANES)
            slab = t[:, sl]
            if ti < 2:
                slab = _rope(slab, ca_ref[...], sa_ref[...], HEAD_DIM_A // 2)
            if ti == 0:
                slab = slab * (LOG2E * HEAD_DIM_A ** -0.5)
            slab_ref[ti * n_slabs + s] = slab
            outs[ti][0][:, sl] = slab.astype(BF16)
        d4, d16 = DILATIONS[1], DILATIONS[2]
        n4 = tm // d4
        for s in range(n_slabs):
            sl = slice(s * LANES, (s + 1) * LANES)
            for r4 in range(d4):
                cls4 = slab_ref[ti * n_slabs + s, pl.ds(r4, n4, stride=d4), :]
                outs[ti][1][r4, :, sl] = cls4.astype(BF16)
                cls_ref[s, r4 * n4:(r4 + 1) * n4, :] = cls4
            for r4 in range(d4):
                for c in range(d16 // d4):
                    cls16 = cls_ref[s, pl.ds(r4 * n4 + c, tm // d16, stride=d16 // d4), :]
                    outs[ti][2][r4 + d4 * c, :, sl] = cls16.astype(BF16)

    def finish_r(rr):
        cr = cr_ref[...]
        sr = sr_ref[...]
        r_ref[:, 0:128] = _rope(rr[:, 0:128], cr, sr, QK_DIM_R // 2).astype(BF16)
        r_ref[:, 128:256] = (_rope(rr[:, 128:256], cr, sr, QK_DIM_R // 2) * (QK_DIM_R ** -0.5)).astype(BF16)
        r_ref[:, 256:768] = rr[:, 256:768].astype(BF16)

    def finish_g(gg):
        g_ref[:, 0:128] = (gg[:, 0:128] * (QK_DIM_G ** -0.5)).astype(BF16)
        g_ref[:, 128:768] = gg[:, 128:768].astype(BF16)
        ag = gg[:, 768:896].astype(BF16)
        z = _dot(ag, walpha_ref[...]) + balpha_ref[...]
        log_sig = jnp.minimum(z, 0.0) - jnp.log(1.0 + jnp.exp(-jnp.abs(z)))
        la_ref[...] = log_sig * (1.0 / GLA_TAU)

    stages = [
        (lambda: mm(2 * WIDTH_A, WIDTH_A), lambda t: finish_a(2, t)),
        (lambda: mm(0, WIDTH_A), lambda t: finish_a(0, t)),
        (lambda: mm(base, 768), finish_r),
        (lambda: mm(WIDTH_A, WIDTH_A), lambda t: finish_a(1, t)),
        (lambda: mm(base + 768, 896), finish_g),
    ]
    prev = None
    for matmul, finish in stages:
        t = matmul()
        if prev is not None:
            prev[0](prev[1])
        prev = (finish, t)
    prev[0](prev[1])


def _inproj(x3, w_all, w_alpha, b_alpha, ca, sa, cr, sr):
    b, seq, _ = x3.shape
    tm = INPROJ_TILE
    row = lambda n, i: (n, i, 0)
    cls = lambda n, i: (n, 0, i, 0)
    fixed = lambda n, i: (0, 0)
    tab = lambda n, i: (i, 0)
    d4, d16 = DILATIONS[1], DILATIONS[2]
    nat = jax.ShapeDtypeStruct((b, seq, WIDTH_A), BF16)
    c4 = jax.ShapeDtypeStruct((b, d4, seq // d4, WIDTH_A), BF16)
    c16 = jax.ShapeDtypeStruct((b, d16, seq // d16, WIDTH_A), BF16)
    nat_spec = pl.BlockSpec((None, tm, WIDTH_A), row)
    c4_spec = pl.BlockSpec((None, d4, tm // d4, WIDTH_A), cls)
    c16_spec = pl.BlockSpec((None, d16, tm // d16, WIDTH_A), cls)
    return pl.pallas_call(
        _inproj_kernel,
        grid=(b, seq // tm),
        in_specs=[
            pl.BlockSpec((None, tm, D_MODEL), row),
            pl.BlockSpec(w_all.shape, fixed, pipeline_mode=pl.Buffered(1)),
            pl.BlockSpec(w_alpha.shape, fixed),
            pl.BlockSpec(b_alpha.shape, fixed),
            pl.BlockSpec((tm, LANES), tab),
            pl.BlockSpec((tm, LANES), tab),
            pl.BlockSpec((tm, LANES), tab),
            pl.BlockSpec((tm, LANES), tab),
        ],
        out_specs=[nat_spec] * 3 + [c4_spec] * 3 + [c16_spec] * 3 + [
            pl.BlockSpec((None, tm, 768), row),
            pl.BlockSpec((None, tm, 768), row),
            pl.BlockSpec((None, tm, LANES), row),
        ],
        out_shape=[nat] * 3 + [c4] * 3 + [c16] * 3 + [
            jax.ShapeDtypeStruct((b, seq, 768), BF16),
            jax.ShapeDtypeStruct((b, seq, 768), BF16),
            jax.ShapeDtypeStruct((b, seq, LANES), F32),
        ],
        scratch_shapes=[
            pltpu.VMEM((3 * (WIDTH_A // LANES), tm, LANES), F32),
            pltpu.VMEM((WIDTH_A // LANES, tm, LANES), F32),
        ],
        compiler_params=pltpu.CompilerParams(
            dimension_semantics=("arbitrary", "arbitrary"), vmem_limit_bytes=VMEM_LIMIT),
        name="inproj",
    )(x3, w_all, w_alpha, b_alpha, ca, sa, cr, sr)


def _attn_kernel(q_ref, k_ref, v_ref, o_ref, lse_ref):
    blk = ATT_BLK
    n_cls, tq, _ = q_ref.shape
    nsub = tq // blk
    n_slabs = WIDTH_A // LANES
    i = pl.program_id(1)

    qi = _iota((blk, 2 * blk), 0)
    kj = _iota((blk, 2 * blk), 1)
    band = (kj >= qi) & (kj <= qi + blk)
    kj0 = kj + jnp.where(i == 0, blk, 0)
    first = (kj0 >= qi) & (kj0 <= qi + blk)
    lane = _iota((blk, LANES), 1)
    low_q = lane < HEAD_DIM_A
    low_v = _iota((2 * blk, LANES), 1) < HEAD_DIM_A

    units = [(g, jb, s, hh) for g in range(n_cls) for jb in range(nsub) for s in range(n_slabs) for hh in range(2)]

    def key_rows(jb):
        if jb == 0:
            start = jnp.maximum(i * tq - blk, 0)
        else:
            start = i * tq + (jb - 1) * blk
        return pl.ds(pl.multiple_of(start, blk), 2 * blk)

    def scores(g, jb, s, hh):
        sl = slice(s * LANES, (s + 1) * LANES)
        qs = q_ref[g, jb * blk:(jb + 1) * blk, sl]
        sel_q = low_q if hh == 0 else jnp.logical_not(low_q)
        qm = jnp.where(sel_q, qs, jnp.zeros_like(qs))
        return _dot_nt(qm, k_ref[g, key_rows(jb), sl])

    def weighted(g, jb, s, hh, sc):
        sl = slice(s * LANES, (s + 1) * LANES)
        sc = jnp.where(first if jb == 0 else band, sc, -jnp.inf)
        m = jnp.max(sc, axis=-1, keepdims=True)
        p = jnp.exp2(sc - m).astype(BF16)
        vs = v_ref[g, key_rows(jb), sl]
        sel_v = low_v if hh == 0 else jnp.logical_not(low_v)
        vm = jnp.where(sel_v, vs, jnp.ones_like(vs))
        r = _dot(p, vm)
        return r, m

    skew = 5
    pending = {}
    done = {}
    den_parts = {}
    max_parts = {}
    grp = (lane % HEAD_DIM_A) // (HEAD_DIM_A // n_slabs)

    def gather_heads(parts):
        tile = parts[n_slabs - 1]
        for s2 in range(n_slabs - 1):
            tile = jnp.where(grp == s2, parts[s2], tile)
        return tile

    for t in range(len(units) + skew):
        if t < len(units):
            pending[t] = scores(*units[t])
        if t >= skew:
            g, jb, s, hh = units[t - skew]
            done[hh] = weighted(g, jb, s, hh, pending.pop(t - skew))
            if hh == 1:
                (r0, m0), (r1, m1) = done[0], done[1]
                num = jnp.where(low_q, r0, r1)
                den_parts[s] = jnp.where(low_q, r1, r0)
                max_parts[s] = jnp.where(low_q, m1, m0)
                den = pltpu.roll(den_parts[s], HEAD_DIM_A, 1)
                o_ref[g, jb * blk:(jb + 1) * blk, s * LANES:(s + 1) * LANES] = (num / den).astype(BF16)
                if s == n_slabs - 1:
                    lse2 = gather_heads(max_parts) + jnp.log2(gather_heads(den_parts))
                    lse_ref[g, jb * blk:(jb + 1) * blk, :] = lse2 * LN2


def _attention(q, k, v):
    nb, length, _ = q.shape
    blk = min(ATT_STEP_ROWS, length)
    n_cls = ATT_STEP_ROWS // blk
    return pl.pallas_call(
        _attn_kernel,
        grid=(nb // n_cls, length // blk),
        in_specs=[
            pl.BlockSpec((n_cls, blk, WIDTH_A), lambda n, i: (n, i, 0)),
            pl.BlockSpec((n_cls, length, WIDTH_A), lambda n, i: (n, 0, 0)),
            pl.BlockSpec((n_cls, length, WIDTH_A), lambda n, i: (n, 0, 0)),
        ],
        out_specs=[
            pl.BlockSpec((n_cls, blk, WIDTH_A), lambda n, i: (n, i, 0)),
            pl.BlockSpec((n_cls, blk, LANES), lambda n, i: (n, i, 0)),
        ],
        out_shape=[
            jax.ShapeDtypeStruct((nb, length, WIDTH_A), BF16),
            jax.ShapeDtypeStruct((nb, length, LANES), F32),
        ],
        compiler_params=pltpu.CompilerParams(
            dimension_semantics=("arbitrary", "arbitrary"), vmem_limit_bytes=VMEM_LIMIT),
        name="dilated_attn",
    )(q, k, v)


def _retention_phases(r_ref, qdec_ref, kdec_ref, dmask_ref, cdec_ref, bd_ref, ones_ref, y_ref,
                      state_ref, qd_ref, o_ref, kv_ref, stb_ref):
    c = RET_CHUNK
    n_chunks = r_ref.shape[0] // c

    @pl.when(pl.program_id(1) == 0)
    def _():
        state_ref[...] = jnp.zeros_like(state_ref)

    lane_q = _iota((c, 128), 1) // QK_DIM_R
    lane_v = _iota((c, 256), 1) // V_DIM_R
    chunk_rows = [slice(ci * c, (ci + 1) * c) for ci in range(n_chunks)]
    sc = {}

    def scores():
        bd = bd_ref[...] > 0
        for ci, rows in enumerate(chunk_rows):
            q = r_ref[rows, 0:128]
            k = r_ref[rows, 128:256]
            sc[ci] = _dot_nt(q, _head_stack(k, lane_q, N_HEADS_R))
            qd_ref[rows, :] = (q.astype(F32) * qdec_ref[...]).astype(BF16)
            kd = (k.astype(F32) * kdec_ref[...]).astype(BF16)
            kv_ref[ci] = jnp.where(bd, _dot_tn(kd, r_ref[rows, 256:512]), 0.0)

    def values():
        for ci, rows in enumerate(chunk_rows):
            p = (sc[ci] * dmask_ref[...]).astype(BF16)
            o_ref[rows, :] = _dot(p, _head_stack(r_ref[rows, 256:512], lane_v, N_HEADS_R))

    def scan():
        st = state_ref[...]
        for ci in range(n_chunks):
            stb_ref[ci] = st.astype(BF16)
            st = cdec_ref[...] * st + kv_ref[ci]
        state_ref[...] = st

    def inter():
        for ci, rows in enumerate(chunk_rows):
            o_ref[rows, :] += _dot(qd_ref[rows, :], stb_ref[ci])

    norm = {}

    def center():
        o = o_ref[...]
        norm["dev"] = o - _group_mean(o, ones_ref[...], V_DIM_R)

    def spread():
        norm["var"] = _group_mean(norm["dev"] * norm["dev"], ones_ref[...], V_DIM_R)

    def finish():
        y = norm["dev"] * lax.rsqrt(norm["var"] + HEAD_NORM_EPS) * _silu(r_ref[:, 512:768].astype(F32))
        y_ref[...] = y.astype(BF16)

    return [scores, values, scan, inter, center, spread, finish]


def _gla_phases(g_ref, la_ref, tri_ref, bdt_ref, ones_ref, y_ref,
                state_ref, cum_ref, qt_ref, o_ref, kv_ref, stb_ref):
    c = GLA_CHUNK
    tm = g_ref.shape[0]
    n_chunks = tm // c

    @pl.when(pl.program_id(1) == 0)
    def _():
        state_ref[...] = jnp.zeros_like(state_ref)

    lane_q = _iota((c, 128), 1) // QK_DIM_G
    lane_v = _iota((c, 256), 1) // V_DIM_G
    chunk_rows = [slice(ci * c, (ci + 1) * c) for ci in range(n_chunks)]
    sc = {}
    decays = {}

    def cumsum():
        for part in range(tm // CUM_BLOCK):
            rows = slice(part * CUM_BLOCK, (part + 1) * CUM_BLOCK)
            la_hi, la_lo = _split_hi_lo(la_ref[rows, :])
            cum_ref[rows, :] = _dot(tri_ref[...], la_hi) + _dot(tri_ref[...], la_lo)

    def scores():
        bdt = bdt_ref[...] > 0
        for ci, rows in enumerate(chunk_rows):
            q = g_ref[rows, 0:128].astype(F32)
            k = g_ref[rows, 128:256].astype(F32)
            cum = cum_ref[rows, :]
            last = cum[c - 1:c, :]
            q_t = (q * jnp.exp(cum)).astype(BF16)
            k_t = (k * jnp.exp(-cum)).astype(BF16)
            k_l = (k * jnp.exp(last - cum)).astype(BF16)
            qt_ref[rows, :] = q_t
            sc[ci] = _dot_nt(q_t, _head_stack(k_t, lane_q, N_HEADS_G))
            kv_ref[ci] = jnp.where(bdt, _dot_tn(g_ref[rows, 256:512], k_l), 0.0)
            decays[ci] = jnp.exp(last)

    def values():
        causal = _iota((c, N_HEADS_G * c), 0) >= (_iota((c, N_HEADS_G * c), 1) % c)
        for ci, rows in enumerate(chunk_rows):
            att = jnp.where(causal, sc[ci], 0.0).astype(BF16)
            o_ref[rows, :] = _dot(att, _head_stack(g_ref[rows, 256:512], lane_v, N_HEADS_G))

    def scan():
        st = state_ref[...]
        for ci in range(n_chunks):
            stb_ref[ci] = st.astype(BF16)
            st = decays[ci] * st + kv_ref[ci]
        state_ref[...] = st

    def inter():
        for ci, rows in enumerate(chunk_rows):
            o_ref[rows, :] += _dot_nt(qt_ref[rows, :], stb_ref[ci])

    norm = {}

    def center():
        o = o_ref[...]
        norm["dev"] = o - _group_mean(o, ones_ref[...], V_DIM_G)

    def spread():
        norm["var"] = _group_mean(norm["dev"] * norm["dev"], ones_ref[...], V_DIM_G)

    def finish():
        y = norm["dev"] * lax.rsqrt(norm["var"] + HEAD_NORM_EPS) * _silu(g_ref[:, 512:768].astype(F32))
        y_ref[...] = y.astype(BF16)

    return [cumsum, scores, values, scan, inter, center, spread, finish]


N_RET_SCRATCH = 5


def _linear_mixers_kernel(r_ref, qdec_ref, kdec_ref, dmask_ref, cdec_ref, bd_ref, ones_ref,
                          g_ref, la_ref, tri_ref, bdt_ref, yr_ref, yg_ref, *scratch):
    ret = _retention_phases(r_ref, qdec_ref, kdec_ref, dmask_ref, cdec_ref, bd_ref, ones_ref, yr_ref,
                            *scratch[:N_RET_SCRATCH])
    gla = _gla_phases(g_ref, la_ref, tri_ref, bdt_ref, ones_ref, yg_ref, *scratch[N_RET_SCRATCH:])
    gla.pop(0)()
    for ret_phase, gla_phase in zip(ret, gla):
        ret_phase()
        gla_phase()


def _linear_mixers(r3, g3, la3, ret_tabs, gla_tabs):
    b, seq, _ = r3.shape
    tm = ROW_TILE
    qdec, kdec, dmask, cdec, bd, ones = ret_tabs
    tri, bdt, _ = gla_tabs
    fixed2 = lambda n, i: (0, 0)
    row = lambda n, i: (n, i, 0)
    y_shape = jax.ShapeDtypeStruct((b, seq, 256), BF16)
    return pl.pallas_call(
        _linear_mixers_kernel,
        grid=(b, seq // tm),
        in_specs=[
            pl.BlockSpec((None, tm, 768), row),
            pl.BlockSpec(qdec.shape, fixed2),
            pl.BlockSpec(kdec.shape, fixed2),
            pl.BlockSpec(dmask.shape, fixed2),
            pl.BlockSpec(cdec.shape, fixed2),
            pl.BlockSpec(bd.shape, fixed2),
            pl.BlockSpec(ones.shape, fixed2),
            pl.BlockSpec((None, tm, 768), row),
            pl.BlockSpec((None, tm, LANES), row),
            pl.BlockSpec(tri.shape, fixed2),
            pl.BlockSpec(bdt.shape, fixed2),
        ],
        out_specs=[pl.BlockSpec((None, tm, 256), row), pl.BlockSpec((None, tm, 256), row)],
        out_shape=[y_shape, y_shape],
        scratch_shapes=[
            pltpu.VMEM((128, 256), F32),
            pltpu.VMEM((tm, 128), BF16),
            pltpu.VMEM((tm, 256), F32),
            pltpu.VMEM((tm // RET_CHUNK, 128, 256), F32),
            pltpu.VMEM((tm // RET_CHUNK, 128, 256), BF16),
            pltpu.VMEM((256, 128), F32),
            pltpu.VMEM((tm, 128), F32),
            pltpu.VMEM((tm, 128), BF16),
            pltpu.VMEM((tm, 256), F32),
            pltpu.VMEM((tm // GLA_CHUNK, 256, 128), F32),
            pltpu.VMEM((tm // GLA_CHUNK, 256, 128), BF16),
        ],
        compiler_params=pltpu.CompilerParams(
            dimension_semantics=("arbitrary", "arbitrary"), vmem_limit_bytes=VMEM_LIMIT),
        name="linear_mixers",
    )(r3, qdec, kdec, dmask, cdec, bd, ones, g3, la3, tri, bdt)


def _mix_out_kernel(x_ref, o1_ref, o4_ref, o16_ref, l1_ref, l4_ref, l16_ref, yr_ref, yg_ref,
                    ms_ref, wout_ref, lng_ref, lnb_ref, ones_ref, out_ref, on_ref, ln_ref, y_ref):
    tm = x_ref.shape[0]
    n_slabs = WIDTH_A // LANES

    def to_sequence_order(part):
        for bi, (o_ref, l_ref, d) in enumerate(((o4_ref, l4_ref, DILATIONS[1]), (o16_ref, l16_ref, DILATIONS[2]))):
            n = PART_ROWS // d
            src = slice(part * n, (part + 1) * n)
            for r in range(d):
                dst = pl.ds(part * PART_ROWS + r, n, stride=d)
                ln_ref[bi, dst, :] = l_ref[r, src, :]
                for s in range(n_slabs):
                    on_ref[bi, s, dst, :] = o_ref[r, src, s * LANES:(s + 1) * LANES].astype(F32)

    low = _iota((SUB_ROWS, LANES), 1) < HEAD_DIM_A
    ms = ms_ref[...]

    def merge(rb):
        rows = slice(rb * SUB_ROWS, (rb + 1) * SUB_ROWS)
        l1 = l1_ref[rows, :]
        l4 = ln_ref[0, rows, :]
        l16 = ln_ref[1, rows, :]
        top = jnp.maximum(jnp.maximum(l1, l4), l16)
        e1 = jnp.exp(l1 - top)
        e4 = jnp.exp(l4 - top)
        e16 = jnp.exp(l16 - top)
        inv = 1.0 / (e1 + e4 + e16)
        w4 = e4 * inv
        w16 = e16 * inv
        for s in range(n_slabs):
            sl = slice(s * LANES, (s + 1) * LANES)

            def expand(w):
                c0 = HEAD_DIM_A + 16 * s
                c1 = 16 * s
                return jnp.where(low, w[:, c0:c0 + 1], w[:, c1:c1 + 1])

            x4 = expand(w4)
            x16 = expand(w16)
            merged = ((1.0 - x4 - x16) * o1_ref[rows, sl].astype(F32) + x4 * on_ref[0, s, rows, :]
                      + x16 * on_ref[1, s, rows, :])
            ya = _head_norm(merged, ones_ref[...], HEAD_DIM_A)
            y_ref[rows, sl] = (ya * ms[:, sl]).astype(BF16)
        y_ref[rows, 512:768] = (yr_ref[rows, :].astype(F32) * ms[:, 512:768]).astype(BF16)
        y_ref[rows, 768:1024] = (yg_ref[rows, :].astype(F32) * ms[:, 768:1024]).astype(BF16)

    def project(part):
        rows = slice(part * PART_ROWS, (part + 1) * PART_ROWS)
        return _dot(y_ref[rows, :], wout_ref[...])

    def finish(part, proj):
        rows = slice(part * PART_ROWS, (part + 1) * PART_ROWS)
        z = DEEPNORM_ALPHA * x_ref[rows, :] + proj
        out_ref[rows, :] = _layer_norm(z, lng_ref[...], lnb_ref[...])

    n_parts = tm // PART_ROWS
    proj = None
    for part in range(n_parts):
        to_sequence_order(part)
        for rb in range(part * (PART_ROWS // SUB_ROWS), (part + 1) * (PART_ROWS // SUB_ROWS)):
            merge(rb)
        if proj is not None:
            finish(part - 1, proj)
        proj = project(part)
    finish(n_parts - 1, proj)


def _mix_out(x3, o1, o4, o16, l1, l4, l16, yr, yg, ms, w_out, ln_g, ln_b, ones):
    b, seq, _ = x3.shape
    tm = ROW_TILE
    row = lambda n, i: (n, i, 0)
    cls = lambda n, i: (n, 0, i, 0)
    fixed = lambda n, i: (0, 0)
    d4, d16 = DILATIONS[1], DILATIONS[2]
    return pl.pallas_call(
        _mix_out_kernel,
        grid=(b, seq // tm),
        in_specs=[
            pl.BlockSpec((None, tm, D_MODEL), row),
            pl.BlockSpec((None, tm, WIDTH_A), row),
            pl.BlockSpec((None, d4, tm // d4, WIDTH_A), cls),
            pl.BlockSpec((None, d16, tm // d16, WIDTH_A), cls),
            pl.BlockSpec((None, tm, LANES), row),
            pl.BlockSpec((None, d4, tm // d4, LANES), cls),
            pl.BlockSpec((None, d16, tm // d16, LANES), cls),
            pl.BlockSpec((None, tm, 256), row),
            pl.BlockSpec((None, tm, 256), row),
            pl.BlockSpec(ms.shape, fixed),
            pl.BlockSpec(w_out.shape, fixed),
            pl.BlockSpec(ln_g.shape, fixed),
            pl.BlockSpec(ln_b.shape, fixed),
            pl.BlockSpec(ones.shape, fixed),
        ],
        out_specs=pl.BlockSpec((None, tm, D_MODEL), row),
        out_shape=jax.ShapeDtypeStruct((b, seq, D_MODEL), F32),
        scratch_shapes=[
            pltpu.VMEM((2, WIDTH_A // LANES, tm, LANES), F32),
            pltpu.VMEM((2, tm, LANES), F32),
            pltpu.VMEM((tm, D_MODEL), BF16),
        ],
        compiler_params=pltpu.CompilerParams(
            dimension_semantics=("arbitrary", "arbitrary"), vmem_limit_bytes=VMEM_LIMIT),
        name="mix_out",
    )(x3, o1, o4, o16, l1, l4, l16, yr, yg, ms, w_out, ln_g, ln_b, ones)


def _ffn_kernel(x_ref, wup_ref, cw_ref, wdn_ref, lng_ref, lnb_ref, out_ref, u_ref, carry_ref, acc_ref, xb_ref):
    assert N_FF_CHUNKS % 2 == 1 and N_FF_CHUNKS >= 3
    tm = x_ref.shape[0]
    halo = 8

    @pl.when(pl.program_id(1) == 0)
    def _():
        carry_ref[...] = jnp.zeros_like(carry_ref)

    xb_ref[...] = x_ref[...].astype(BF16)

    def produce(j, buf):
        u_ref[buf, 0:halo, :] = carry_ref[j]
        u_ref[buf, halo:halo + tm, :] = _dot(xb_ref[...], wup_ref[j])
        carry_ref[j] = u_ref[buf, tm:tm + halo, :]

    def consume(j, buf, first=False):
        cw = cw_ref[j]
        u = u_ref[buf, halo:halo + tm, :]
        y = u * cw[2:3, :] + pltpu.roll(u, 1, 0) * cw[1:2, :] + pltpu.roll(u, 2, 0) * cw[0:1, :] + cw[3:4, :]
        top = (u_ref[buf, halo:2 * halo, :] * cw[2:3, :] + u_ref[buf, halo - 1:2 * halo - 1, :] * cw[1:2, :]
               + u_ref[buf, halo - 2:2 * halo - 2, :] * cw[0:1, :] + cw[3:4, :])
        y = jnp.concatenate([top, y[halo:, :]], axis=0)
        h = _silu(y[:, :FF_CHUNK]) * y[:, FF_CHUNK:]
        d = _dot(h.astype(BF16), wdn_ref[j])
        if first:
            acc_ref[...] = d
        else:
            acc_ref[...] += d

    produce(0, 0)
    produce(1, 1)
    consume(0, 0, first=True)

    def pair(p, _):
        j = 2 * p + 1
        produce(j + 1, 0)
        consume(j, 1)
        produce(j + 2, 1)
        consume(j + 1, 0)
        return 0

    lax.fori_loop(0, (N_FF_CHUNKS - 3) // 2, pair, 0)
    produce(N_FF_CHUNKS - 1, 0)
    consume(N_FF_CHUNKS - 2, 1)
    consume(N_FF_CHUNKS - 1, 0)
    z = DEEPNORM_ALPHA * x_ref[...] + acc_ref[...]
    out_ref[...] = _layer_norm(z, lng_ref[...], lnb_ref[...])


def _ffn(x3, wup_c, cw_c, wdn_c, ln_g, ln_b):
    b, seq, _ = x3.shape
    tm = FFN_TILE
    fixed3 = lambda n, i: (0, 0, 0)
    fixed2 = lambda n, i: (0, 0)
    return pl.pallas_call(
        _ffn_kernel,
        grid=(b, seq // tm),
        in_specs=[
            pl.BlockSpec((None, tm, D_MODEL), lambda n, i: (n, i, 0)),
            pl.BlockSpec(wup_c.shape, fixed3, pipeline_mode=pl.Buffered(1)),
            pl.BlockSpec(cw_c.shape, fixed3),
            pl.BlockSpec(wdn_c.shape, fixed3, pipeline_mode=pl.Buffered(1)),
            pl.BlockSpec(ln_g.shape, fixed2),
            pl.BlockSpec(ln_b.shape, fixed2),
        ],
        out_specs=pl.BlockSpec((None, tm, D_MODEL), lambda n, i: (n, i, 0)),
        out_shape=jax.ShapeDtypeStruct((b, seq, D_MODEL), F32),
        scratch_shapes=[
            pltpu.VMEM((2, tm + 8, 2 * FF_CHUNK), F32),
            pltpu.VMEM((N_FF_CHUNKS, 8, 2 * FF_CHUNK), F32),
            pltpu.VMEM((tm, D_MODEL), F32),
            pltpu.VMEM((tm, D_MODEL), BF16),
        ],
        compiler_params=pltpu.CompilerParams(
            dimension_semantics=("arbitrary", "arbitrary"), vmem_limit_bytes=VMEM_LIMIT),
        name="conv_glu_ffn",
    )(x3, wup_c, cw_c, wdn_c, ln_g, ln_b)


def _rope_tables(seq, dim):
    inv = 1.0 / (ROPE_THETA ** (jnp.arange(0, dim, 2, dtype=F32) / dim))
    ang = jnp.arange(seq, dtype=F32)[:, None] * inv[None, :]
    cos, sin = jnp.cos(ang), jnp.sin(ang)
    reps = LANES // dim
    cos_l = jnp.tile(jnp.concatenate([cos, cos], axis=1), (1, reps))
    sin_l = jnp.tile(jnp.concatenate([-sin, sin], axis=1), (1, reps))
    return cos_l, sin_l


def _block_diag_ones(n, group):
    idx = np.arange(n) // group
    return jnp.asarray(idx[:, None] == idx[None, :], BF16)


def _retention_tables():
    c = RET_CHUNK
    h = N_HEADS_R
    lg = jnp.log(1.0 - jnp.power(2.0, -5.0 - jnp.arange(h, dtype=F32)))
    idx = jnp.arange(c, dtype=F32)
    dist = idx[:, None] - idx[None, :]
    dmask = jnp.where(dist >= 0, jnp.exp(lg[:, None, None] * jnp.maximum(dist, 0.0)), 0.0)
    dmask_l = jnp.concatenate([dmask[i] for i in range(h)], axis=1)
    q_dec = jnp.exp(lg[:, None] * (idx + 1.0))
    k_dec = jnp.exp(lg[:, None] * (c - 1.0 - idx))
    chunk_dec = jnp.exp(lg * c)
    qdec_l = jnp.repeat(q_dec.T, QK_DIM_R, axis=1)
    kdec_l = jnp.repeat(k_dec.T, QK_DIM_R, axis=1)
    row_h = np.arange(128) // QK_DIM_R
    col_h = np.arange(256) // V_DIM_R
    bd = jnp.asarray(row_h[:, None] == col_h[None, :], F32)
    cdec = bd * chunk_dec[row_h][:, None]
    return qdec_l, kdec_l, dmask_l, cdec, bd, _block_diag_ones(256, V_DIM_R)


def _gla_tables():
    r = np.arange(CUM_BLOCK)
    tri = jnp.asarray((r[:, None] >= r[None, :]) & (r[:, None] // GLA_CHUNK == r[None, :] // GLA_CHUNK), BF16)
    row_h = np.arange(256) // V_DIM_G
    col_h = np.arange(128) // QK_DIM_G
    bdt = jnp.asarray(row_h[:, None] == col_h[None, :], F32)
    return tri, bdt, _block_diag_ones(256, V_DIM_G)


def kernel(x, w_in, w_alpha, b_alpha, mix_scale, w_out, ln1_g, ln1_b, w_up, conv_w, conv_b, w_down, ln2_g, ln2_b):
    b, seq, d_model = x.shape
    assert (d_model, w_in.shape[0]) == (D_MODEL, DEPTH)
    assert seq % max(ROW_TILE, FFN_TILE, INPROJ_TILE) == 0
    assert (seq // DILATIONS[2]) % ATT_BLK == 0 and seq // DILATIONS[2] >= 2 * ATT_BLK
    ca, sa = _rope_tables(seq, HEAD_DIM_A)
    cr, sr = _rope_tables(seq, QK_DIM_R)
    ret_tabs = _retention_tables()
    gla_tabs = _gla_tables()
    ones_a = _block_diag_ones(LANES, HEAD_DIM_A)
    pad_cols = 3 * WIDTH_A + 768 + 768 + LANES - w_in.shape[2]

    for l in range(DEPTH):
        w_all = jnp.pad(w_in[l], ((0, 0), (0, pad_cols))).astype(BF16)
        w_al = jnp.pad(w_alpha[l], ((0, LANES - GLA_LOW_RANK), (0, 0))).astype(BF16)
        (q1, k1, v1, q4, k4, v4, q16, k16, v16, r3, g3, la3) = _inproj(
            x, w_all, w_al, b_alpha[l][None, :], ca, sa, cr, sr)

        outs, lses = [], []
        for d, (q, k, v) in zip(DILATIONS, ((q1, k1, v1), (q4, k4, v4), (q16, k16, v16))):
            shp = (b * d, seq // d, WIDTH_A)
            o, lse = _attention(q.reshape(shp), k.reshape(shp), v.reshape(shp))
            if d > 1:
                o = o.reshape(b, d, seq // d, WIDTH_A)
                lse = lse.reshape(b, d, seq // d, LANES)
            outs.append(o)
            lses.append(lse)

        yr, yg = _linear_mixers(r3, g3, la3, ret_tabs, gla_tabs)

        x = _mix_out(x, outs[0], outs[1], outs[2], lses[0], lses[1], lses[2], yr, yg,
                     mix_scale[l][None, :], w_out[l].astype(BF16), ln1_g[l][None, :], ln1_b[l][None, :], ones_a)

        wu = w_up[l].astype(BF16)
        wup_c = jnp.concatenate(
            [wu[:, :D_FF].reshape(D_MODEL, N_FF_CHUNKS, FF_CHUNK), wu[:, D_FF:].reshape(D_MODEL, N_FF_CHUNKS, FF_CHUNK)],
            axis=2).transpose(1, 0, 2)
        taps = jnp.concatenate([conv_w[l], conv_b[l][None, :], jnp.zeros((4, 2 * D_FF), F32)], axis=0)
        cw_c = jnp.concatenate(
            [taps[:, :D_FF].reshape(8, N_FF_CHUNKS, FF_CHUNK), taps[:, D_FF:].reshape(8, N_FF_CHUNKS, FF_CHUNK)],
            axis=2).transpose(1, 0, 2)
        wdn_c = w_down[l].astype(BF16).reshape(N_FF_CHUNKS, FF_CHUNK, D_MODEL)
        x = _ffn(x, wup_c, cw_c, wdn_c, ln2_g[l][None, :], ln2_b[l][None, :])
    return x
```

```python
import jax
import jax.numpy as jnp
import numpy as np
from jax import lax
from jax.experimental import pallas as pl
from jax.experimental.pallas import tpu as pltpu

F32 = jnp.float32
BF16 = jnp.bfloat16

D_MODEL = 1024
DEPTH = 4
HEAD_DIM_A = 64
WIDTH_A = 512
DILATIONS = (1, 4, 16)
ATT_BLK = 128
ATT_STEP_ROWS = 1024
ROPE_THETA = 10000.0
N_HEADS_R = 4
QK_DIM_R = 32
V_DIM_R = 64
RET_CHUNK = 128
N_HEADS_G = 4
QK_DIM_G = 32
V_DIM_G = 64
GLA_LOW_RANK = 16
GLA_TAU = 16.0
GLA_CHUNK = 64
CUM_BLOCK = 256
D_FF = 2816
FF_CHUNK = 256
N_FF_CHUNKS = D_FF // FF_CHUNK
DEEPNORM_ALPHA = (2 * DEPTH) ** 0.25
LOG2E = 1.4426950408889634
LN2 = 0.6931471805599453
LN_EPS = 1e-5
HEAD_NORM_EPS = 1e-6
LANES = 128
ROW_TILE = 1024
FFN_TILE = 1024
INPROJ_TILE = 1024
SUB_ROWS = 128
PART_ROWS = 256
VMEM_LIMIT = 56 * 1024 * 1024


def _iota(shape, dim):
    return lax.broadcasted_iota(jnp.int32, shape, dim)


def _dot(a, b):
    return jnp.dot(a, b, preferred_element_type=F32)


def _dot_nt(a, b):
    return lax.dot_general(a, b, (((1,), (1,)), ((), ())), preferred_element_type=F32)


def _dot_tn(a, b):
    return lax.dot_general(a, b, (((0,), (0,)), ((), ())), preferred_element_type=F32)


def _split_hi_lo(x):
    hi = x.astype(BF16)
    lo = (x - hi.astype(F32)).astype(BF16)
    return hi, lo


def _group_mean(x, ones_bd, group):
    hi, lo = _split_hi_lo(x)
    return (_dot(hi, ones_bd) + _dot(lo, ones_bd)) * (1.0 / group)


def _head_norm(x, ones_bd, group):
    mu = _group_mean(x, ones_bd, group)
    d = x - mu
    var = _group_mean(d * d, ones_bd, group)
    return d * lax.rsqrt(var + HEAD_NORM_EPS)


def _silu(g):
    return g * (1.0 / (1.0 + jnp.exp(-g)))


def _layer_norm(z, g, b):
    mu = jnp.mean(z, axis=-1, keepdims=True)
    d = z - mu
    var = jnp.mean(d * d, axis=-1, keepdims=True)
    return d * lax.rsqrt(var + LN_EPS) * g + b


def _head_stack(t, lane_head, n_heads):
    return jnp.concatenate([jnp.where(lane_head == h, t, jnp.zeros_like(t)) for h in range(n_heads)], axis=0)


def _rope(t, cos, sin_signed, half):
    lane = _iota(t.shape, 1)
    first = (lane % (2 * half)) < half
    partner = jnp.where(first, pltpu.roll(t, LANES - half, 1), pltpu.roll(t, half, 1))
    return t * cos + partner * sin_signed


def _inproj_kernel(x_ref, w_ref, walpha_ref, balpha_ref, ca_ref, sa_ref, cr_ref, sr_ref,
                   q1_ref, k1_ref, v1_ref, q4_ref, k4_ref, v4_ref, q16_ref, k16_ref, v16_ref,
                   r_ref, g_ref, la_ref, slab_ref, cls_ref):
    tm = x_ref.shape[0]
    n_slabs = WIDTH_A // LANES
    xb = x_ref[...].astype(BF16)

    def mm(c0, n):
        return _dot(xb, w_ref[:, c0:c0 + n])

    outs = ((q1_ref, q4_ref, q16_ref), (k1_ref, k4_ref, k16_ref), (v1_ref, v4_ref, v16_ref))
    base = 3 * WIDTH_A

    def finish_a(ti, t):
        for s in range(n_slabs):
            sl = slice(s * LANES, (s + 1) * LANES)
            slab = t[:, sl]
            if ti < 2:
                slab = _rope(slab, ca_ref[...], sa_ref[...], HEAD_DIM_A // 2)
            if ti == 0:
                slab = slab * (LOG2E * HEAD_DIM_A ** -0.5)
            slab_ref[ti * n_slabs + s] = slab
            outs[ti][0][:, sl] = slab.astype(BF16)
        d4, d16 = DILATIONS[1], DILATIONS[2]
        n4 = tm // d4
        for s in range(n_slabs):
            sl = slice(s * LANES, (s + 1) * LANES)
            for r4 in range(d4):
                cls4 = slab_ref[ti * n_slabs + s, pl.ds(r4, n4, stride=d4), :]
                outs[ti][1][r4, :, sl] = cls4.astype(BF16)
                cls_ref[s, r4 * n4:(r4 + 1) * n4, :] = cls4
            for r4 in range(d4):
                for c in range(d16 // d4):
                    cls16 = cls_ref[s, pl.ds(r4 * n4 + c, tm // d16, stride=d16 // d4), :]
                    outs[ti][2][r4 + d4 * c, :, sl] = cls16.astype(BF16)

    def finish_r(rr):
        cr = cr_ref[...]
        sr = sr_ref[...]
        r_ref[:, 0:128] = _rope(rr[:, 0:128], cr, sr, QK_DIM_R // 2).astype(BF16)
        r_ref[:, 128:256] = (_rope(rr[:, 128:256], cr, sr, QK_DIM_R // 2) * (QK_DIM_R ** -0.5)).astype(BF16)
        r_ref[:, 256:768] = rr[:, 256:768].astype(BF16)

    def finish_g(gg):
        g_ref[:, 0:128] = (gg[:, 0:128] * (QK_DIM_G ** -0.5)).astype(BF16)
        g_ref[:, 128:768] = gg[:, 128:768].astype(BF16)
        ag = gg[:, 768:896].astype(BF16)
        z = _dot(ag, walpha_ref[...]) + balpha_ref[...]
        log_sig = jnp.minimum(z, 0.0) - jnp.log(1.0 + jnp.exp(-jnp.abs(z)))
        la_ref[...] = log_sig * (1.0 / GLA_TAU)

    stages = [
        (lambda: mm(2 * WIDTH_A, WIDTH_A), lambda t: finish_a(2, t)),
        (lambda: mm(0, WIDTH_A), lambda t: finish_a(0, t)),
        (lambda: mm(base, 768), finish_r),
        (lambda: mm(WIDTH_A, WIDTH_A), lambda t: finish_a(1, t)),
        (lambda: mm(base + 768, 896), finish_g),
    ]
    prev = None
    for matmul, finish in stages:
        t = matmul()
        if prev is not None:
            prev[0](prev[1])
        prev = (finish, t)
    prev[0](prev[1])


def _inproj(x3, w_all, w_alpha, b_alpha, ca, sa, cr, sr):
    b, seq, _ = x3.shape
    tm = INPROJ_TILE
    row = lambda n, i: (n, i, 0)
    cls = lambda n, i: (n, 0, i, 0)
    fixed = lambda n, i: (0, 0)
    tab = lambda n, i: (i, 0)
    d4, d16 = DILATIONS[1], DILATIONS[2]
    nat = jax.ShapeDtypeStruct((b, seq, WIDTH_A), BF16)
    c4 = jax.ShapeDtypeStruct((b, d4, seq // d4, WIDTH_A), BF16)
    c16 = jax.ShapeDtypeStruct((b, d16, seq // d16, WIDTH_A), BF16)
    nat_spec = pl.BlockSpec((None, tm, WIDTH_A), row)
    c4_spec = pl.BlockSpec((None, d4, tm // d4, WIDTH_A), cls)
    c16_spec = pl.BlockSpec((None, d16, tm // d16, WIDTH_A), cls)
    return pl.pallas_call(
        _inproj_kernel,
        grid=(b, seq // tm),
        in_specs=[
            pl.BlockSpec((None, tm, D_MODEL), row),
            pl.BlockSpec(w_all.shape, fixed, pipeline_mode=pl.Buffered(1)),
            pl.BlockSpec(w_alpha.shape, fixed),
            pl.BlockSpec(b_alpha.shape, fixed),
            pl.BlockSpec((tm, LANES), tab),
            pl.BlockSpec((tm, LANES), tab),
            pl.BlockSpec((tm, LANES), tab),
            pl.BlockSpec((tm, LANES), tab),
        ],
        out_specs=[nat_spec] * 3 + [c4_spec] * 3 + [c16_spec] * 3 + [
            pl.BlockSpec((None, tm, 768), row),
            pl.BlockSpec((None, tm, 768), row),
            pl.BlockSpec((None, tm, LANES), row),
        ],
        out_shape=[nat] * 3 + [c4] * 3 + [c16] * 3 + [
            jax.ShapeDtypeStruct((b, seq, 768), BF16),
            jax.ShapeDtypeStruct((b, seq, 768), BF16),
            jax.ShapeDtypeStruct((b, seq, LANES), F32),
        ],
        scratch_shapes=[
            pltpu.VMEM((3 * (WIDTH_A // LANES), tm, LANES), F32),
            pltpu.VMEM((WIDTH_A // LANES, tm, LANES), F32),
        ],
        compiler_params=pltpu.CompilerParams(
            dimension_semantics=("arbitrary", "arbitrary"), vmem_limit_bytes=VMEM_LIMIT),
        name="inproj",
    )(x3, w_all, w_alpha, b_alpha, ca, sa, cr, sr)


def _attn_kernel(q_ref, k_ref, v_ref, o_ref, lse_ref):
    blk = ATT_BLK
    n_cls, tq, _ = q_ref.shape
    nsub = tq // blk
    n_slabs = WIDTH_A // LANES
    i = pl.program_id(1)

    qi = _iota((blk, 2 * blk), 0)
    kj = _iota((blk, 2 * blk), 1)
    band = (kj >= qi) & (kj <= qi + blk)
    kj0 = kj + jnp.where(i == 0, blk, 0)
    first = (kj0 >= qi) & (kj0 <= qi + blk)
    lane = _iota((blk, LANES), 1)
    low_q = lane < HEAD_DIM_A
    low_v = _iota((2 * blk, LANES), 1) < HEAD_DIM_A

    units = [(g, jb, s, hh) for g in range(n_cls) for jb in range(nsub) for s in range(n_slabs) for hh in range(2)]

    def key_rows(jb):
        if jb == 0:
            start = jnp.maximum(i * tq - blk, 0)
        else:
            start = i * tq + (jb - 1) * blk
        return pl.ds(pl.multiple_of(start, blk), 2 * blk)

    def scores(g, jb, s, hh):
        sl = slice(s * LANES, (s + 1) * LANES)
        qs = q_ref[g, jb * blk:(jb + 1) * blk, sl]
        sel_q = low_q if hh == 0 else jnp.logical_not(low_q)
        qm = jnp.where(sel_q, qs, jnp.zeros_like(qs))
        return _dot_nt(qm, k_ref[g, key_rows(jb), sl])

    def weighted(g, jb, s, hh, sc):
        sl = slice(s * LANES, (s + 1) * LANES)
        sc = jnp.where(first if jb == 0 else band, sc, -jnp.inf)
        m = jnp.max(sc, axis=-1, keepdims=True)
        p = jnp.exp2(sc - m).astype(BF16)
        vs = v_ref[g, key_rows(jb), sl]
        sel_v = low_v if hh == 0 else jnp.logical_not(low_v)
        vm = jnp.where(sel_v, vs, jnp.ones_like(vs))
        r = _dot(p, vm)
        return r, m

    skew = 5
    pending = {}
    done = {}
    den_parts = {}
    max_parts = {}
    grp = (lane % HEAD_DIM_A) // (HEAD_DIM_A // n_slabs)

    def gather_heads(parts):
        tile = parts[n_slabs - 1]
        for s2 in range(n_slabs - 1):
            tile = jnp.where(grp == s2, parts[s2], tile)
        return tile

    for t in range(len(units) + skew):
        if t < len(units):
            pending[t] = scores(*units[t])
        if t >= skew:
            g, jb, s, hh = units[t - skew]
            done[hh] = weighted(g, jb, s, hh, pending.pop(t - skew))
            if hh == 1:
                (r0, m0), (r1, m1) = done[0], done[1]
                num = jnp.where(low_q, r0, r1)
                den_parts[s] = jnp.where(low_q, r1, r0)
                max_parts[s] = jnp.where(low_q, m1, m0)
                den = pltpu.roll(den_parts[s], HEAD_DIM_A, 1)
                o_ref[g, jb * blk:(jb + 1) * blk, s * LANES:(s + 1) * LANES] = (num / den).astype(BF16)
                if s == n_slabs - 1:
                    lse2 = gather_heads(max_parts) + jnp.log2(gather_heads(den_parts))
                    lse_ref[g, jb * blk:(jb + 1) * blk, :] = lse2 * LN2


def _attention(q, k, v):
    nb, length, _ = q.shape
    blk = min(ATT_STEP_ROWS, length)
    n_cls = ATT_STEP_ROWS // blk
    return pl.pallas_call(
        _attn_kernel,
        grid=(nb // n_cls, length // blk),
        in_specs=[
            pl.BlockSpec((n_cls, blk, WIDTH_A), lambda n, i: (n, i, 0)),
            pl.BlockSpec((n_cls, length, WIDTH_A), lambda n, i: (n, 0, 0)),
            pl.BlockSpec((n_cls, length, WIDTH_A), lambda n, i: (n, 0, 0)),
        ],
        out_specs=[
            pl.BlockSpec((n_cls, blk, WIDTH_A), lambda n, i: (n, i, 0)),
            pl.BlockSpec((n_cls, blk, LANES), lambda n, i: (n, i, 0)),
        ],
        out_shape=[
            jax.ShapeDtypeStruct((nb, length, WIDTH_A), BF16),
            jax.ShapeDtypeStruct((nb, length, LANES), F32),
        ],
        compiler_params=pltpu.CompilerParams(
            dimension_semantics=("arbitrary", "arbitrary"), vmem_limit_bytes=VMEM_LIMIT),
        name="dilated_attn",
    )(q, k, v)


def _retention_phases(r_ref, qdec_ref, kdec_ref, dmask_ref, cdec_ref, bd_ref, ones_ref, y_ref,
                      state_ref, qd_ref, o_ref, kv_ref, stb_ref):
    c = RET_CHUNK
    n_chunks = r_ref.shape[0] // c

    @pl.when(pl.program_id(1) == 0)
    def _():
        state_ref[...] = jnp.zeros_like(state_ref)

    lane_q = _iota((c, 128), 1) // QK_DIM_R
    lane_v = _iota((c, 256), 1) // V_DIM_R
    chunk_rows = [slice(ci * c, (ci + 1) * c) for ci in range(n_chunks)]
    sc = {}

    def scores():
        bd = bd_ref[...] > 0
        for ci, rows in enumerate(chunk_rows):
            q = r_ref[rows, 0:128]
            k = r_ref[rows, 128:256]
            sc[ci] = _dot_nt(q, _head_stack(k, lane_q, N_HEADS_R))
            qd_ref[rows, :] = (q.astype(F32) * qdec_ref[...]).astype(BF16)
            kd = (k.astype(F32) * kdec_ref[...]).astype(BF16)
            kv_ref[ci] = jnp.where(bd, _dot_tn(kd, r_ref[rows, 256:512]), 0.0)

    def values():
        for ci, rows in enumerate(chunk_rows):
            p = (sc[ci] * dmask_ref[...]).astype(BF16)
            o_ref[rows, :] = _dot(p, _head_stack(r_ref[rows, 256:512], lane_v, N_HEADS_R))

    def scan():
        st = state_ref[...]
        for ci in range(n_chunks):
            stb_ref[ci] = st.astype(BF16)
            st = cdec_ref[...] * st + kv_ref[ci]
        state_ref[...] = st

    def inter():
        for ci, rows in enumerate(chunk_rows):
            o_ref[rows, :] += _dot(qd_ref[rows, :], stb_ref[ci])

    norm = {}

    def center():
        o = o_ref[...]
        norm["dev"] = o - _group_mean(o, ones_ref[...], V_DIM_R)

    def spread():
        norm["var"] = _group_mean(norm["dev"] * norm["dev"], ones_ref[...], V_DIM_R)

    def finish():
        y = norm["dev"] * lax.rsqrt(norm["var"] + HEAD_NORM_EPS) * _silu(r_ref[:, 512:768].astype(F32))
        y_ref[...] = y.astype(BF16)

    return [scores, values, scan, inter, center, spread, finish]


def _gla_phases(g_ref, la_ref, tri_ref, bdt_ref, ones_ref, y_ref,
                state_ref, cum_ref, qt_ref, o_ref, kv_ref, stb_ref):
    c = GLA_CHUNK
    tm = g_ref.shape[0]
    n_chunks = tm // c

    @pl.when(pl.program_id(1) == 0)
    def _():
        state_ref[...] = jnp.zeros_like(state_ref)

    lane_q = _iota((c, 128), 1) // QK_DIM_G
    lane_v = _iota((c, 256), 1) // V_DIM_G
    chunk_rows = [slice(ci * c, (ci + 1) * c) for ci in range(n_chunks)]
    sc = {}
    decays = {}

    def cumsum():
        for part in range(tm // CUM_BLOCK):
            rows = slice(part * CUM_BLOCK, (part + 1) * CUM_BLOCK)
            la_hi, la_lo = _split_hi_lo(la_ref[rows, :])
            cum_ref[rows, :] = _dot(tri_ref[...], la_hi) + _dot(tri_ref[...], la_lo)

    def scores():
        bdt = bdt_ref[...] > 0
        for ci, rows in enumerate(chunk_rows):
            q = g_ref[rows, 0:128].astype(F32)
            k = g_ref[rows, 128:256].astype(F32)
            cum = cum_ref[rows, :]
            last = cum[c - 1:c, :]
            q_t = (q * jnp.exp(cum)).astype(BF16)
            k_t = (k * jnp.exp(-cum)).astype(BF16)
            k_l = (k * jnp.exp(last - cum)).astype(BF16)
            qt_ref[rows, :] = q_t
            sc[ci] = _dot_nt(q_t, _head_stack(k_t, lane_q, N_HEADS_G))
            kv_ref[ci] = jnp.where(bdt, _dot_tn(g_ref[rows, 256:512], k_l), 0.0)
            decays[ci] = jnp.exp(last)

    def values():
        causal = _iota((c, N_HEADS_G * c), 0) >= (_iota((c, N_HEADS_G * c), 1) % c)
        for ci, rows in enumerate(chunk_rows):
            att = jnp.where(causal, sc[ci], 0.0).astype(BF16)
            o_ref[rows, :] = _dot(att, _head_stack(g_ref[rows, 256:512], lane_v, N_HEADS_G))

    def scan():
        st = state_ref[...]
        for ci in range(n_chunks):
            stb_ref[ci] = st.astype(BF16)
            st = decays[ci] * st + kv_ref[ci]
        state_ref[...] = st

    def inter():
        for ci, rows in enumerate(chunk_rows):
            o_ref[rows, :] += _dot_nt(qt_ref[rows, :], stb_ref[ci])

    norm = {}

    def center():
        o = o_ref[...]
        norm["dev"] = o - _group_mean(o, ones_ref[...], V_DIM_G)

    def spread():
        norm["var"] = _group_mean(norm["dev"] * norm["dev"], ones_ref[...], V_DIM_G)

    def finish():
        y = norm["dev"] * lax.rsqrt(norm["var"] + HEAD_NORM_EPS) * _silu(g_ref[:, 512:768].astype(F32))
        y_ref[...] = y.astype(BF16)

    return [cumsum, scores, values, scan, inter, center, spread, finish]


N_RET_SCRATCH = 5


def _linear_mixers_kernel(r_ref, qdec_ref, kdec_ref, dmask_ref, cdec_ref, bd_ref, ones_ref,
                          g_ref, la_ref, tri_ref, bdt_ref, yr_ref, yg_ref, *scratch):
    ret = _retention_phases(r_ref, qdec_ref, kdec_ref, dmask_ref, cdec_ref, bd_ref, ones_ref, yr_ref,
                            *scratch[:N_RET_SCRATCH])
    gla = _gla_phases(g_ref, la_ref, tri_ref, bdt_ref, ones_ref, yg_ref, *scratch[N_RET_SCRATCH:])
    gla.pop(0)()
    for ret_phase, gla_phase in zip(ret, gla):
        ret_phase()
        gla_phase()


def _linear_mixers(r3, g3, la3, ret_tabs, gla_tabs):
    b, seq, _ = r3.shape
    tm = ROW_TILE
    qdec, kdec, dmask, cdec, bd, ones = ret_tabs
    tri, bdt, _ = gla_tabs
    fixed2 = lambda n, i: (0, 0)
    row = lambda n, i: (n, i, 0)
    y_shape = jax.ShapeDtypeStruct((b, seq, 256), BF16)
    return pl.pallas_call(
        _linear_mixers_kernel,
        grid=(b, seq // tm),
        in_specs=[
            pl.BlockSpec((None, tm, 768), row),
            pl.BlockSpec(qdec.shape, fixed2),
            pl.BlockSpec(kdec.shape, fixed2),
            pl.BlockSpec(dmask.shape, fixed2),
            pl.BlockSpec(cdec.shape, fixed2),
            pl.BlockSpec(bd.shape, fixed2),
            pl.BlockSpec(ones.shape, fixed2),
            pl.BlockSpec((None, tm, 768), row),
            pl.BlockSpec((None, tm, LANES), row),
            pl.BlockSpec(tri.shape, fixed2),
            pl.BlockSpec(bdt.shape, fixed2),
        ],
        out_specs=[pl.BlockSpec((None, tm, 256), row), pl.BlockSpec((None, tm, 256), row)],
        out_shape=[y_shape, y_shape],
        scratch_shapes=[
            pltpu.VMEM((128, 256), F32),
            pltpu.VMEM((tm, 128), BF16),
            pltpu.VMEM((tm, 256), F32),
            pltpu.VMEM((tm // RET_CHUNK, 128, 256), F32),
            pltpu.VMEM((tm // RET_CHUNK, 128, 256), BF16),
            pltpu.VMEM((256, 128), F32),
            pltpu.VMEM((tm, 128), F32),
            pltpu.VMEM((tm, 128), BF16),
            pltpu.VMEM((tm, 256), F32),
            pltpu.VMEM((tm // GLA_CHUNK, 256, 128), F32),
            pltpu.VMEM((tm // GLA_CHUNK, 256, 128), BF16),
        ],
        compiler_params=pltpu.CompilerParams(
            dimension_semantics=("arbitrary", "arbitrary"), vmem_limit_bytes=VMEM_LIMIT),
        name="linear_mixers",
    )(r3, qdec, kdec, dmask, cdec, bd, ones, g3, la3, tri, bdt)


def _mix_out_kernel(x_ref, o1_ref, o4_ref, o16_ref, l1_ref, l4_ref, l16_ref, yr_ref, yg_ref,
                    ms_ref, wout_ref, lng_ref, lnb_ref, ones_ref, out_ref, on_ref, ln_ref, y_ref):
    tm = x_ref.shape[0]
    n_slabs = WIDTH_A // LANES

    def to_sequence_order(part):
        for bi, (o_ref, l_ref, d) in enumerate(((o4_ref, l4_ref, DILATIONS[1]), (o16_ref, l16_ref, DILATIONS[2]))):
            n = PART_ROWS // d
            src = slice(part * n, (part + 1) * n)
            for r in range(d):
                dst = pl.ds(part * PART_ROWS + r, n, stride=d)
                ln_ref[bi, dst, :] = l_ref[r, src, :]
                for s in range(n_slabs):
                    on_ref[bi, s, dst, :] = o_ref[r, src, s * LANES:(s + 1) * LANES].astype(F32)

    low = _iota((SUB_ROWS, LANES), 1) < HEAD_DIM_A
    ms = ms_ref[...]

    def merge(rb):
        rows = slice(rb * SUB_ROWS, (rb + 1) * SUB_ROWS)
        l1 = l1_ref[rows, :]
        l4 = ln_ref[0, rows, :]
        l16 = ln_ref[1, rows, :]
        top = jnp.maximum(jnp.maximum(l1, l4), l16)
        e1 = jnp.exp(l1 - top)
        e4 = jnp.exp(l4 - top)
        e16 = jnp.exp(l16 - top)
        inv = 1.0 / (e1 + e4 + e16)
        w4 = e4 * inv
        w16 = e16 * inv
        for s in range(n_slabs):
            sl = slice(s * LANES, (s + 1) * LANES)

            def expand(w):
                c0 = HEAD_DIM_A + 16 * s
                c1 = 16 * s
                return jnp.where(low, w[:, c0:c0 + 1], w[:, c1:c1 + 1])

            x4 = expand(w4)
            x16 = expand(w16)
            merged = ((1.0 - x4 - x16) * o1_ref[rows, sl].astype(F32) + x4 * on_ref[0, s, rows, :]
                      + x16 * on_ref[1, s, rows, :])
            ya = _head_norm(merged, ones_ref[...], HEAD_DIM_A)
            y_ref[rows, sl] = (ya * ms[:, sl]).astype(BF16)
        y_ref[rows, 512:768] = (yr_ref[rows, :].astype(F32) * ms[:, 512:768]).astype(BF16)
        y_ref[rows, 768:1024] = (yg_ref[rows, :].astype(F32) * ms[:, 768:1024]).astype(BF16)

    def project(part):
        rows = slice(part * PART_ROWS, (part + 1) * PART_ROWS)
        return _dot(y_ref[rows, :], wout_ref[...])

    def finish(part, proj):
        rows = slice(part * PART_ROWS, (part + 1) * PART_ROWS)
        z = DEEPNORM_ALPHA * x_ref[rows, :] + proj
        out_ref[rows, :] = _layer_norm(z, lng_ref[...], lnb_ref[...])

    n_parts = tm // PART_ROWS
    proj = None
    for part in range(n_parts):
        to_sequence_order(part)
        for rb in range(part * (PART_ROWS // SUB_ROWS), (part + 1) * (PART_ROWS // SUB_ROWS)):
            merge(rb)
        if proj is not None:
            finish(part - 1, proj)
        proj = project(part)
    finish(n_parts - 1, proj)


def _mix_out(x3, o1, o4, o16, l1, l4, l16, yr, yg, ms, w_out, ln_g, ln_b, ones):
    b, seq, _ = x3.shape
    tm = ROW_TILE
    row = lambda n, i: (n, i, 0)
    cls = lambda n, i: (n, 0, i, 0)
    fixed = lambda n, i: (0, 0)
    d4, d16 = DILATIONS[1], DILATIONS[2]
    return pl.pallas_call(
        _mix_out_kernel,
        grid=(b, seq // tm),
        in_specs=[
            pl.BlockSpec((None, tm, D_MODEL), row),
            pl.BlockSpec((None, tm, WIDTH_A), row),
            pl.BlockSpec((None, d4, tm // d4, WIDTH_A), cls),
            pl.BlockSpec((None, d16, tm // d16, WIDTH_A), cls),
            pl.BlockSpec((None, tm, LANES), row),
            pl.BlockSpec((None, d4, tm // d4, LANES), cls),
            pl.BlockSpec((None, d16, tm // d16, LANES), cls),
            pl.BlockSpec((None, tm, 256), row),
            pl.BlockSpec((None, tm, 256), row),
            pl.BlockSpec(ms.shape, fixed),
            pl.BlockSpec(w_out.shape, fixed),
            pl.BlockSpec(ln_g.shape, fixed),
            pl.BlockSpec(ln_b.shape, fixed),
            pl.BlockSpec(ones.shape, fixed),
        ],
        out_specs=pl.BlockSpec((None, tm, D_MODEL), row),
        out_shape=jax.ShapeDtypeStruct((b, seq, D_MODEL), F32),
        scratch_shapes=[
            pltpu.VMEM((2, WIDTH_A // LANES, tm, LANES), F32),
            pltpu.VMEM((2, tm, LANES), F32),
            pltpu.VMEM((tm, D_MODEL), BF16),
        ],
        compiler_params=pltpu.CompilerParams(
            dimension_semantics=("arbitrary", "arbitrary"), vmem_limit_bytes=VMEM_LIMIT),
        name="mix_out",
    )(x3, o1, o4, o16, l1, l4, l16, yr, yg, ms, w_out, ln_g, ln_b, ones)


def _ffn_kernel(x_ref, wup_ref, cw_ref, wdn_ref, lng_ref, lnb_ref, out_ref, u_ref, carry_ref, acc_ref, xb_ref):
    assert N_FF_CHUNKS % 2 == 1 and N_FF_CHUNKS >= 3
    tm = x_ref.shape[0]
    halo = 8

    @pl.when(pl.program_id(1) == 0)
    def _():
        carry_ref[...] = jnp.zeros_like(carry_ref)

    xb_ref[...] = x_ref[...].astype(BF16)

    def produce(j, buf):
        u_ref[buf, 0:halo, :] = carry_ref[j]
        u_ref[buf, halo:halo + tm, :] = _dot(xb_ref[...], wup_ref[j])
        carry_ref[j] = u_ref[buf, tm:tm + halo, :]

    def consume(j, buf, first=False):
        cw = cw_ref[j]
        y = (u_ref[buf, halo:halo + tm, :] * cw[2:3, :] + u_ref[buf, halo - 1:halo - 1 + tm, :] * cw[1:2, :]
             + u_ref[buf, halo - 2:halo - 2 + tm, :] * cw[0:1, :] + cw[3:4, :])
        h = _silu(y[:, :FF_CHUNK]) * y[:, FF_CHUNK:]
        d = _dot(h.astype(BF16), wdn_ref[j])
        if first:
            acc_ref[...] = d
        else:
            acc_ref[...] += d

    produce(0, 0)
    produce(1, 1)
    consume(0, 0, first=True)

    def pair(p, _):
        j = 2 * p + 1
        produce(j + 1, 0)
        consume(j, 1)
        produce(j + 2, 1)
        consume(j + 1, 0)
        return 0

    lax.fori_loop(0, (N_FF_CHUNKS - 3) // 2, pair, 0)
    produce(N_FF_CHUNKS - 1, 0)
    consume(N_FF_CHUNKS - 2, 1)
    consume(N_FF_CHUNKS - 1, 0)
    z = DEEPNORM_ALPHA * x_ref[...] + acc_ref[...]
    out_ref[...] = _layer_norm(z, lng_ref[...], lnb_ref[...])


def _ffn(x3, wup_c, cw_c, wdn_c, ln_g, ln_b):
    b, seq, _ = x3.shape
    tm = FFN_TILE
    fixed3 = lambda n, i: (0, 0, 0)
    fixed2 = lambda n, i: (0, 0)
    return pl.pallas_call(
        _ffn_kernel,
        grid=(b, seq // tm),
        in_specs=[
            pl.BlockSpec((None, tm, D_MODEL), lambda n, i: (n, i, 0)),
            pl.BlockSpec(wup_c.shape, fixed3, pipeline_mode=pl.Buffered(1)),
            pl.BlockSpec(cw_c.shape, fixed3),
            pl.BlockSpec(wdn_c.shape, fixed3, pipeline_mode=pl.Buffered(1)),
            pl.BlockSpec(ln_g.shape, fixed2),
            pl.BlockSpec(ln_b.shape, fixed2),
        ],
        out_specs=pl.BlockSpec((None, tm, D_MODEL), lambda n, i: (n, i, 0)),
        out_shape=jax.ShapeDtypeStruct((b, seq, D_MODEL), F32),
        scratch_shapes=[
            pltpu.VMEM((2, tm + 8, 2 * FF_CHUNK), F32),
            pltpu.VMEM((N_FF_CHUNKS, 8, 2 * FF_CHUNK), F32),
            pltpu.VMEM((tm, D_MODEL), F32),
            pltpu.VMEM((tm, D_MODEL), BF16),
        ],
        compiler_params=pltpu.CompilerParams(
            dimension_semantics=("arbitrary", "arbitrary"), vmem_limit_bytes=VMEM_LIMIT),
        name="conv_glu_ffn",
    )(x3, wup_c, cw_c, wdn_c, ln_g, ln_b)


def _rope_tables(seq, dim):
    inv = 1.0 / (ROPE_THETA ** (jnp.arange(0, dim, 2, dtype=F32) / dim))
    ang = jnp.arange(seq, dtype=F32)[:, None] * inv[None, :]
    cos, sin = jnp.cos(ang), jnp.sin(ang)
    reps = LANES // dim
    cos_l = jnp.tile(jnp.concatenate([cos, cos], axis=1), (1, reps))
    sin_l = jnp.tile(jnp.concatenate([-sin, sin], axis=1), (1, reps))
    return cos_l, sin_l


def _block_diag_ones(n, group):
    idx = np.arange(n) // group
    return jnp.asarray(idx[:, None] == idx[None, :], BF16)


def _retention_tables():
    c = RET_CHUNK
    h = N_HEADS_R
    lg = jnp.log(1.0 - jnp.power(2.0, -5.0 - jnp.arange(h, dtype=F32)))
    idx = jnp.arange(c, dtype=F32)
    dist = idx[:, None] - idx[None, :]
    dmask = jnp.where(dist >= 0, jnp.exp(lg[:, None, None] * jnp.maximum(dist, 0.0)), 0.0)
    dmask_l = jnp.concatenate([dmask[i] for i in range(h)], axis=1)
    q_dec = jnp.exp(lg[:, None] * (idx + 1.0))
    k_dec = jnp.exp(lg[:, None] * (c - 1.0 - idx))
    chunk_dec = jnp.exp(lg * c)
    qdec_l = jnp.repeat(q_dec.T, QK_DIM_R, axis=1)
    kdec_l = jnp.repeat(k_dec.T, QK_DIM_R, axis=1)
    row_h = np.arange(128) // QK_DIM_R
    col_h = np.arange(256) // V_DIM_R
    bd = jnp.asarray(row_h[:, None] == col_h[None, :], F32)
    cdec = bd * chunk_dec[row_h][:, None]
    return qdec_l, kdec_l, dmask_l, cdec, bd, _block_diag_ones(256, V_DIM_R)


def _gla_tables():
    r = np.arange(CUM_BLOCK)
    tri = jnp.asarray((r[:, None] >= r[None, :]) & (r[:, None] // GLA_CHUNK == r[None, :] // GLA_CHUNK), BF16)
    row_h = np.arange(256) // V_DIM_G
    col_h = np.arange(128) // QK_DIM_G
    bdt = jnp.asarray(row_h[:, None] == col_h[None, :], F32)
    return tri, bdt, _block_diag_ones(256, V_DIM_G)


def kernel(x, w_in, w_alpha, b_alpha, mix_scale, w_out, ln1_g, ln1_b, w_up, conv_w, conv_b, w_down, ln2_g, ln2_b):
    b, seq, d_model = x.shape
    assert (d_model, w_in.shape[0]) == (D_MODEL, DEPTH)
    assert seq % max(ROW_TILE, FFN_TILE, INPROJ_TILE) == 0
    assert (seq // DILATIONS[2]) % ATT_BLK == 0 and seq // DILATIONS[2] >= 2 * ATT_BLK
    ca, sa = _rope_tables(seq, HEAD_DIM_A)
    cr, sr = _rope_tables(seq, QK_DIM_R)
    ret_tabs = _retention_tables()
    gla_tabs = _gla_tables()
    ones_a = _block_diag_ones(LANES, HEAD_DIM_A)
    pad_cols = 3 * WIDTH_A + 768 + 768 + LANES - w_in.shape[2]

    for l in range(DEPTH):
        w_all = jnp.pad(w_in[l], ((0, 0), (0, pad_cols))).astype(BF16)
        w_al = jnp.pad(w_alpha[l], ((0, LANES - GLA_LOW_RANK), (0, 0))).astype(BF16)
        (q1, k1, v1, q4, k4, v4, q16, k16, v16, r3, g3, la3) = _inproj(
            x, w_all, w_al, b_alpha[l][None, :], ca, sa, cr, sr)

        outs, lses = [], []
        for d, (q, k, v) in zip(DILATIONS, ((q1, k1, v1), (q4, k4, v4), (q16, k16, v16))):
            shp = (b * d, seq // d, WIDTH_A)
            o, lse = _attention(q.reshape(shp), k.reshape(shp), v.reshape(shp))
            if d > 1:
                o = o.reshape(b, d, seq // d, WIDTH_A)
                lse = lse.reshape(b, d, seq // d, LANES)
            outs.append(o)
            lses.append(lse)

        yr, yg = _linear_mixers(r3, g3, la3, ret_tabs, gla_tabs)

        x = _mix_out(x, outs[0], outs[1], outs[2], lses[0], lses[1], lses[2], yr, yg,
                     mix_scale[l][None, :], w_out[l].astype(BF16), ln1_g[l][None, :], ln1_b[l][None, :], ones_a)

        wu = w_up[l].astype(BF16)
        wup_c = jnp.concatenate(
            [wu[:, :D_FF].reshape(D_MODEL, N_FF_CHUNKS, FF_CHUNK), wu[:, D_FF:].reshape(D_MODEL, N_FF_CHUNKS, FF_CHUNK)],
            axis=2).transpose(1, 0, 2)
        taps = jnp.concatenate([conv_w[l], conv_b[l][None, :], jnp.zeros((4, 2 * D_FF), F32)], axis=0)
        cw_c = jnp.concatenate(
            [taps[:, :D_FF].reshape(8, N_FF_CHUNKS, FF_CHUNK), taps[:, D_FF:].reshape(8, N_FF_CHUNKS, FF_CHUNK)],
            axis=2).transpose(1, 0, 2)
        wdn_c = w_down[l].astype(BF16).reshape(N_FF_CHUNKS, FF_CHUNK, D_MODEL)
        x = _ffn(x, wup_c, cw_c, wdn_c, ln2_g[l][None, :], ln2_b[l][None, :])
    return x
```

```python
import jax
import jax.numpy as jnp
import numpy as np
from jax import lax
from jax.experimental import pallas as pl
from jax.experimental.pallas import tpu as pltpu

F32 = jnp.float32
BF16 = jnp.bfloat16

D_MODEL = 1024
DEPTH = 4
HEAD_DIM_A = 64
WIDTH_A = 512
DILATIONS = (1, 4, 16)
ATT_BLK = 128
ATT_STEP_ROWS = 1024
ROPE_THETA = 10000.0
N_HEADS_R = 4
QK_DIM_R = 32
V_DIM_R = 64
RET_CHUNK = 128
N_HEADS_G = 4
QK_DIM_G = 32
V_DIM_G = 64
GLA_LOW_RANK = 16
GLA_TAU = 16.0
GLA_CHUNK = 64
CUM_BLOCK = 256
D_FF = 2816
FF_CHUNK = 256
N_FF_CHUNKS = D_FF // FF_CHUNK
DEEPNORM_ALPHA = (2 * DEPTH) ** 0.25
LOG2E = 1.4426950408889634
LN2 = 0.6931471805599453
LN_EPS = 1e-5
HEAD_NORM_EPS = 1e-6
LANES = 128
ROW_TILE = 1024
FFN_TILE = 1024
INPROJ_TILE = 1024
SUB_ROWS = 128
PART_ROWS = 256
VMEM_LIMIT = 56 * 1024 * 1024


def _iota(shape, dim):
    return lax.broadcasted_iota(jnp.int32, shape, dim)


def _dot(a, b):
    return jnp.dot(a, b, preferred_element_type=F32)


def _dot_nt(a, b):
    return lax.dot_general(a, b, (((1,), (1,)), ((), ())), preferred_element_type=F32)


def _dot_tn(a, b):
    return lax.dot_general(a, b, (((0,), (0,)), ((), ())), preferred_element_type=F32)


def _split_hi_lo(x):
    hi = x.astype(BF16)
    lo = (x - hi.astype(F32)).astype(BF16)
    return hi, lo


def _group_mean(x, ones_bd, group):
    hi, lo = _split_hi_lo(x)
    return (_dot(hi, ones_bd) + _dot(lo, ones_bd)) * (1.0 / group)


def _head_norm(x, ones_bd, group):
    mu = _group_mean(x, ones_bd, group)
    d = x - mu
    var = _group_mean(d * d, ones_bd, group)
    return d * lax.rsqrt(var + HEAD_NORM_EPS)


def _silu(g):
    return g * (1.0 / (1.0 + jnp.exp(-g)))


def _layer_norm(z, g, b):
    mu = jnp.mean(z, axis=-1, keepdims=True)
    d = z - mu
    var = jnp.mean(d * d, axis=-1, keepdims=True)
    return d * lax.rsqrt(var + LN_EPS) * g + b


def _head_stack(t, lane_head, n_heads):
    return jnp.concatenate([jnp.where(lane_head == h, t, jnp.zeros_like(t)) for h in range(n_heads)], axis=0)


def _rope(t, cos, sin_signed, half):
    lane = _iota(t.shape, 1)
    first = (lane % (2 * half)) < half
    partner = jnp.where(first, pltpu.roll(t, LANES - half, 1), pltpu.roll(t, half, 1))
    return t * cos + partner * sin_signed


def _inproj_kernel(x_ref, w_ref, walpha_ref, balpha_ref, ca_ref, sa_ref, cr_ref, sr_ref,
                   qkv1_ref, qkv4_ref, qkv16_ref,
                   r_ref, g_ref, la_ref, slab_ref, cls_ref):
    tm = x_ref.shape[0]
    n_slabs = WIDTH_A // LANES
    xb = x_ref[...].astype(BF16)

    def mm(c0, n):
        return _dot(xb, w_ref[:, c0:c0 + n])

    outs = ((qkv1_ref, qkv4_ref, qkv16_ref),) * 3
    base = 3 * WIDTH_A

    def finish_a(ti, t):
        for s in range(n_slabs):
            sl = slice(s * LANES, (s + 1) * LANES)
            slab = t[:, sl]
            if ti < 2:
                slab = _rope(slab, ca_ref[...], sa_ref[...], HEAD_DIM_A // 2)
            if ti == 0:
                slab = slab * (LOG2E * HEAD_DIM_A ** -0.5)
            slab_ref[ti * n_slabs + s] = slab
            outs[ti][0][:, ti * WIDTH_A + s * LANES:ti * WIDTH_A + (s + 1) * LANES] = slab.astype(BF16)
        d4, d16 = DILATIONS[1], DILATIONS[2]
        n4 = tm // d4
        for s in range(n_slabs):
            sl = slice(ti * WIDTH_A + s * LANES, ti * WIDTH_A + (s + 1) * LANES)
            for r4 in range(d4):
                cls4 = slab_ref[ti * n_slabs + s, pl.ds(r4, n4, stride=d4), :]
                outs[ti][1][r4, :, sl] = cls4.astype(BF16)
                cls_ref[s, r4 * n4:(r4 + 1) * n4, :] = cls4
            for r4 in range(d4):
                for c in range(d16 // d4):
                    cls16 = cls_ref[s, pl.ds(r4 * n4 + c, tm // d16, stride=d16 // d4), :]
                    outs[ti][2][r4 + d4 * c, :, sl] = cls16.astype(BF16)

    def finish_r(rr):
        cr = cr_ref[...]
        sr = sr_ref[...]
        r_ref[:, 0:128] = _rope(rr[:, 0:128], cr, sr, QK_DIM_R // 2).astype(BF16)
        r_ref[:, 128:256] = (_rope(rr[:, 128:256], cr, sr, QK_DIM_R // 2) * (QK_DIM_R ** -0.5)).astype(BF16)
        r_ref[:, 256:768] = rr[:, 256:768].astype(BF16)

    def finish_g(gg):
        g_ref[:, 0:128] = (gg[:, 0:128] * (QK_DIM_G ** -0.5)).astype(BF16)
        g_ref[:, 128:768] = gg[:, 128:768].astype(BF16)
        ag = gg[:, 768:896].astype(BF16)
        z = _dot(ag, walpha_ref[...]) + balpha_ref[...]
        log_sig = jnp.minimum(z, 0.0) - jnp.log(1.0 + jnp.exp(-jnp.abs(z)))
        la_ref[...] = log_sig * (1.0 / GLA_TAU)

    stages = [
        (lambda: mm(2 * WIDTH_A, WIDTH_A), lambda t: finish_a(2, t)),
        (lambda: mm(0, WIDTH_A), lambda t: finish_a(0, t)),
        (lambda: mm(base, 768), finish_r),
        (lambda: mm(WIDTH_A, WIDTH_A), lambda t: finish_a(1, t)),
        (lambda: mm(base + 768, 896), finish_g),
    ]
    prev = None
    for matmul, finish in stages:
        t = matmul()
        if prev is not None:
            prev[0](prev[1])
        prev = (finish, t)
    prev[0](prev[1])


def _inproj(x3, w_all, w_alpha, b_alpha, ca, sa, cr, sr):
    b, seq, _ = x3.shape
    tm = INPROJ_TILE
    row = lambda n, i: (n, i, 0)
    cls = lambda n, i: (n, 0, i, 0)
    fixed = lambda n, i: (0, 0)
    tab = lambda n, i: (i, 0)
    d4, d16 = DILATIONS[1], DILATIONS[2]
    nat = jax.ShapeDtypeStruct((b, seq, 3 * WIDTH_A), BF16)
    c4 = jax.ShapeDtypeStruct((b, d4, seq // d4, 3 * WIDTH_A), BF16)
    c16 = jax.ShapeDtypeStruct((b, d16, seq // d16, 3 * WIDTH_A), BF16)
    nat_spec = pl.BlockSpec((None, tm, 3 * WIDTH_A), row)
    c4_spec = pl.BlockSpec((None, d4, tm // d4, 3 * WIDTH_A), cls)
    c16_spec = pl.BlockSpec((None, d16, tm // d16, 3 * WIDTH_A), cls)
    return pl.pallas_call(
        _inproj_kernel,
        grid=(b, seq // tm),
        in_specs=[
            pl.BlockSpec((None, tm, D_MODEL), row),
            pl.BlockSpec(w_all.shape, fixed, pipeline_mode=pl.Buffered(1)),
            pl.BlockSpec(w_alpha.shape, fixed),
            pl.BlockSpec(b_alpha.shape, fixed),
            pl.BlockSpec((tm, LANES), tab),
            pl.BlockSpec((tm, LANES), tab),
            pl.BlockSpec((tm, LANES), tab),
            pl.BlockSpec((tm, LANES), tab),
        ],
        out_specs=[nat_spec, c4_spec, c16_spec] + [
            pl.BlockSpec((None, tm, 768), row),
            pl.BlockSpec((None, tm, 768), row),
            pl.BlockSpec((None, tm, LANES), row),
        ],
        out_shape=[nat, c4, c16] + [
            jax.ShapeDtypeStruct((b, seq, 768), BF16),
            jax.ShapeDtypeStruct((b, seq, 768), BF16),
            jax.ShapeDtypeStruct((b, seq, LANES), F32),
        ],
        scratch_shapes=[
            pltpu.VMEM((3 * (WIDTH_A // LANES), tm, LANES), F32),
            pltpu.VMEM((WIDTH_A // LANES, tm, LANES), F32),
        ],
        compiler_params=pltpu.CompilerParams(
            dimension_semantics=("arbitrary", "arbitrary"), vmem_limit_bytes=VMEM_LIMIT),
        name="inproj",
    )(x3, w_all, w_alpha, b_alpha, ca, sa, cr, sr)


def _attn_kernel(q_ref, k_ref, v_ref, o_ref, lse_ref):
    blk = ATT_BLK
    n_cls, tq, _ = q_ref.shape
    nsub = tq // blk
    n_slabs = WIDTH_A // LANES
    i = pl.program_id(1)

    qi = _iota((blk, 2 * blk), 0)
    kj = _iota((blk, 2 * blk), 1)
    band = (kj >= qi) & (kj <= qi + blk)
    kj0 = kj + jnp.where(i == 0, blk, 0)
    first = (kj0 >= qi) & (kj0 <= qi + blk)
    lane = _iota((blk, LANES), 1)
    low_q = lane < HEAD_DIM_A
    low_v = _iota((2 * blk, LANES), 1) < HEAD_DIM_A

    units = [(g, jb, s, hh) for g in range(n_cls) for jb in range(nsub) for s in range(n_slabs) for hh in range(2)]

    def key_rows(jb):
        if jb == 0:
            start = jnp.maximum(i * tq - blk, 0)
        else:
            start = i * tq + (jb - 1) * blk
        return pl.ds(pl.multiple_of(start, blk), 2 * blk)

    def scores(g, jb, s, hh):
        sl = slice(s * LANES, (s + 1) * LANES)
        qs = q_ref[g, jb * blk:(jb + 1) * blk, sl]
        sel_q = low_q if hh == 0 else jnp.logical_not(low_q)
        qm = jnp.where(sel_q, qs, jnp.zeros_like(qs))
        return _dot_nt(qm, k_ref[g, key_rows(jb), sl])

    def weighted(g, jb, s, hh, sc):
        sl = slice(s * LANES, (s + 1) * LANES)
        sc = jnp.where(first if jb == 0 else band, sc, -jnp.inf)
        m = jnp.max(sc, axis=-1, keepdims=True)
        p = jnp.exp2(sc - m).astype(BF16)
        vs = v_ref[g, key_rows(jb), sl]
        sel_v = low_v if hh == 0 else jnp.logical_not(low_v)
        vm = jnp.where(sel_v, vs, jnp.ones_like(vs))
        r = _dot(p, vm)
        return r, m

    skew = 5
    pending = {}
    done = {}
    den_parts = {}
    max_parts = {}
    grp = (lane % HEAD_DIM_A) // (HEAD_DIM_A // n_slabs)

    def gather_heads(parts):
        tile = parts[n_slabs - 1]
        for s2 in range(n_slabs - 1):
            tile = jnp.where(grp == s2, parts[s2], tile)
        return tile

    for t in range(len(units) + skew):
        if t < len(units):
            pending[t] = scores(*units[t])
        if t >= skew:
            g, jb, s, hh = units[t - skew]
            done[hh] = weighted(g, jb, s, hh, pending.pop(t - skew))
            if hh == 1:
                (r0, m0), (r1, m1) = done[0], done[1]
                num = jnp.where(low_q, r0, r1)
                den_parts[s] = jnp.where(low_q, r1, r0)
                max_parts[s] = jnp.where(low_q, m1, m0)
                den = pltpu.roll(den_parts[s], HEAD_DIM_A, 1)
                o_ref[g, jb * blk:(jb + 1) * blk, s * LANES:(s + 1) * LANES] = (num / den).astype(BF16)
                if s == n_slabs - 1:
                    lse2 = gather_heads(max_parts) + jnp.log2(gather_heads(den_parts))
                    lse_ref[g, jb * blk:(jb + 1) * blk, :] = lse2 * LN2


def _attention(qkv):
    nb, length, _ = qkv.shape
    blk = min(ATT_STEP_ROWS, length)
    n_cls = ATT_STEP_ROWS // blk
    return pl.pallas_call(
        _attn_kernel,
        grid=(nb // n_cls, length // blk),
        in_specs=[
            pl.BlockSpec((n_cls, blk, WIDTH_A), lambda n, i: (n, i, 0)),
            pl.BlockSpec((n_cls, length, WIDTH_A), lambda n, i: (n, 0, 1)),
            pl.BlockSpec((n_cls, length, WIDTH_A), lambda n, i: (n, 0, 2)),
        ],
        out_specs=[
            pl.BlockSpec((n_cls, blk, WIDTH_A), lambda n, i: (n, i, 0)),
            pl.BlockSpec((n_cls, blk, LANES), lambda n, i: (n, i, 0)),
        ],
        out_shape=[
            jax.ShapeDtypeStruct((nb, length, WIDTH_A), BF16),
            jax.ShapeDtypeStruct((nb, length, LANES), F32),
        ],
        compiler_params=pltpu.CompilerParams(
            dimension_semantics=("arbitrary", "arbitrary"), vmem_limit_bytes=VMEM_LIMIT),
        name="dilated_attn",
    )(qkv, qkv, qkv)


def _retention_phases(r_ref, qdec_ref, kdec_ref, dmask_ref, cdec_ref, bd_ref, ones_ref, y_ref,
                      state_ref, qd_ref, o_ref, kv_ref, stb_ref):
    c = RET_CHUNK
    n_chunks = r_ref.shape[0] // c

    @pl.when(pl.program_id(1) == 0)
    def _():
        state_ref[...] = jnp.zeros_like(state_ref)

    lane_q = _iota((c, 128), 1) // QK_DIM_R
    lane_v = _iota((c, 256), 1) // V_DIM_R
    chunk_rows = [slice(ci * c, (ci + 1) * c) for ci in range(n_chunks)]
    sc = {}

    def scores():
        bd = bd_ref[...] > 0
        for ci, rows in enumerate(chunk_rows):
            q = r_ref[rows, 0:128]
            k = r_ref[rows, 128:256]
            sc[ci] = _dot_nt(q, _head_stack(k, lane_q, N_HEADS_R))
            qd_ref[rows, :] = (q.astype(F32) * qdec_ref[...]).astype(BF16)
            kd = (k.astype(F32) * kdec_ref[...]).astype(BF16)
            kv_ref[ci] = jnp.where(bd, _dot_tn(kd, r_ref[rows, 256:512]), 0.0)

    def values():
        for ci, rows in enumerate(chunk_rows):
            p = (sc[ci] * dmask_ref[...]).astype(BF16)
            o_ref[rows, :] = _dot(p, _head_stack(r_ref[rows, 256:512], lane_v, N_HEADS_R))

    def scan():
        st = state_ref[...]
        for ci in range(n_chunks):
            stb_ref[ci] = st.astype(BF16)
            st = cdec_ref[...] * st + kv_ref[ci]
        state_ref[...] = st

    def inter():
        for ci, rows in enumerate(chunk_rows):
            o_ref[rows, :] += _dot(qd_ref[rows, :], stb_ref[ci])

    norm = {}

    def center():
        o = o_ref[...]
        norm["dev"] = o - _group_mean(o, ones_ref[...], V_DIM_R)

    def spread():
        norm["var"] = _group_mean(norm["dev"] * norm["dev"], ones_ref[...], V_DIM_R)

    def finish():
        y = norm["dev"] * lax.rsqrt(norm["var"] + HEAD_NORM_EPS) * _silu(r_ref[:, 512:768].astype(F32))
        y_ref[...] = y.astype(BF16)

    return [scores, values, scan, inter, center, spread, finish]


def _gla_phases(g_ref, la_ref, tri_ref, bdt_ref, ones_ref, y_ref,
                state_ref, cum_ref, qt_ref, o_ref, kv_ref, stb_ref):
    c = GLA_CHUNK
    tm = g_ref.shape[0]
    n_chunks = tm // c

    @pl.when(pl.program_id(1) == 0)
    def _():
        state_ref[...] = jnp.zeros_like(state_ref)

    lane_q = _iota((c, 128), 1) // QK_DIM_G
    lane_v = _iota((c, 256), 1) // V_DIM_G
    chunk_rows = [slice(ci * c, (ci + 1) * c) for ci in range(n_chunks)]
    sc = {}
    decays = {}

    def cumsum():
        for part in range(tm // CUM_BLOCK):
            rows = slice(part * CUM_BLOCK, (part + 1) * CUM_BLOCK)
            la_hi, la_lo = _split_hi_lo(la_ref[rows, :])
            cum_ref[rows, :] = _dot(tri_ref[...], la_hi) + _dot(tri_ref[...], la_lo)

    def scores():
        bdt = bdt_ref[...] > 0
        for ci, rows in enumerate(chunk_rows):
            q = g_ref[rows, 0:128].astype(F32)
            k = g_ref[rows, 128:256].astype(F32)
            cum = cum_ref[rows, :]
            last = cum[c - 1:c, :]
            q_t = (q * jnp.exp(cum)).astype(BF16)
            k_t = (k * jnp.exp(-cum)).astype(BF16)
            k_l = (k * jnp.exp(last - cum)).astype(BF16)
            qt_ref[rows, :] = q_t
            sc[ci] = _dot_nt(q_t, _head_stack(k_t, lane_q, N_HEADS_G))
            kv_ref[ci] = jnp.where(bdt, _dot_tn(g_ref[rows, 256:512], k_l), 0.0)
            decays[ci] = jnp.exp(last)

    def values():
        causal = _iota((c, N_HEADS_G * c), 0) >= (_iota((c, N_HEADS_G * c), 1) % c)
        for ci, rows in enumerate(chunk_rows):
            att = jnp.where(causal, sc[ci], 0.0).astype(BF16)
            o_ref[rows, :] = _dot(att, _head_stack(g_ref[rows, 256:512], lane_v, N_HEADS_G))

    def scan():
        st = state_ref[...]
        for ci in range(n_chunks):
            stb_ref[ci] = st.astype(BF16)
            st = decays[ci] * st + kv_ref[ci]
        state_ref[...] = st

    def inter():
        for ci, rows in enumerate(chunk_rows):
            o_ref[rows, :] += _dot_nt(qt_ref[rows, :], stb_ref[ci])

    norm = {}

    def center():
        o = o_ref[...]
        norm["dev"] = o - _group_mean(o, ones_ref[...], V_DIM_G)

    def spread():
        norm["var"] = _group_mean(norm["dev"] * norm["dev"], ones_ref[...], V_DIM_G)

    def finish():
        y = norm["dev"] * lax.rsqrt(norm["var"] + HEAD_NORM_EPS) * _silu(g_ref[:, 512:768].astype(F32))
        y_ref[...] = y.astype(BF16)

    return [cumsum, scores, values, scan, inter, center, spread, finish]


N_RET_SCRATCH = 5


def _linear_mixers_kernel(r_ref, qdec_ref, kdec_ref, dmask_ref, cdec_ref, bd_ref, ones_ref,
                          g_ref, la_ref, tri_ref, bdt_ref, yr_ref, yg_ref, *scratch):
    ret = _retention_phases(r_ref, qdec_ref, kdec_ref, dmask_ref, cdec_ref, bd_ref, ones_ref, yr_ref,
                            *scratch[:N_RET_SCRATCH])
    gla = _gla_phases(g_ref, la_ref, tri_ref, bdt_ref, ones_ref, yg_ref, *scratch[N_RET_SCRATCH:])
    gla.pop(0)()
    for ret_phase, gla_phase in zip(ret, gla):
        ret_phase()
        gla_phase()


def _linear_mixers(r3, g3, la3, ret_tabs, gla_tabs):
    b, seq, _ = r3.shape
    tm = ROW_TILE
    qdec, kdec, dmask, cdec, bd, ones = ret_tabs
    tri, bdt, _ = gla_tabs
    fixed2 = lambda n, i: (0, 0)
    row = lambda n, i: (n, i, 0)
    y_shape = jax.ShapeDtypeStruct((b, seq, 256), BF16)
    return pl.pallas_call(
        _linear_mixers_kernel,
        grid=(b, seq // tm),
        in_specs=[
            pl.BlockSpec((None, tm, 768), row),
            pl.BlockSpec(qdec.shape, fixed2),
            pl.BlockSpec(kdec.shape, fixed2),
            pl.BlockSpec(dmask.shape, fixed2),
            pl.BlockSpec(cdec.shape, fixed2),
            pl.BlockSpec(bd.shape, fixed2),
            pl.BlockSpec(ones.shape, fixed2),
            pl.BlockSpec((None, tm, 768), row),
            pl.BlockSpec((None, tm, LANES), row),
            pl.BlockSpec(tri.shape, fixed2),
            pl.BlockSpec(bdt.shape, fixed2),
        ],
        out_specs=[pl.BlockSpec((None, tm, 256), row), pl.BlockSpec((None, tm, 256), row)],
        out_shape=[y_shape, y_shape],
        scratch_shapes=[
            pltpu.VMEM((128, 256), F32),
            pltpu.VMEM((tm, 128), BF16),
            pltpu.VMEM((tm, 256), F32),
            pltpu.VMEM((tm // RET_CHUNK, 128, 256), F32),
            pltpu.VMEM((tm // RET_CHUNK, 128, 256), BF16),
            pltpu.VMEM((256, 128), F32),
            pltpu.VMEM((tm, 128), F32),
            pltpu.VMEM((tm, 128), BF16),
            pltpu.VMEM((tm, 256), F32),
            pltpu.VMEM((tm // GLA_CHUNK, 256, 128), F32),
            pltpu.VMEM((tm // GLA_CHUNK, 256, 128), BF16),
        ],
        compiler_params=pltpu.CompilerParams(
            dimension_semantics=("arbitrary", "arbitrary"), vmem_limit_bytes=VMEM_LIMIT),
        name="linear_mixers",
    )(r3, qdec, kdec, dmask, cdec, bd, ones, g3, la3, tri, bdt)


def _mix_out_kernel(x_ref, o1_ref, o4_ref, o16_ref, l1_ref, l4_ref, l16_ref, yr_ref, yg_ref,
                    ms_ref, wout_ref, lng_ref, lnb_ref, ones_ref, out_ref, on_ref, ln_ref, y_ref):
    tm = x_ref.shape[0]
    n_slabs = WIDTH_A // LANES

    def to_sequence_order(part):
        for bi, (o_ref, l_ref, d) in enumerate(((o4_ref, l4_ref, DILATIONS[1]), (o16_ref, l16_ref, DILATIONS[2]))):
            n = PART_ROWS // d
            src = slice(part * n, (part + 1) * n)
            for r in range(d):
                dst = pl.ds(part * PART_ROWS + r, n, stride=d)
                ln_ref[bi, dst, :] = l_ref[r, src, :]
                for s in range(n_slabs):
                    on_ref[bi, s, dst, :] = o_ref[r, src, s * LANES:(s + 1) * LANES].astype(F32)

    low = _iota((SUB_ROWS, LANES), 1) < HEAD_DIM_A
    ms = ms_ref[...]

    def merge(rb):
        rows = slice(rb * SUB_ROWS, (rb + 1) * SUB_ROWS)
        l1 = l1_ref[rows, :]
        l4 = ln_ref[0, rows, :]
        l16 = ln_ref[1, rows, :]
        top = jnp.maximum(jnp.maximum(l1, l4), l16)
        e1 = jnp.exp(l1 - top)
        e4 = jnp.exp(l4 - top)
        e16 = jnp.exp(l16 - top)
        inv = 1.0 / (e1 + e4 + e16)
        w4 = e4 * inv
        w16 = e16 * inv
        for s in range(n_slabs):
            sl = slice(s * LANES, (s + 1) * LANES)

            def expand(w):
                c0 = HEAD_DIM_A + 16 * s
                c1 = 16 * s
                return jnp.where(low, w[:, c0:c0 + 1], w[:, c1:c1 + 1])

            x4 = expand(w4)
            x16 = expand(w16)
            merged = ((1.0 - x4 - x16) * o1_ref[rows, sl].astype(F32) + x4 * on_ref[0, s, rows, :]
                      + x16 * on_ref[1, s, rows, :])
            ya = _head_norm(merged, ones_ref[...], HEAD_DIM_A)
            y_ref[rows, sl] = (ya * ms[:, sl]).astype(BF16)
        y_ref[rows, 512:768] = (yr_ref[rows, :].astype(F32) * ms[:, 512:768]).astype(BF16)
        y_ref[rows, 768:1024] = (yg_ref[rows, :].astype(F32) * ms[:, 768:1024]).astype(BF16)

    def project(part):
        rows = slice(part * PART_ROWS, (part + 1) * PART_ROWS)
        return _dot(y_ref[rows, :], wout_ref[...])

    def finish(part, proj):
        rows = slice(part * PART_ROWS, (part + 1) * PART_ROWS)
        z = DEEPNORM_ALPHA * x_ref[rows, :] + proj
        out_ref[rows, :] = _layer_norm(z, lng_ref[...], lnb_ref[...])

    n_parts = tm // PART_ROWS
    proj = None
    for part in range(n_parts):
        to_sequence_order(part)
        for rb in range(part * (PART_ROWS // SUB_ROWS), (part + 1) * (PART_ROWS // SUB_ROWS)):
            merge(rb)
        if proj is not None:
            finish(part - 1, proj)
        proj = project(part)
    finish(n_parts - 1, proj)


def _mix_out(x3, o1, o4, o16, l1, l4, l16, yr, yg, ms, w_out, ln_g, ln_b, ones):
    b, seq, _ = x3.shape
    tm = ROW_TILE
    row = lambda n, i: (n, i, 0)
    cls = lambda n, i: (n, 0, i, 0)
    fixed = lambda n, i: (0, 0)
    d4, d16 = DILATIONS[1], DILATIONS[2]
    return pl.pallas_call(
        _mix_out_kernel,
        grid=(b, seq // tm),
        in_specs=[
            pl.BlockSpec((None, tm, D_MODEL), row),
            pl.BlockSpec((None, tm, WIDTH_A), row),
            pl.BlockSpec((None, d4, tm // d4, WIDTH_A), cls),
            pl.BlockSpec((None, d16, tm // d16, WIDTH_A), cls),
            pl.BlockSpec((None, tm, LANES), row),
            pl.BlockSpec((None, d4, tm // d4, LANES), cls),
            pl.BlockSpec((None, d16, tm // d16, LANES), cls),
            pl.BlockSpec((None, tm, 256), row),
            pl.BlockSpec((None, tm, 256), row),
            pl.BlockSpec(ms.shape, fixed),
            pl.BlockSpec(w_out.shape, fixed),
            pl.BlockSpec(ln_g.shape, fixed),
            pl.BlockSpec(ln_b.shape, fixed),
            pl.BlockSpec(ones.shape, fixed),
        ],
        out_specs=pl.BlockSpec((None, tm, D_MODEL), row),
        out_shape=jax.ShapeDtypeStruct((b, seq, D_MODEL), F32),
        scratch_shapes=[
            pltpu.VMEM((2, WIDTH_A // LANES, tm, LANES), F32),
            pltpu.VMEM((2, tm, LANES), F32),
            pltpu.VMEM((tm, D_MODEL), BF16),
        ],
        compiler_params=pltpu.CompilerParams(
            dimension_semantics=("arbitrary", "arbitrary"), vmem_limit_bytes=VMEM_LIMIT),
        name="mix_out",
    )(x3, o1, o4, o16, l1, l4, l16, yr, yg, ms, w_out, ln_g, ln_b, ones)


def _ffn_kernel(x_ref, wup_ref, cw_ref, wdn_ref, lng_ref, lnb_ref, out_ref, u_ref, carry_ref, acc_ref, xb_ref):
    assert N_FF_CHUNKS % 2 == 1 and N_FF_CHUNKS >= 3
    tm = x_ref.shape[0]
    halo = 8

    @pl.when(pl.program_id(1) == 0)
    def _():
        carry_ref[...] = jnp.zeros_like(carry_ref)

    xb_ref[...] = x_ref[...].astype(BF16)

    def produce(j, buf):
        u_ref[buf, 0:halo, :] = carry_ref[j]
        u_ref[buf, halo:halo + tm, :] = _dot(xb_ref[...], wup_ref[j])
        carry_ref[j] = u_ref[buf, tm:tm + halo, :]

    def consume(j, buf, first=False):
        cw = cw_ref[j]
        y = (u_ref[buf, halo:halo + tm, :] * cw[2:3, :] + u_ref[buf, halo - 1:halo - 1 + tm, :] * cw[1:2, :]
             + u_ref[buf, halo - 2:halo - 2 + tm, :] * cw[0:1, :] + cw[3:4, :])
        h = _silu(y[:, :FF_CHUNK]) * y[:, FF_CHUNK:]
        d = _dot(h.astype(BF16), wdn_ref[j])
        if first:
            acc_ref[...] = d
        else:
            acc_ref[...] += d

    produce(0, 0)
    produce(1, 1)
    consume(0, 0, first=True)

    def pair(p, _):
        j = 2 * p + 1
        produce(j + 1, 0)
        consume(j, 1)
        produce(j + 2, 1)
        consume(j + 1, 0)
        return 0

    lax.fori_loop(0, (N_FF_CHUNKS - 3) // 2, pair, 0)
    produce(N_FF_CHUNKS - 1, 0)
    consume(N_FF_CHUNKS - 2, 1)
    consume(N_FF_CHUNKS - 1, 0)
    z = DEEPNORM_ALPHA * x_ref[...] + acc_ref[...]
    out_ref[...] = _layer_norm(z, lng_ref[...], lnb_ref[...])


def _ffn(x3, wup_c, cw_c, wdn_c, ln_g, ln_b):
    b, seq, _ = x3.shape
    tm = FFN_TILE
    fixed3 = lambda n, i: (0, 0, 0)
    fixed2 = lambda n, i: (0, 0)
    return pl.pallas_call(
        _ffn_kernel,
        grid=(b, seq // tm),
        in_specs=[
            pl.BlockSpec((None, tm, D_MODEL), lambda n, i: (n, i, 0)),
            pl.BlockSpec(wup_c.shape, fixed3, pipeline_mode=pl.Buffered(1)),
            pl.BlockSpec(cw_c.shape, fixed3),
            pl.BlockSpec(wdn_c.shape, fixed3, pipeline_mode=pl.Buffered(1)),
            pl.BlockSpec(ln_g.shape, fixed2),
            pl.BlockSpec(ln_b.shape, fixed2),
        ],
        out_specs=pl.BlockSpec((None, tm, D_MODEL), lambda n, i: (n, i, 0)),
        out_shape=jax.ShapeDtypeStruct((b, seq, D_MODEL), F32),
        scratch_shapes=[
            pltpu.VMEM((2, tm + 8, 2 * FF_CHUNK), F32),
            pltpu.VMEM((N_FF_CHUNKS, 8, 2 * FF_CHUNK), F32),
            pltpu.VMEM((tm, D_MODEL), F32),
            pltpu.VMEM((tm, D_MODEL), BF16),
        ],
        compiler_params=pltpu.CompilerParams(
            dimension_semantics=("arbitrary", "arbitrary"), vmem_limit_bytes=VMEM_LIMIT),
        name="conv_glu_ffn",
    )(x3, wup_c, cw_c, wdn_c, ln_g, ln_b)


def _rope_tables(seq, dim):
    inv = 1.0 / (ROPE_THETA ** (jnp.arange(0, dim, 2, dtype=F32) / dim))
    ang = jnp.arange(seq, dtype=F32)[:, None] * inv[None, :]
    cos, sin = jnp.cos(ang), jnp.sin(ang)
    reps = LANES // dim
    cos_l = jnp.tile(jnp.concatenate([cos, cos], axis=1), (1, reps))
    sin_l = jnp.tile(jnp.concatenate([-sin, sin], axis=1), (1, reps))
    return cos_l, sin_l


def _block_diag_ones(n, group):
    idx = np.arange(n) // group
    return jnp.asarray(idx[:, None] == idx[None, :], BF16)


def _retention_tables():
    c = RET_CHUNK
    h = N_HEADS_R
    lg = jnp.log(1.0 - jnp.power(2.0, -5.0 - jnp.arange(h, dtype=F32)))
    idx = jnp.arange(c, dtype=F32)
    dist = idx[:, None] - idx[None, :]
    dmask = jnp.where(dist >= 0, jnp.exp(lg[:, None, None] * jnp.maximum(dist, 0.0)), 0.0)
    dmask_l = jnp.concatenate([dmask[i] for i in range(h)], axis=1)
    q_dec = jnp.exp(lg[:, None] * (idx + 1.0))
    k_dec = jnp.exp(lg[:, None] * (c - 1.0 - idx))
    chunk_dec = jnp.exp(lg * c)
    qdec_l = jnp.repeat(q_dec.T, QK_DIM_R, axis=1)
    kdec_l = jnp.repeat(k_dec.T, QK_DIM_R, axis=1)
    row_h = np.arange(128) // QK_DIM_R
    col_h = np.arange(256) // V_DIM_R
    bd = jnp.asarray(row_h[:, None] == col_h[None, :], F32)
    cdec = bd * chunk_dec[row_h][:, None]
    return qdec_l, kdec_l, dmask_l, cdec, bd, _block_diag_ones(256, V_DIM_R)


def _gla_tables():
    r = np.arange(CUM_BLOCK)
    tri = jnp.asarray((r[:, None] >= r[None, :]) & (r[:, None] // GLA_CHUNK == r[None, :] // GLA_CHUNK), BF16)
    row_h = np.arange(256) // V_DIM_G
    col_h = np.arange(128) // QK_DIM_G
    bdt = jnp.asarray(row_h[:, None] == col_h[None, :], F32)
    return tri, bdt, _block_diag_ones(256, V_DIM_G)


def kernel(x, w_in, w_alpha, b_alpha, mix_scale, w_out, ln1_g, ln1_b, w_up, conv_w, conv_b, w_down, ln2_g, ln2_b):
    b, seq, d_model = x.shape
    assert (d_model, w_in.shape[0]) == (D_MODEL, DEPTH)
    assert seq % max(ROW_TILE, FFN_TILE, INPROJ_TILE) == 0
    assert (seq // DILATIONS[2]) % ATT_BLK == 0 and seq // DILATIONS[2] >= 2 * ATT_BLK
    ca, sa = _rope_tables(seq, HEAD_DIM_A)
    cr, sr = _rope_tables(seq, QK_DIM_R)
    ret_tabs = _retention_tables()
    gla_tabs = _gla_tables()
    ones_a = _block_diag_ones(LANES, HEAD_DIM_A)
    pad_cols = 3 * WIDTH_A + 768 + 768 + LANES - w_in.shape[2]

    for l in range(DEPTH):
        w_all = jnp.pad(w_in[l], ((0, 0), (0, pad_cols))).astype(BF16)
        w_al = jnp.pad(w_alpha[l], ((0, LANES - GLA_LOW_RANK), (0, 0))).astype(BF16)
        (qkv1, qkv4, qkv16, r3, g3, la3) = _inproj(
            x, w_all, w_al, b_alpha[l][None, :], ca, sa, cr, sr)

        outs, lses = [], []
        for d, qkv in zip(DILATIONS, (qkv1, qkv4, qkv16)):
            o, lse = _attention(qkv.reshape(b * d, seq // d, 3 * WIDTH_A))
            if d > 1:
                o = o.reshape(b, d, seq // d, WIDTH_A)
                lse = lse.reshape(b, d, seq // d, LANES)
            outs.append(o)
            lses.append(lse)

        yr, yg = _linear_mixers(r3, g3, la3, ret_tabs, gla_tabs)

        x = _mix_out(x, outs[0], outs[1], outs[2], lses[0], lses[1], lses[2], yr, yg,
                     mix_scale[l][None, :], w_out[l].astype(BF16), ln1_g[l][None, :], ln1_b[l][None, :], ones_a)

        wu = w_up[l].astype(BF16)
        wup_c = jnp.concatenate(
            [wu[:, :D_FF].reshape(D_MODEL, N_FF_CHUNKS, FF_CHUNK), wu[:, D_FF:].reshape(D_MODEL, N_FF_CHUNKS, FF_CHUNK)],
            axis=2).transpose(1, 0, 2)
        taps = jnp.concatenate([conv_w[l], conv_b[l][None, :], jnp.zeros((4, 2 * D_FF), F32)], axis=0)
        cw_c = jnp.concatenate(
            [taps[:, :D_FF].reshape(8, N_FF_CHUNKS, FF_CHUNK), taps[:, D_FF:].reshape(8, N_FF_CHUNKS, FF_CHUNK)],
            axis=2).transpose(1, 0, 2)
        wdn_c = w_down[l].astype(BF16).reshape(N_FF_CHUNKS, FF_CHUNK, D_MODEL)
        x = _ffn(x, wup_c, cw_c, wdn_c, ln2_g[l][None, :], ln2_b[l][None, :])
    return x
```
